```python
import math
import jax, jax.numpy as jnp
from jax import lax
import numpy as np

D_MODEL = 1024
BATCH = 16
SEQ = 256
DEPTH = 4
DEC_BATCH = 8
DEC_SEQ = 2048
PAST_LEN = 256

GRID_W = 64
N_MIXERS = 3
N_A_LAYERS = (DEPTH + 2) // 3
N_D_LAYERS = (DEPTH + 1) // 3
N_G_LAYERS = DEPTH // 3
A_HEADS = 16
A_KV = 4
A_REP = A_HEADS // A_KV
A_HD = 64
D_HEADS = 8
D_HD = 64
D_VD = 2 * D_HD
D_W = D_HEADS * 2 * D_HD
G_DIM = 1024
G_GROUPS = 8
G_GDIM = G_DIM // G_GROUPS
G_CHUNK = 128
P_HEADS = 8
N_KEYS = 128
N_EXPERTS = N_KEYS * N_KEYS
P_KEY_DIM = 128
P_HALF = P_KEY_DIM // 2
P_TOPK = 16
TOK_BLOCK = 128
Q_BLOCK = 128
ROPE_THETA = 10000.0
DN_ALPHA = (2 * DEPTH) ** 0.25
DN_BETA = (8 * DEPTH) ** -0.25
EPS = 1e-6

kernel_name = "hybrid_diffusion_gqa_diffattn_gmlp_peer_step"


def layer_norm(x, g, b):
    xf = x.astype(jnp.float32)
    mu = jnp.mean(xf, -1, keepdims=True)
    var = jnp.mean(jnp.square(xf - mu), -1, keepdims=True)
    return ((xf - mu) * lax.rsqrt(var + EPS) * g + b).astype(x.dtype)


def rms_norm(x, g):
    xf = x.astype(jnp.float32)
    return (xf * lax.rsqrt(jnp.mean(xf * xf, -1, keepdims=True) + EPS) * g).astype(x.dtype)


def axial_rope(n_tok, hd):
    rows = n_tok // GRID_W
    row = jnp.repeat(jnp.arange(rows), GRID_W).astype(jnp.float32)
    col = jnp.tile(jnp.arange(GRID_W), rows).astype(jnp.float32)
    n_freq = hd // 4
    inv = ROPE_THETA ** (-jnp.arange(n_freq, dtype=jnp.float32) / n_freq)
    ang = jnp.concatenate([row[:, None] * inv, col[:, None] * inv], -1)
    return jnp.cos(ang), jnp.sin(ang)


def apply_rope(x, cos, sin):
    shp = (1, cos.shape[0]) + (1,) * (x.ndim - 3) + (cos.shape[1],)
    c = cos.reshape(shp).astype(x.dtype)
    s = sin.reshape(shp).astype(x.dtype)
    xp = x.reshape(x.shape[:-1] + (-1, 2))
    x0, x1 = xp[..., 0], xp[..., 1]
    return jnp.stack([x0 * c - x1 * s, x0 * s + x1 * c], -1).reshape(x.shape)


def map_query_blocks(fn, q):
    b, sq = q.shape[:2]
    nb = sq // Q_BLOCK
    qb = jnp.moveaxis(q.reshape((b, nb, Q_BLOCK) + q.shape[2:]), 1, 0)
    ob = jnp.moveaxis(lax.map(fn, qb), 0, 1)
    return ob.reshape((b, sq) + ob.shape[3:])


def modulation(cond, w, b):
    m = (jax.nn.silu(cond) @ w + b).reshape(cond.shape[0], 6, D_MODEL)
    return [m[:, k, None, :] for k in range(6)]


def mixer_gqa(h, w_qkv, qn, kn, w_o, rope, ctx_kv):
    b, s, _ = h.shape
    qkv = h @ w_qkv
    q = qkv[..., :A_HEADS * A_HD].reshape(b, s, A_KV, A_REP, A_HD)
    k = qkv[..., A_HEADS * A_HD:(A_HEADS + A_KV) * A_HD].reshape(b, s, A_KV, A_HD)
    v = qkv[..., (A_HEADS + A_KV) * A_HD:].reshape(b, s, A_KV, A_HD)
    q = rms_norm(q, qn)
    k = rms_norm(k, kn)
    if rope is None:
        k_all, v_all = k, v
    else:
        q = apply_rope(q, *rope)
        k_all = jnp.concatenate([apply_rope(k, *rope), ctx_kv[0]], 1)
        v_all = jnp.concatenate([v, ctx_kv[1]], 1)
    scale = A_HD ** -0.5

    def blk(qb):
        sc = jnp.einsum('bqgrd,bkgd->bgrqk', qb, k_all).astype(jnp.float32) * scale
        p = jax.nn.softmax(sc, -1).astype(v_all.dtype)
        return jnp.einsum('bgrqk,bkgd->bqgrd', p, v_all)

    o = map_query_blocks(blk, q)
    return o.reshape(b, s, A_HEADS * A_HD) @ w_o, k, v


def mixer_diff(h, w_qkv, lq1, lk1, lq2, lk2, sub_g, w_o, lam_init, rope, ctx_kv):
    b, s, _ = h.shape
    qkv = h @ w_qkv
    q = qkv[..., :D_W].reshape(b, s, D_HEADS, 2, D_HD)
    k = qkv[..., D_W:2 * D_W].reshape(b, s, D_HEADS, 2, D_HD)
    v = qkv[..., 2 * D_W:].reshape(b, s, D_HEADS, D_VD)
    lam = (jnp.exp(jnp.sum(lq1 * lk1).astype(jnp.float32))
           - jnp.exp(jnp.sum(lq2 * lk2).astype(jnp.float32)) + lam_init)
    if rope is None:
        k_all, v_all = k, v
    else:
        q = apply_rope(q, *rope)
        k_all = jnp.concatenate([apply_rope(k, *rope), ctx_kv[0]], 1)
        v_all = jnp.concatenate([v, ctx_kv[1]], 1)
    scale = D_HD ** -0.5

    def blk(qb):
        sc = jnp.einsum('bqhjd,bkhjd->bhjqk', qb, k_all).astype(jnp.float32) * scale
        p = jax.nn.softmax(sc, -1)
        a = (p[:, :, 0] - lam * p[:, :, 1]).astype(v_all.dtype)
        return jnp.einsum('bhqk,bkhe->bqhe', a, v_all)

    o = map_query_blocks(blk, q)
    o = rms_norm(o, sub_g) * (1.0 - lam_init)
    return o.reshape(b, s, D_HEADS * D_VD) @ w_o, k, v


def mixer_gmlp(h, w_in, b_in, lnv_g, lnv_b, w_s, b_s, w_out):
    b, s, _ = h.shape
    z = jax.nn.gelu(h @ w_in + b_in)
    u, v = z[..., :G_DIM], z[..., G_DIM:]
    v = layer_norm(v, lnv_g, lnv_b)
    vc = v.reshape(b, s // G_CHUNK, G_CHUNK, G_GROUPS, G_GDIM)
    vm = jnp.einsum('gpq,bnqgc->bnpgc', w_s, vc) + b_s.T[None, None, :, :, None]
    return (u * vm.reshape(b, s, G_DIM)) @ w_out


def peer(h, w_pq, sub_keys, e_u, e_v):
    b, s, d = h.shape
    hb = h.reshape((b * s) // TOK_BLOCK, TOK_BLOCK, d)

    def blk(xb):
        q = (xb @ w_pq).reshape(TOK_BLOCK, P_HEADS, 2, P_HALF)
        sc = jnp.einsum('thjd,hjnd->thjn', q, sub_keys).astype(jnp.float32)
        s1, i1 = lax.top_k(sc[:, :, 0], P_TOPK)
        s2, i2 = lax.top_k(sc[:, :, 1], P_TOPK)
        cand_s = (s1[..., :, None] + s2[..., None, :]).reshape(TOK_BLOCK, P_HEADS, P_TOPK * P_TOPK)
        cand_i = (i1[..., :, None] * N_KEYS + i2[..., None, :]).reshape(TOK_BLOCK, P_HEADS, P_TOPK * P_TOPK)
        top_s, pos = lax.top_k(cand_s, P_TOPK)
        eidx = jnp.take_along_axis(cand_i, pos, -1)
        g = jax.nn.softmax(top_s, -1)
        act = jax.nn.gelu(jnp.einsum('thkd,td->thk', e_u[eidx], xb).astype(jnp.float32))
        w = (g * act).astype(xb.dtype)
        return jnp.einsum('thk,thkd->td', w, e_v[eidx])

    return lax.map(blk, hb).reshape(b, s, d)


def setup_inputs(seed: int = 0) -> dict:
    key = jax.random.key(seed)
    ks = iter(jax.random.split(key, 40))

    def nrm(shape, scale):
        return jax.random.normal(next(ks), shape, jnp.float32) * scale

    def gain(shape):
        return 1.0 + nrm(shape, 0.02)

    d = D_MODEL
    return {
        "x_prompt": nrm((BATCH, SEQ, d), 1.0),
        "x_sample": nrm((DEC_BATCH, DEC_SEQ, d), 1.0),
        "cache_a_k": nrm((DEC_BATCH, N_A_LAYERS, PAST_LEN, A_KV, A_HD), 1.0),
        "cache_a_v": nrm((DEC_BATCH, N_A_LAYERS, PAST_LEN, A_KV, A_HD), 1.0),
        "cache_d_k": nrm((DEC_BATCH, N_D_LAYERS, PAST_LEN, D_HEADS, 2, D_HD), 1.0),
        "cache_d_v": nrm((DEC_BATCH, N_D_LAYERS, PAST_LEN, D_HEADS, D_VD), 1.0),
        "c": nrm((DEC_BATCH, d), 1.0),
        "c_ctx": nrm((d,), 1.0),
        "w_mod": nrm((DEPTH, d, 6 * d), 0.5 * d ** -0.5),
        "b_mod": nrm((DEPTH, 6 * d), 0.01),
        "ln_g": gain((DEPTH, 2, d)),
        "ln_b": nrm((DEPTH, 2, d), 0.01),
        "a_w_qkv": nrm((N_A_LAYERS, d, (A_HEADS + 2 * A_KV) * A_HD), d ** -0.5),
        "a_q_norm": gain((N_A_LAYERS, A_HD)),
        "a_k_norm": gain((N_A_LAYERS, A_HD)),
        "a_w_o": nrm((N_A_LAYERS, A_HEADS * A_HD, d), DN_BETA * (A_HEADS * A_HD) ** -0.5),
        "d_w_qkv": nrm((N_D_LAYERS, d, 3 * D_W), d ** -0.5),
        "d_lambda_q1": nrm((N_D_LAYERS, D_HD), 0.1),
        "d_lambda_k1": nrm((N_D_LAYERS, D_HD), 0.1),
        "d_lambda_q2": nrm((N_D_LAYERS, D_HD), 0.1),
        "d_lambda_k2": nrm((N_D_LAYERS, D_HD), 0.1),
        "d_sub_norm": gain((N_D_LAYERS, D_VD)),
        "d_w_o": nrm((N_D_LAYERS, D_HEADS * D_VD, d), DN_BETA * (D_HEADS * D_VD) ** -0.5),
        "g_w_in": nrm((N_G_LAYERS, d, 2 * G_DIM), d ** -0.5),
        "g_b_in": nrm((N_G_LAYERS, 2 * G_DIM), 0.01),
        "g_v_norm_g": gain((N_G_LAYERS, G_DIM)),
        "g_v_norm_b": nrm((N_G_LAYERS, G_DIM), 0.01),
        "g_w_s": nrm((N_G_LAYERS, G_GROUPS, G_CHUNK, G_CHUNK), G_CHUNK ** -0.5),
        "g_b_s": gain((N_G_LAYERS, G_GROUPS, G_CHUNK)),
        "g_w_out": nrm((N_G_LAYERS, G_DIM, d), DN_BETA * G_DIM ** -0.5),
        "p_w_q": nrm((DEPTH, d, P_HEADS * P_KEY_DIM), d ** -0.5),
        "p_sub_keys": nrm((DEPTH, P_HEADS, 2, N_KEYS, P_HALF), P_HALF ** -0.5),
        "p_expert_u": nrm((DEPTH, N_EXPERTS, d), d ** -0.5),
        "p_expert_v": nrm((DEPTH, N_EXPERTS, d), DN_BETA),
    }


def reference(x_prompt, x_sample, cache_a_k, cache_a_v, cache_d_k, cache_d_v, c, c_ctx,
              w_mod, b_mod, ln_g, ln_b,
              a_w_qkv, a_q_norm, a_k_norm, a_w_o,
              d_w_qkv, d_lambda_q1, d_lambda_k1, d_lambda_q2, d_lambda_k2, d_sub_norm, d_w_o,
              g_w_in, g_b_in, g_v_norm_g, g_v_norm_b, g_w_s, g_b_s, g_w_out,
              p_w_q, p_sub_keys, p_expert_u, p_expert_v):
    rope = axial_rope(x_sample.shape[1], A_HD)
    xp, xs = x_prompt, x_sample
    cond_ctx = c_ctx[None, :]
    ak, av, dk, dv = [], [], [], []
    for i in range(DEPTH):
        kind, j = i % N_MIXERS, i // N_MIXERS
        mp = modulation(cond_ctx, w_mod[i], b_mod[i])
        ms = modulation(c, w_mod[i], b_mod[i])
        hp = xp * (1.0 + mp[1]) + mp[0]
        hs = xs * (1.0 + ms[1]) + ms[0]
        if kind == 0:
            op, kp, vp = mixer_gqa(hp, a_w_qkv[j], a_q_norm[j], a_k_norm[j], a_w_o[j], None, None)
            os_, _, _ = mixer_gqa(hs, a_w_qkv[j], a_q_norm[j], a_k_norm[j], a_w_o[j], rope,
                                  (cache_a_k[:, j], cache_a_v[:, j]))
            ak.append(kp)
            av.append(vp)
        elif kind == 1:
            lam_init = 0.8 - 0.6 * math.exp(-0.3 * i)
            dargs = (d_w_qkv[j], d_lambda_q1[j], d_lambda_k1[j], d_lambda_q2[j], d_lambda_k2[j],
                     d_sub_norm[j], d_w_o[j], lam_init)
            op, kp, vp = mixer_diff(hp, *dargs, None, None)
            os_, _, _ = mixer_diff(hs, *dargs, rope, (cache_d_k[:, j], cache_d_v[:, j]))
            dk.append(kp)
            dv.append(vp)
        else:
            gargs = (g_w_in[j], g_b_in[j], g_v_norm_g[j], g_v_norm_b[j], g_w_s[j], g_b_s[j], g_w_out[j])
            op = mixer_gmlp(hp, *gargs)
            os_ = mixer_gmlp(hs, *gargs)
        xp = layer_norm(DN_ALPHA * xp + mp[2] * op, ln_g[i, 0], ln_b[i, 0])
        xs = layer_norm(DN_ALPHA * xs + ms[2] * os_, ln_g[i, 0], ln_b[i, 0])
        hp = xp * (1.0 + mp[4]) + mp[3]
        hs = xs * (1.0 + ms[4]) + ms[3]
        fp = peer(hp, p_w_q[i], p_sub_keys[i], p_expert_u[i], p_expert_v[i])
        fs = peer(hs, p_w_q[i], p_sub_keys[i], p_expert_u[i], p_expert_v[i])
        xp = layer_norm(DN_ALPHA * xp + mp[5] * fp, ln_g[i, 1], ln_b[i, 1])
        xs = layer_norm(DN_ALPHA * xs + ms[5] * fs, ln_g[i, 1], ln_b[i, 1])
    new_cache_a_k = jnp.stack(ak, axis=1)
    new_cache_a_v = jnp.stack(av, axis=1)
    new_cache_d_k = jnp.stack(dk, axis=1)
    new_cache_d_v = jnp.stack(dv, axis=1)
    return (xp, xs, new_cache_a_k, new_cache_a_v, new_cache_d_k, new_cache_d_v)
```

```python
import functools
import math

import numpy as np
import jax
import jax.numpy as jnp
from jax import lax
from jax.experimental import pallas as pl
from jax.experimental.pallas import tpu as pltpu

F32 = jnp.float32
BF16 = jnp.bfloat16

D_MODEL = 1024
DEPTH = 4
GRID_W = 64
N_MIXERS = 3
A_HEADS = 16
A_KV = 4
A_REP = A_HEADS // A_KV
A_HD = 64
D_HEADS = 8
D_HD = 64
D_VD = 2 * D_HD
D_W = D_HEADS * 2 * D_HD
G_DIM = 1024
G_GROUPS = 8
G_CHUNK = 128
P_HEADS = 8
N_KEYS = 128
P_HALF = 64
P_TOPK = 16
ROPE_THETA = 10000.0
DN_ALPHA = (2 * DEPTH) ** 0.25
EPS = 1e-6

LANES = 128
NEG_INF = float("-inf")
VMEM_LIMIT = 56 * 1024 * 1024

TM_QKV = 256
TM_PROJ = 512
TQ_ATTN = 256
TM_GMLP = 256
TM_TOPK = 512
TM_PEER = 512
TE_PEER = 1024


def _mm(a, b):
    return jnp.dot(a, b, preferred_element_type=F32)


def _mm_nt(a, b):
    return lax.dot_general(a, b, (((1,), (1,)), ((), ())), preferred_element_type=F32)


def _split(x):
    hi = x.astype(BF16)
    lo = (x - hi.astype(F32)).astype(BF16)
    return hi, lo


def _mm3(a_hi, a_lo, b_hi, b_lo):
    return _mm(a_hi, b_hi) + _mm(a_hi, b_lo) + _mm(a_lo, b_hi)


def _layer_norm(z, g, b):
    mu = jnp.mean(z, axis=-1, keepdims=True)
    zc = z - mu
    var = jnp.mean(zc * zc, axis=-1, keepdims=True)
    return zc * lax.rsqrt(var + EPS) * g + b


def _params(sem):
    return pltpu.CompilerParams(dimension_semantics=sem, vmem_limit_bytes=VMEM_LIMIT)


def _tile_group(i, n_p_tiles, tiles_per_batch):
    return jnp.where(i < n_p_tiles, 0, 1 + (i - n_p_tiles) // tiles_per_batch)


def _tile_pos_block(i, n_p_tiles, tiles_per_batch):
    return jnp.where(i < n_p_tiles, 0, (i - n_p_tiles) % tiles_per_batch)


def _mods_kernel(c_ref, w_ref, b_ref, o_ref):
    c = c_ref[...]
    a = c * jax.nn.sigmoid(c)
    a_hi, a_lo = _split(a)
    w_hi, w_lo = _split(w_ref[0])
    o_ref[0] = _mm3(a_hi, a_lo, w_hi, w_lo) + b_ref[0]


def _modulation(cond, w_mod, b_mod):
    depth, d, n = w_mod.shape
    rows = cond.shape[0]
    tn = 1536
    return pl.pallas_call(
        _mods_kernel,
        grid=(depth, n // tn),
        in_specs=[
            pl.BlockSpec((rows, d), lambda l, j: (0, 0)),
            pl.BlockSpec((1, d, tn), lambda l, j: (l, 0, j)),
            pl.BlockSpec((1, 1, tn), lambda l, j: (l, 0, j)),
        ],
        out_specs=pl.BlockSpec((1, rows, tn), lambda l, j: (l, 0, j)),
        out_shape=jax.ShapeDtypeStruct((depth, rows, n), F32),
        compiler_params=_params(("parallel", "parallel")),
    )(cond, w_mod, b_mod.reshape(depth, 1, n))


def _swap_pairs(x):
    lane = lax.broadcasted_iota(jnp.int32, x.shape, 1)
    nxt = pltpu.roll(x, LANES - 1, 1)
    prv = pltpu.roll(x, 1, 1)
    return jnp.where((lane & 1) == 0, nxt, prv)


def _qkv_kernel(x_ref, m_ref, w_ref, bd_ref, gq_ref, gk_ref, cs_ref, sn_ref,
                q_ref, ka_ref, va_ref, kf_ref, vf_ref,
                *, nq, nk, nv, use_norm, n_p_tiles, q_scale):
    x = x_ref[...]
    h = x * (1.0 + m_ref[0, 1:2, :]) + m_ref[0, 0:1, :]
    y = _mm(h.astype(BF16), w_ref[...])
    is_lat = pl.program_id(0) >= n_p_tiles
    cs = jnp.where(is_lat, cs_ref[...], 1.0)
    sn = jnp.where(is_lat, sn_ref[...], 0.0)
    bd = bd_ref[...]

    def head_norm(yg, gain):
        s_hi, s_lo = _split(yg * yg)
        ms = _mm(s_hi, bd) + _mm(s_lo, bd)
        return yg * lax.rsqrt(ms + EPS) * gain

    def rope(yg):
        return yg * cs + _swap_pairs(yg) * sn

    for g in range(nq):
        sl = slice(g * LANES, (g + 1) * LANES)
        yg = y[:, sl]
        if use_norm:
            yg = head_norm(yg, gq_ref[:, sl])
        q_ref[:, sl] = (rope(yg) * q_scale).astype(BF16)
    for g in range(nk):
        sl = slice(g * LANES, (g + 1) * LANES)
        yg = y[:, (nq + g) * LANES:(nq + g + 1) * LANES]
        if use_norm:
            yg = head_norm(yg, gk_ref[:, sl])
        kf_ref[:, sl] = yg
        ka_ref[:, sl] = rope(yg).astype(BF16)
    for g in range(nv):
        sl = slice(g * LANES, (g + 1) * LANES)
        yg = y[:, (nq + nk + g) * LANES:(nq + nk + g + 1) * LANES]
        vf_ref[:, sl] = yg
        va_ref[:, sl] = yg.astype(BF16)


def _qkv_project(x, mods_l, w_exp, gq, gk, rope_c, rope_s, *, nq, nk, nv, use_norm, t_p, s_len, q_scale):
    t, d = x.shape
    tm = TM_QKV
    n_p = t_p // tm
    tpb = s_len // tm
    ncol = (nq + nk + nv) * LANES
    bd = np.kron(np.eye(2, dtype=np.float32), np.full((64, 64), 1.0 / 64.0, np.float32))
    bd = jnp.asarray(bd, BF16)
    grp = lambda i: (_tile_group(i, n_p, tpb), 0, 0)
    pos = lambda i: (_tile_pos_block(i, n_p, tpb), 0)
    row = lambda i: (i, 0)
    fixed = lambda i: (0, 0)
    kern = functools.partial(_qkv_kernel, nq=nq, nk=nk, nv=nv, use_norm=use_norm,
                             n_p_tiles=n_p, q_scale=q_scale)
    return pl.pallas_call(
        kern,
        grid=(t // tm,),
        in_specs=[
            pl.BlockSpec((tm, d), row),
            pl.BlockSpec((1, 6, d), grp),
            pl.BlockSpec((d, ncol), fixed),
            pl.BlockSpec((LANES, LANES), fixed),
            pl.BlockSpec((1, nq * LANES), fixed),
            pl.BlockSpec((1, nk * LANES), fixed),
            pl.BlockSpec((tm, LANES), pos),
            pl.BlockSpec((tm, LANES), pos),
        ],
        out_specs=[
            pl.BlockSpec((tm, nq * LANES), row),
            pl.BlockSpec((tm, nk * LANES), row),
            pl.BlockSpec((tm, nv * LANES), row),
            pl.BlockSpec((tm, nk * LANES), row),
            pl.BlockSpec((tm, nv * LANES), row),
        ],
        out_shape=[
            jax.ShapeDtypeStruct((t, nq * LANES), BF16),
            jax.ShapeDtypeStruct((t, nk * LANES), BF16),
            jax.ShapeDtypeStruct((t, nv * LANES), BF16),
            jax.ShapeDtypeStruct((t, nk * LANES), F32),
            jax.ShapeDtypeStruct((t, nv * LANES), F32),
        ],
        compiler_params=_params(("parallel",)),
    )(x, mods_l, w_exp, bd, gq, gk, rope_c, rope_s)


def _softmax_pv(qh, k_all, v_all):
    scores = [_mm_nt(qh, k) for k in k_all]
    m = scores[0].max(axis=-1, keepdims=True)
    for s in scores[1:]:
        m = jnp.maximum(m, s.max(axis=-1, keepdims=True))
    o = None
    l = None
    for s, v in zip(scores, v_all):
        p = jnp.exp(s - m)
        ls = p.sum(axis=-1, keepdims=True)
        os_ = _mm(p.astype(BF16), v)
        o = os_ if o is None else o + os_
        l = ls if l is None else l + ls
    return o, l


def _gqa_attn_kernel(*refs, has_ctx):
    if has_ctx:
        q_ref, k_ref, v_ref, kc_ref, vc_ref, o_ref = refs
    else:
        q_ref, k_ref, v_ref, o_ref = refs
    for h in range(A_HEADS):
        grp = (h // A_REP) // 2
        sl = slice(h * LANES, (h + 1) * LANES)
        ks = slice(grp * LANES, (grp + 1) * LANES)
        k_all = [k_ref[:, ks]]
        v_all = [v_ref[:, ks]]
        if has_ctx:
            k_all.append(kc_ref[0, :, ks])
            v_all.append(vc_ref[0, :, ks])
        o, l = _softmax_pv(q_ref[:, sl], k_all, v_all)
        o_ref[:, sl] = (o / l).astype(BF16)


def _diff_attn_kernel(*refs, has_ctx, lam_init):
    if has_ctx:
        lv_ref, sg_ref, q_ref, k_ref, v_ref, kc_ref, vc_ref, o_ref = refs
    else:
        lv_ref, sg_ref, q_ref, k_ref, v_ref, o_ref = refs
    lv = lv_ref[...]
    lam = (jnp.exp(jnp.sum(lv[0:1] * lv[1:2], axis=-1, keepdims=True))
           - jnp.exp(jnp.sum(lv[2:3] * lv[3:4], axis=-1, keepdims=True)) + lam_init)
    for h in range(D_HEADS):
        ks = slice(h * LANES, (h + 1) * LANES)
        k_all = [k_ref[:, ks]]
        v_all = [v_ref[:, ks]]
        if has_ctx:
            k_all.append(kc_ref[0, :, ks])
            v_all.append(vc_ref[0, :, ks])
        outs = []
        for j in range(2):
            sl = slice((2 * h + j) * LANES, (2 * h + j + 1) * LANES)
            o, l = _softmax_pv(q_ref[:, sl], k_all, v_all)
            outs.append(o / l)
        o = outs[0] - lam * outs[1]
        ms = jnp.mean(o * o, axis=-1, keepdims=True)
        o = o * lax.rsqrt(ms + EPS) * sg_ref[...] * (1.0 - lam_init)
        o_ref[:, ks] = o.astype(BF16)


def _attention(kern, extra, q, k, v, ctx_k, ctx_v, *, n_out, t_p, seq_p, batch_s, seq_s):
    nqc = q.shape[1]
    nkc = k.shape[1]
    extra_specs = [pl.BlockSpec(a.shape, lambda *_: (0, 0)) for a in extra]
    n_seq_p = t_p // seq_p
    o_p = pl.pallas_call(
        functools.partial(kern, has_ctx=False),
        grid=(n_seq_p,),
        in_specs=extra_specs + [
            pl.BlockSpec((seq_p, nqc), lambda b: (b, 0)),
            pl.BlockSpec((seq_p, nkc), lambda b: (b, 0)),
            pl.BlockSpec((seq_p, nkc), lambda b: (b, 0)),
        ],
        out_specs=pl.BlockSpec((seq_p, n_out), lambda b: (b, 0)),
        out_shape=jax.ShapeDtypeStruct((t_p, n_out), BF16),
        compiler_params=_params(("parallel",)),
    )(*extra, q, k, v)
    tq = TQ_ATTN
    nqt = seq_s // tq
    assert t_p % seq_s == 0 and t_p % tq == 0
    q_off = t_p // tq
    k_off = t_p // seq_s
    past = ctx_k.shape[1]
    o_s = pl.pallas_call(
        functools.partial(kern, has_ctx=True),
        grid=(batch_s, nqt),
        in_specs=extra_specs + [
            pl.BlockSpec((tq, nqc), lambda b, i: (q_off + b * nqt + i, 0)),
            pl.BlockSpec((seq_s, nkc), lambda b, i: (k_off + b, 0)),
            pl.BlockSpec((seq_s, nkc), lambda b, i: (k_off + b, 0)),
            pl.BlockSpec((1, past, nkc), lambda b, i: (b, 0, 0)),
            pl.BlockSpec((1, past, nkc), lambda b, i: (b, 0, 0)),
        ],
        out_specs=pl.BlockSpec((tq, n_out), lambda b, i: (b * nqt + i, 0)),
        out_shape=jax.ShapeDtypeStruct((batch_s * seq_s, n_out), BF16),
        compiler_params=_params(("parallel", "arbitrary")),
    )(*extra, q, k, v, ctx_k, ctx_v)
    return jnp.concatenate([o_p, o_s], axis=0)


def _proj_ln_kernel(o_ref, w_ref, x_ref, m_ref, g_ref, b_ref, y_ref, *, gate_idx):
    y = _mm(o_ref[...], w_ref[...])
    z = DN_ALPHA * x_ref[...] + m_ref[0, gate_idx:gate_idx + 1, :] * y
    y_ref[...] = _layer_norm(z, g_ref[...], b_ref[...])


def _proj_ln(o, w, x, mods_l, ln_g, ln_b, *, gate_idx, t_p, s_len):
    t, d = x.shape
    tm = TM_PROJ
    n_p = t_p // tm
    tpb = s_len // tm
    kin = o.shape[1]
    row = lambda i: (i, 0)
    fixed = lambda i: (0, 0)
    return pl.pallas_call(
        functools.partial(_proj_ln_kernel, gate_idx=gate_idx),
        grid=(t // tm,),
        in_specs=[
            pl.BlockSpec((tm, kin), row),
            pl.BlockSpec((kin, d), fixed),
            pl.BlockSpec((tm, d), row),
            pl.BlockSpec((1, 6, d), lambda i: (_tile_group(i, n_p, tpb), 0, 0)),
            pl.BlockSpec((1, d), fixed),
            pl.BlockSpec((1, d), fixed),
        ],
        out_specs=pl.BlockSpec((tm, d), row),
        out_shape=jax.ShapeDtypeStruct((t, d), F32),
        compiler_params=_params(("parallel",)),
    )(o, w, x, mods_l, ln_g.reshape(1, d), ln_b.reshape(1, d))


def _gmlp_kernel(x_ref, m_ref, win_ref, bin_ref, vg_ref, vb_ref, ws_ref, bs_ref, wout_ref,
                 g_ref, b_ref, y_ref, uv_sc):
    x = x_ref[...]
    h = x * (1.0 + m_ref[0, 1:2, :]) + m_ref[0, 0:1, :]
    z = jax.nn.gelu(_mm(h.astype(BF16), win_ref[...]) + bin_ref[...])
    u = z[:, :G_DIM]
    v = _layer_norm(z[:, G_DIM:], vg_ref[...], vb_ref[...])
    tm = x.shape[0]
    for c in range(tm // G_CHUNK):
        rs = slice(c * G_CHUNK, (c + 1) * G_CHUNK)
        for g in range(G_GROUPS):
            cs = slice(g * LANES, (g + 1) * LANES)
            vm = _mm(ws_ref[g], v[rs, cs].astype(BF16)) + bs_ref[g]
            uv_sc[rs, cs] = (u[rs, cs] * vm).astype(BF16)
    y = _mm(uv_sc[...], wout_ref[...])
    zz = DN_ALPHA * x + m_ref[0, 2:3, :] * y
    y_ref[...] = _layer_norm(zz, g_ref[...], b_ref[...])


def _gmlp(x, mods_l, w_in, b_in, vg, vb, w_s, b_s, w_out, ln_g, ln_b, *, t_p, s_len):
    t, d = x.shape
    tm = TM_GMLP
    n_p = t_p // tm
    tpb = s_len // tm
    row = lambda i: (i, 0)
    fixed = lambda i: (0, 0)
    fixed3 = lambda i: (0, 0, 0)
    b_s_b = jnp.broadcast_to(b_s[:, :, None], (G_GROUPS, G_CHUNK, LANES))
    return pl.pallas_call(
        _gmlp_kernel,
        grid=(t // tm,),
        in_specs=[
            pl.BlockSpec((tm, d), row),
            pl.BlockSpec((1, 6, d), lambda i: (_tile_group(i, n_p, tpb), 0, 0)),
            pl.BlockSpec((d, 2 * G_DIM), fixed),
            pl.BlockSpec((1, 2 * G_DIM), fixed),
            pl.BlockSpec((1, G_DIM), fixed),
            pl.BlockSpec((1, G_DIM), fixed),
            pl.BlockSpec((G_GROUPS, G_CHUNK, G_CHUNK), fixed3),
            pl.BlockSpec((G_GROUPS, G_CHUNK, LANES), fixed3),
            pl.BlockSpec((G_DIM, d), fixed),
            pl.BlockSpec((1, d), fixed),
            pl.BlockSpec((1, d), fixed),
        ],
        out_specs=pl.BlockSpec((tm, d), row),
        out_shape=jax.ShapeDtypeStruct((t, d), F32),
        scratch_shapes=[pltpu.VMEM((tm, G_DIM), BF16)],
        compiler_params=_params(("parallel",)),
    )(x, mods_l, w_in.astype(BF16), b_in.reshape(1, -1), vg.reshape(1, -1), vb.reshape(1, -1),
      w_s.astype(BF16), b_s_b, w_out.astype(BF16), ln_g.reshape(1, d), ln_b.reshape(1, d))


def _top16(sc):
    n, L = sc.shape
    pos = lax.broadcasted_iota(jnp.int32, (n, L), 0).astype(F32)
    slot = lax.broadcasted_iota(jnp.int32, (P_TOPK, L), 0)

    def body(k, carry):
        cur, rank, vals = carry
        m = jnp.max(cur, axis=0, keepdims=True)
        idx = jnp.min(jnp.where(cur == m, pos, float(n)), axis=0, keepdims=True)
        hit = pos == idx
        cur = jnp.where(hit, NEG_INF, cur)
        rank = jnp.where(hit, k.astype(F32), rank)
        vals = jnp.where(slot == k, m, vals)
        return cur, rank, vals

    init = (sc, jnp.full((n, L), 64.0, F32), jnp.zeros((P_TOPK, L), F32))
    _, rank, vals = lax.fori_loop(0, P_TOPK, body, init)
    return vals, rank


_CAND_BLOCKS = ((0, 0, 8), (0, 8, 8), (1, 0, 8), (2, 0, 5), (3, 0, 4), (4, 0, 3), (5, 0, 2), (6, 0, 2), (7, 0, 2))


def _pair_select(v1, v2):
    L = v1.shape[1]
    r8 = lax.broadcasted_iota(jnp.int32, (8, L), 0).astype(F32)
    blocks = []
    codes = []
    for a, b0, nb in _CAND_BLOCKS:
        c = v1[a:a + 1] + v2[b0:b0 + 8]
        if nb < 8:
            c = jnp.where(r8 < float(nb), c, NEG_INF)
        blocks.append(c)
        codes.append(r8 + float(a * P_TOPK + b0))
    blocks.append(v1[8:16] + v2[0:1])
    codes.append((r8 + 8.0) * float(P_TOPK))
    cand0 = jnp.concatenate(blocks, axis=0)
    code = jnp.concatenate(codes, axis=0)

    def body(k, carry):
        cur, sel = carry
        m = jnp.max(cur, axis=0, keepdims=True)
        idx = jnp.min(jnp.where(cur == m, code, 1e9), axis=0, keepdims=True)
        hit = code == idx
        return jnp.where(hit, NEG_INF, cur), jnp.where(hit, 1.0, sel)

    _, sel = lax.fori_loop(0, P_TOPK, body, (cand0, jnp.zeros_like(cand0)))
    top = v1[0:1] + v2[0:1]
    z = jnp.sum(jnp.where(sel > 0.0, jnp.exp(cand0 - top), 0.0), axis=0, keepdims=True)
    cnt_lo = jnp.zeros((8, L), F32)
    starts = (0, 16, 24, 32, 40, 48, 56, 64, 72)
    for a in range(8):
        n_a = jnp.sum(sel[starts[a]:starts[a + 1]], axis=0, keepdims=True)
        cnt_lo = jnp.where(r8 == float(a), n_a, cnt_lo)
    count = jnp.concatenate([cnt_lo, sel[72:80]], axis=0)
    return count, z


def _peer_topk_kernel(x_ref, m_ref, wh_ref, wl_ref, kh_ref, kl_ref,
                      h2t_ref, ni_ref, ai_ref, rj_ref, bj_ref, qt_sc):
    hd = pl.program_id(1)

    @pl.when(hd == 0)
    def _():
        x = x_ref[...]
        h2 = x * (1.0 + m_ref[0, 4:5, :]) + m_ref[0, 3:4, :]
        h_hi, h_lo = _split(h2.T)
        h2t_ref[...] = h_hi
        qt_sc[...] = _mm3(wh_ref[...], wl_ref[...], h_hi, h_lo)

    qh = qt_sc[pl.ds(pl.multiple_of(hd * (2 * P_HALF), 2 * P_HALF), 2 * P_HALF), :]
    q1_hi, q1_lo = _split(qh[:P_HALF])
    q2_hi, q2_lo = _split(qh[P_HALF:])
    sc1 = _mm3(kh_ref[0], kl_ref[0], q1_hi, q1_lo)
    sc2 = _mm3(kh_ref[1], kl_ref[1], q2_hi, q2_lo)
    tm = sc1.shape[1]
    for c in range(tm // LANES):
        ls = slice(c * LANES, (c + 1) * LANES)
        s1 = sc1[:, ls]
        s2 = sc2[:, ls]
        v1, r1 = _top16(s1)
        v2, r2 = _top16(s2)
        count, z = _pair_select(v1, v2)
        npos = jnp.zeros_like(r1)
        for a in range(P_TOPK):
            npos = jnp.where(r1 == float(a), count[a:a + 1], npos)
        ni_ref[0, :, ls] = npos
        ai_ref[0, :, ls] = jnp.exp(s1 - v1[0:1]) / z
        rj_ref[0, :, ls] = r2.astype(BF16)
        bj_ref[0, :, ls] = jnp.exp(s2 - v2[0:1]).astype(BF16)


def _peer_topk(x, mods_l, wq_t_hi, wq_t_lo, keys_hi, keys_lo, *, t_p, s_len):
    t, d = x.shape
    tm = TM_TOPK
    n_p = t_p // tm
    tpb = s_len // tm
    tab = lambda dt: jax.ShapeDtypeStruct((P_HEADS, N_KEYS, t), dt)
    tab_spec = pl.BlockSpec((1, N_KEYS, tm), lambda i, h: (h, 0, i))
    return pl.pallas_call(
        _peer_topk_kernel,
        grid=(t // tm, P_HEADS),
        in_specs=[
            pl.BlockSpec((tm, d), lambda i, h: (i, 0)),
            pl.BlockSpec((1, 6, d), lambda i, h: (_tile_group(i, n_p, tpb), 0, 0)),
            pl.BlockSpec((d, d), lambda i, h: (0, 0)),
            pl.BlockSpec((d, d), lambda i, h: (0, 0)),
            pl.BlockSpec((2, N_KEYS, P_HALF), lambda i, h: (h, 0, 0)),
            pl.BlockSpec((2, N_KEYS, P_HALF), lambda i, h: (h, 0, 0)),
        ],
        out_specs=[pl.BlockSpec((d, tm), lambda i, h: (0, i)), tab_spec, tab_spec, tab_spec, tab_spec],
        out_shape=[jax.ShapeDtypeStruct((d, t), BF16), tab(F32), tab(F32), tab(BF16), tab(BF16)],
        scratch_shapes=[pltpu.VMEM((d, tm), F32)],
        compiler_params=_params(("parallel", "arbitrary")),
    )(x, mods_l, wq_t_hi, wq_t_lo, keys_hi, keys_lo)


def _peer_dense_kernel(h2t_ref, eu_ref, evt_ref, ni_ref, ai_ref, rj_ref, bj_ref,
                       x_ref, m_ref, g_ref, b_ref, y_ref, acc_sc, wt_sc, *, n_e):
    e = pl.program_id(1)

    @pl.when(e == 0)
    def _():
        acc_sc[...] = jnp.zeros_like(acc_sc)

    act_t = _mm(eu_ref[...], h2t_ref[...])
    te = act_t.shape[0]
    for il in range(te // N_KEYS):
        rs = slice(il * N_KEYS, (il + 1) * N_KEYS)
        gate = None
        for h in range(P_HEADS):
            cnt = ni_ref[h, il:il + 1, :].astype(BF16)
            a_i = ai_ref[h, il:il + 1, :].astype(BF16)
            gh = jnp.where(rj_ref[h] < cnt, bj_ref[h] * a_i, jnp.zeros((), BF16))
            gate = gh if gate is None else gate + gh
        wt_sc[rs, :] = (gate.astype(F32) * jax.nn.gelu(act_t[rs, :])).astype(BF16)
    acc_sc[...] += _mm(evt_ref[...], wt_sc[...])

    @pl.when(e == n_e - 1)
    def _():
        f = acc_sc[...].T
        z = DN_ALPHA * x_ref[...] + m_ref[0, 5:6, :] * f
        y_ref[...] = _layer_norm(z, g_ref[...], b_ref[...])


def _peer_dense(h2t, e_u, e_vt, ni, ai, rj, bj, x, mods_l, ln_g, ln_b, *, t_p, s_len):
    t, d = x.shape
    tm = TM_PEER
    te = TE_PEER
    n_p = t_p // tm
    tpb = s_len // tm
    n_exp = e_u.shape[0]
    n_e = n_exp // te
    n_i = te // N_KEYS
    return pl.pallas_call(
        functools.partial(_peer_dense_kernel, n_e=n_e),
        grid=(t // tm, n_e),
        in_specs=[
            pl.BlockSpec((d, tm), lambda i, e: (0, i)),
            pl.BlockSpec((te, d), lambda i, e: (e, 0)),
            pl.BlockSpec((d, te), lambda i, e: (0, e)),
            pl.BlockSpec((P_HEADS, n_i, tm), lambda i, e: (0, e, i)),
            pl.BlockSpec((P_HEADS, n_i, tm), lambda i, e: (0, e, i)),
            pl.BlockSpec((P_HEADS, N_KEYS, tm), lambda i, e: (0, 0, i)),
            pl.BlockSpec((P_HEADS, N_KEYS, tm), lambda i, e: (0, 0, i)),
            pl.BlockSpec((tm, d), lambda i, e: (i, 0)),
            pl.BlockSpec((1, 6, d), lambda i, e: (_tile_group(i, n_p, tpb), 0, 0)),
            pl.BlockSpec((1, d), lambda i, e: (0, 0)),
            pl.BlockSpec((1, d), lambda i, e: (0, 0)),
        ],
        out_specs=pl.BlockSpec((tm, d), lambda i, e: (i, 0)),
        out_shape=jax.ShapeDtypeStruct((t, d), F32),
        scratch_shapes=[pltpu.VMEM((d, tm), F32), pltpu.VMEM((te, tm), BF16)],
        compiler_params=_params(("parallel", "arbitrary")),
    )(h2t, e_u, e_vt, ni, ai, rj, bj, x, mods_l, ln_g.reshape(1, d), ln_b.reshape(1, d))


def _rope_tables(n_tok):
    rows = n_tok // GRID_W
    row = jnp.repeat(jnp.arange(rows), GRID_W).astype(F32)
    col = jnp.tile(jnp.arange(GRID_W), rows).astype(F32)
    n_freq = A_HD // 4
    inv = ROPE_THETA ** (-jnp.arange(n_freq, dtype=F32) / n_freq)
    ang = jnp.concatenate([row[:, None] * inv, col[:, None] * inv], -1)
    cos = jnp.repeat(jnp.cos(ang), 2, axis=-1)
    sin = jnp.repeat(jnp.sin(ang), 2, axis=-1)
    sign = jnp.tile(jnp.asarray([-1.0, 1.0], F32), A_HD // 2)
    reps = LANES // A_HD
    return jnp.tile(cos, (1, reps)), jnp.tile(sin * sign, (1, reps))


def _gqa_weights(w_qkv, q_norm, k_norm, w_o):
    d = w_qkv.shape[0]
    nq = A_HEADS * A_HD
    half = (np.arange(A_HEADS) // A_REP) % 2
    sel = np.stack([half == 0, half == 1], axis=1).astype(np.float32)
    wq = w_qkv[:, :nq].reshape(d, A_HEADS, 1, A_HD) * sel[None, :, :, None]
    w_exp = jnp.concatenate([wq.reshape(d, A_HEADS * LANES), w_qkv[:, nq:]], axis=1).astype(BF16)
    gq = jnp.tile(q_norm, 2 * A_HEADS).reshape(1, -1)
    gk = jnp.tile(k_norm, A_KV).reshape(1, -1)
    wo = w_o.reshape(A_HEADS, 1, A_HD, d) * sel[:, :, None, None]
    return w_exp, gq, gk, wo.reshape(A_HEADS * LANES, d).astype(BF16)


def _diff_weights(w_qkv):
    d = w_qkv.shape[0]
    eye = np.eye(2, dtype=np.float32)
    wq = w_qkv[:, :D_W].reshape(d, D_HEADS, 1, 2, D_HD) * eye[None, None, :, :, None]
    return jnp.concatenate([wq.reshape(d, D_HEADS * 2 * LANES), w_qkv[:, D_W:]], axis=1).astype(BF16)


def _split_f32(w):
    hi = w.astype(BF16)
    return hi, (w - hi.astype(F32)).astype(BF16)


def kernel(x_prompt, x_sample, cache_a_k, cache_a_v, cache_d_k, cache_d_v, c, c_ctx, w_mod, b_mod, ln_g, ln_b, a_w_qkv, a_q_norm, a_k_norm, a_w_o, d_w_qkv, d_lambda_q1, d_lambda_k1, d_lambda_q2, d_lambda_k2, d_sub_norm, d_w_o, g_w_in, g_b_in, g_v_norm_g, g_v_norm_b, g_w_s, g_b_s, g_w_out, p_w_q, p_sub_keys, p_expert_u, p_expert_v):
    batch, seq, d = x_prompt.shape
    dec_batch, dec_seq, _ = x_sample.shape
    past = cache_a_k.shape[2]
    t_p = batch * seq
    t_s = dec_batch * dec_seq
    dims = dict(t_p=t_p, s_len=dec_seq)

    x = jnp.concatenate([x_prompt.reshape(t_p, d), x_sample.reshape(t_s, d)], axis=0)
    n_cond = -(-(1 + dec_batch) // 8) * 8
    cond = jnp.zeros((n_cond, d), F32).at[0].set(c_ctx).at[1:1 + dec_batch].set(c)
    mods = _modulation(cond, w_mod, b_mod).reshape(DEPTH, n_cond, 6, d)
    rope_c, rope_s = _rope_tables(dec_seq)

    ak, av, dk, dv = [], [], [], []
    for i in range(DEPTH):
        kind, j = i % N_MIXERS, i // N_MIXERS
        mods_l = mods[i]
        if kind == 0:
            w_exp, gq, gk, wo = _gqa_weights(a_w_qkv[j], a_q_norm[j], a_k_norm[j], a_w_o[j])
            q, k_att, v_att, k_f, v_f = _qkv_project(
                x, mods_l, w_exp, gq, gk, rope_c, rope_s, nq=A_HEADS, nk=A_KV * A_HD // LANES,
                nv=A_KV * A_HD // LANES, use_norm=True, q_scale=A_HD ** -0.5, **dims)
            ctx_k = cache_a_k[:, j].reshape(dec_batch, past, A_KV * A_HD).astype(BF16)
            ctx_v = cache_a_v[:, j].reshape(dec_batch, past, A_KV * A_HD).astype(BF16)
            o = _attention(_gqa_attn_kernel, (), q, k_att, v_att, ctx_k, ctx_v, n_out=A_HEADS * LANES,
                           t_p=t_p, seq_p=seq, batch_s=dec_batch, seq_s=dec_seq)
            ak.append(k_f[:t_p].reshape(batch, seq, A_KV, A_HD))
            av.append(v_f[:t_p].reshape(batch, seq, A_KV, A_HD))
        elif kind == 1:
            lam_init = 0.8 - 0.6 * math.exp(-0.3 * i)
            w_exp = _diff_weights(d_w_qkv[j])
            q, k_att, v_att, k_f, v_f = _qkv_project(
                x, mods_l, w_exp, jnp.ones((1, 2 * D_W), F32), jnp.ones((1, D_W), F32),
                rope_c, rope_s, nq=2 * D_HEADS, nk=D_HEADS, nv=D_HEADS,
                use_norm=False, q_scale=D_HD ** -0.5, **dims)
            ctx_k = cache_d_k[:, j].reshape(dec_batch, past, D_W).astype(BF16)
            ctx_v = cache_d_v[:, j].reshape(dec_batch, past, D_HEADS * D_VD).astype(BF16)
            lvec = jnp.stack([d_lambda_q1[j], d_lambda_k1[j], d_lambda_q2[j], d_lambda_k2[j]])
            kern = functools.partial(_diff_attn_kernel, lam_init=lam_init)
            o = _attention(kern, (lvec, d_sub_norm[j].reshape(1, D_VD)), q, k_att, v_att, ctx_k, ctx_v,
                           n_out=D_HEADS * D_VD, t_p=t_p, seq_p=seq, batch_s=dec_batch, seq_s=dec_seq)
            wo = d_w_o[j].astype(BF16)
            dk.append(k_f[:t_p].reshape(batch, seq, D_HEADS, 2, D_HD))
            dv.append(v_f[:t_p].reshape(batch, seq, D_HEADS, D_VD))
        if kind == 2:
            x = _gmlp(x, mods_l, g_w_in[j], g_b_in[j], g_v_norm_g[j], g_v_norm_b[j], g_w_s[j], g_b_s[j],
                      g_w_out[j], ln_g[i, 0], ln_b[i, 0], **dims)
        else:
            x = _proj_ln(o, wo, x, mods_l, ln_g[i, 0], ln_b[i, 0], gate_idx=2, **dims)

        wq_hi, wq_lo = _split_f32(p_w_q[i].T)
        keys = p_sub_keys[i].reshape(P_HEADS * 2, N_KEYS, P_HALF)
        k_hi, k_lo = _split_f32(keys)
        h2t, ni, ai, rj, bj = _peer_topk(x, mods_l, wq_hi, wq_lo, k_hi, k_lo, **dims)
        x = _peer_dense(h2t, p_expert_u[i].astype(BF16), p_expert_v[i].T.astype(BF16),
                        ni, ai, rj, bj, x, mods_l, ln_g[i, 1], ln_b[i, 1], **dims)

    y_prompt = x[:t_p].reshape(batch, seq, d)
    y_sample = x[t_p:].reshape(dec_batch, dec_seq, d)
    return (y_prompt, y_sample, jnp.stack(ak, axis=1), jnp.stack(av, axis=1),
            jnp.stack(dk, axis=1), jnp.stack(dv, axis=1))
```

```python
import functools
import math

import numpy as np
import jax
import jax.numpy as jnp
from jax import lax
from jax.experimental import pallas as pl
from jax.experimental.pallas import tpu as pltpu

F32 = jnp.float32
BF16 = jnp.bfloat16

D_MODEL = 1024
DEPTH = 4
GRID_W = 64
N_MIXERS = 3
A_HEADS = 16
A_KV = 4
A_REP = A_HEADS // A_KV
A_HD = 64
D_HEADS = 8
D_HD = 64
D_VD = 2 * D_HD
D_W = D_HEADS * 2 * D_HD
G_DIM = 1024
G_GROUPS = 8
G_CHUNK = 128
P_HEADS = 8
N_KEYS = 128
P_HALF = 64
P_TOPK = 16
ROPE_THETA = 10000.0
DN_ALPHA = (2 * DEPTH) ** 0.25
EPS = 1e-6

LANES = 128
NEG_INF = float("-inf")
VMEM_LIMIT = 56 * 1024 * 1024

TM_QKV = 256
TM_PROJ = 512
TQ_ATTN = 256
TM_GMLP = 256
TM_TOPK = 512
TM_PEER = 512
TE_PEER = 2048


def _mm(a, b):
    return jnp.dot(a, b, preferred_element_type=F32)


def _mm_nt(a, b):
    return lax.dot_general(a, b, (((1,), (1,)), ((), ())), preferred_element_type=F32)


def _split(x):
    hi = x.astype(BF16)
    lo = (x - hi.astype(F32)).astype(BF16)
    return hi, lo


def _mm3(a_hi, a_lo, b_hi, b_lo):
    return _mm(a_hi, b_hi) + _mm(a_hi, b_lo) + _mm(a_lo, b_hi)


def _layer_norm(z, g, b):
    mu = jnp.mean(z, axis=-1, keepdims=True)
    zc = z - mu
    var = jnp.mean(zc * zc, axis=-1, keepdims=True)
    return zc * lax.rsqrt(var + EPS) * g + b


def _params(sem):
    return pltpu.CompilerParams(dimension_semantics=sem, vmem_limit_bytes=VMEM_LIMIT)


def _tile_group(i, n_p_tiles, tiles_per_batch):
    return jnp.where(i < n_p_tiles, 0, 1 + (i - n_p_tiles) // tiles_per_batch)


def _tile_pos_block(i, n_p_tiles, tiles_per_batch):
    return jnp.where(i < n_p_tiles, 0, (i - n_p_tiles) % tiles_per_batch)


def _mods_kernel(c_ref, w_ref, b_ref, o_ref):
    c = c_ref[...]
    a = c * jax.nn.sigmoid(c)
    a_hi, a_lo = _split(a)
    w_hi, w_lo = _split(w_ref[0])
    o_ref[0] = _mm3(a_hi, a_lo, w_hi, w_lo) + b_ref[0]


def _modulation(cond, w_mod, b_mod):
    depth, d, n = w_mod.shape
    rows = cond.shape[0]
    tn = 1536
    return pl.pallas_call(
        _mods_kernel,
        grid=(depth, n // tn),
        in_specs=[
            pl.BlockSpec((rows, d), lambda l, j: (0, 0)),
            pl.BlockSpec((1, d, tn), lambda l, j: (l, 0, j)),
            pl.BlockSpec((1, 1, tn), lambda l, j: (l, 0, j)),
        ],
        out_specs=pl.BlockSpec((1, rows, tn), lambda l, j: (l, 0, j)),
        out_shape=jax.ShapeDtypeStruct((depth, rows, n), F32),
        compiler_params=_params(("parallel", "parallel")),
    )(cond, w_mod, b_mod.reshape(depth, 1, n))


def _swap_pairs(x):
    lane = lax.broadcasted_iota(jnp.int32, x.shape, 1)
    nxt = pltpu.roll(x, LANES - 1, 1)
    prv = pltpu.roll(x, 1, 1)
    return jnp.where((lane & 1) == 0, nxt, prv)


def _qkv_kernel(x_ref, m_ref, w_ref, bd_ref, gq_ref, gk_ref, cs_ref, sn_ref,
                q_ref, ka_ref, va_ref, kf_ref, vf_ref,
                *, nq, nk, nv, use_norm, n_p_tiles, q_scale):
    x = x_ref[...]
    h = x * (1.0 + m_ref[0, 1:2, :]) + m_ref[0, 0:1, :]
    y = _mm(h.astype(BF16), w_ref[...])
    is_lat = pl.program_id(0) >= n_p_tiles
    cs = jnp.where(is_lat, cs_ref[...], 1.0)
    sn = jnp.where(is_lat, sn_ref[...], 0.0)
    bd = bd_ref[...]

    def head_norm(yg, gain):
        s_hi, s_lo = _split(yg * yg)
        ms = _mm(s_hi, bd) + _mm(s_lo, bd)
        return yg * lax.rsqrt(ms + EPS) * gain

    def rope(yg):
        return yg * cs + _swap_pairs(yg) * sn

    for g in range(nq):
        sl = slice(g * LANES, (g + 1) * LANES)
        yg = y[:, sl]
        if use_norm:
            yg = head_norm(yg, gq_ref[:, sl])
        q_ref[:, sl] = (rope(yg) * q_scale).astype(BF16)
    for g in range(nk):
        sl = slice(g * LANES, (g + 1) * LANES)
        yg = y[:, (nq + g) * LANES:(nq + g + 1) * LANES]
        if use_norm:
            yg = head_norm(yg, gk_ref[:, sl])
        kf_ref[:, sl] = yg
        ka_ref[:, sl] = rope(yg).astype(BF16)
    for g in range(nv):
        sl = slice(g * LANES, (g + 1) * LANES)
        yg = y[:, (nq + nk + g) * LANES:(nq + nk + g + 1) * LANES]
        vf_ref[:, sl] = yg
        va_ref[:, sl] = yg.astype(BF16)


def _qkv_project(x, mods_l, w_exp, gq, gk, rope_c, rope_s, *, nq, nk, nv, use_norm, t_p, s_len, q_scale):
    t, d = x.shape
    tm = TM_QKV
    n_p = t_p // tm
    tpb = s_len // tm
    ncol = (nq + nk + nv) * LANES
    bd = np.kron(np.eye(2, dtype=np.float32), np.full((64, 64), 1.0 / 64.0, np.float32))
    bd = jnp.asarray(bd, BF16)
    grp = lambda i: (_tile_group(i, n_p, tpb), 0, 0)
    pos = lambda i: (_tile_pos_block(i, n_p, tpb), 0)
    row = lambda i: (i, 0)
    fixed = lambda i: (0, 0)
    kern = functools.partial(_qkv_kernel, nq=nq, nk=nk, nv=nv, use_norm=use_norm,
                             n_p_tiles=n_p, q_scale=q_scale)
    return pl.pallas_call(
        kern,
        grid=(t // tm,),
        in_specs=[
            pl.BlockSpec((tm, d), row),
            pl.BlockSpec((1, 6, d), grp),
            pl.BlockSpec((d, ncol), fixed),
            pl.BlockSpec((LANES, LANES), fixed),
            pl.BlockSpec((1, nq * LANES), fixed),
            pl.BlockSpec((1, nk * LANES), fixed),
            pl.BlockSpec((tm, LANES), pos),
            pl.BlockSpec((tm, LANES), pos),
        ],
        out_specs=[
            pl.BlockSpec((tm, nq * LANES), row),
            pl.BlockSpec((tm, nk * LANES), row),
            pl.BlockSpec((tm, nv * LANES), row),
            pl.BlockSpec((tm, nk * LANES), row),
            pl.BlockSpec((tm, nv * LANES), row),
        ],
        out_shape=[
            jax.ShapeDtypeStruct((t, nq * LANES), BF16),
            jax.ShapeDtypeStruct((t, nk * LANES), BF16),
            jax.ShapeDtypeStruct((t, nv * LANES), BF16),
            jax.ShapeDtypeStruct((t, nk * LANES), F32),
            jax.ShapeDtypeStruct((t, nv * LANES), F32),
        ],
        compiler_params=_params(("parallel",)),
    )(x, mods_l, w_exp, bd, gq, gk, rope_c, rope_s)


def _softmax_pv(qh, k_all, v_all):
    scores = [_mm_nt(qh, k) for k in k_all]
    m = scores[0].max(axis=-1, keepdims=True)
    for s in scores[1:]:
        m = jnp.maximum(m, s.max(axis=-1, keepdims=True))
    o = None
    l = None
    for s, v in zip(scores, v_all):
        p = jnp.exp(s - m)
        ls = p.sum(axis=-1, keepdims=True)
        os_ = _mm(p.astype(BF16), v)
        o = os_ if o is None else o + os_
        l = ls if l is None else l + ls
    return o, l


def _gqa_attn_kernel(*refs, has_ctx):
    if has_ctx:
        q_ref, k_ref, v_ref, kc_ref, vc_ref, o_ref = refs
    else:
        q_ref, k_ref, v_ref, o_ref = refs
    for h in range(A_HEADS):
        grp = (h // A_REP) // 2
        sl = slice(h * LANES, (h + 1) * LANES)
        ks = slice(grp * LANES, (grp + 1) * LANES)
        k_all = [k_ref[:, ks]]
        v_all = [v_ref[:, ks]]
        if has_ctx:
            k_all.append(kc_ref[0, :, ks])
            v_all.append(vc_ref[0, :, ks])
        o, l = _softmax_pv(q_ref[:, sl], k_all, v_all)
        o_ref[:, sl] = (o / l).astype(BF16)


def _diff_attn_kernel(*refs, has_ctx, lam_init):
    if has_ctx:
        lv_ref, sg_ref, q_ref, k_ref, v_ref, kc_ref, vc_ref, o_ref = refs
    else:
        lv_ref, sg_ref, q_ref, k_ref, v_ref, o_ref = refs
    lv = lv_ref[...]
    lam = (jnp.exp(jnp.sum(lv[0:1] * lv[1:2], axis=-1, keepdims=True))
           - jnp.exp(jnp.sum(lv[2:3] * lv[3:4], axis=-1, keepdims=True)) + lam_init)
    for h in range(D_HEADS):
        ks = slice(h * LANES, (h + 1) * LANES)
        k_all = [k_ref[:, ks]]
        v_all = [v_ref[:, ks]]
        if has_ctx:
            k_all.append(kc_ref[0, :, ks])
            v_all.append(vc_ref[0, :, ks])
        outs = []
        for j in range(2):
            sl = slice((2 * h + j) * LANES, (2 * h + j + 1) * LANES)
            o, l = _softmax_pv(q_ref[:, sl], k_all, v_all)
            outs.append(o / l)
        o = outs[0] - lam * outs[1]
        ms = jnp.mean(o * o, axis=-1, keepdims=True)
        o = o * lax.rsqrt(ms + EPS) * sg_ref[...] * (1.0 - lam_init)
        o_ref[:, ks] = o.astype(BF16)


def _attention(kern, extra, q, k, v, ctx_k, ctx_v, *, n_out, t_p, seq_p, batch_s, seq_s):
    nqc = q.shape[1]
    nkc = k.shape[1]
    extra_specs = [pl.BlockSpec(a.shape, lambda *_: (0, 0)) for a in extra]
    n_seq_p = t_p // seq_p
    o_p = pl.pallas_call(
        functools.partial(kern, has_ctx=False),
        grid=(n_seq_p,),
        in_specs=extra_specs + [
            pl.BlockSpec((seq_p, nqc), lambda b: (b, 0)),
            pl.BlockSpec((seq_p, nkc), lambda b: (b, 0)),
            pl.BlockSpec((seq_p, nkc), lambda b: (b, 0)),
        ],
        out_specs=pl.BlockSpec((seq_p, n_out), lambda b: (b, 0)),
        out_shape=jax.ShapeDtypeStruct((t_p, n_out), BF16),
        compiler_params=_params(("parallel",)),
    )(*extra, q, k, v)
    tq = TQ_ATTN
    nqt = seq_s // tq
    assert t_p % seq_s == 0 and t_p % tq == 0
    q_off = t_p // tq
    k_off = t_p // seq_s
    past = ctx_k.shape[1]
    o_s = pl.pallas_call(
        functools.partial(kern, has_ctx=True),
        grid=(batch_s, nqt),
        in_specs=extra_specs + [
            pl.BlockSpec((tq, nqc), lambda b, i: (q_off + b * nqt + i, 0)),
            pl.BlockSpec((seq_s, nkc), lambda b, i: (k_off + b, 0)),
            pl.BlockSpec((seq_s, nkc), lambda b, i: (k_off + b, 0)),
            pl.BlockSpec((1, past, nkc), lambda b, i: (b, 0, 0)),
            pl.BlockSpec((1, past, nkc), lambda b, i: (b, 0, 0)),
        ],
        out_specs=pl.BlockSpec((tq, n_out), lambda b, i: (b * nqt + i, 0)),
        out_shape=jax.ShapeDtypeStruct((batch_s * seq_s, n_out), BF16),
        compiler_params=_params(("parallel", "arbitrary")),
    )(*extra, q, k, v, ctx_k, ctx_v)
    return jnp.concatenate([o_p, o_s], axis=0)


def _proj_ln_kernel(o_ref, w_ref, x_ref, m_ref, g_ref, b_ref, y_ref, *, gate_idx):
    y = _mm(o_ref[...], w_ref[...])
    z = DN_ALPHA * x_ref[...] + m_ref[0, gate_idx:gate_idx + 1, :] * y
    y_ref[...] = _layer_norm(z, g_ref[...], b_ref[...])


def _proj_ln(o, w, x, mods_l, ln_g, ln_b, *, gate_idx, t_p, s_len):
    t, d = x.shape
    tm = TM_PROJ
    n_p = t_p // tm
    tpb = s_len // tm
    kin = o.shape[1]
    row = lambda i: (i, 0)
    fixed = lambda i: (0, 0)
    return pl.pallas_call(
        functools.partial(_proj_ln_kernel, gate_idx=gate_idx),
        grid=(t // tm,),
        in_specs=[
            pl.BlockSpec((tm, kin), row),
            pl.BlockSpec((kin, d), fixed),
            pl.BlockSpec((tm, d), row),
            pl.BlockSpec((1, 6, d), lambda i: (_tile_group(i, n_p, tpb), 0, 0)),
            pl.BlockSpec((1, d), fixed),
            pl.BlockSpec((1, d), fixed),
        ],
        out_specs=pl.BlockSpec((tm, d), row),
        out_shape=jax.ShapeDtypeStruct((t, d), F32),
        compiler_params=_params(("parallel",)),
    )(o, w, x, mods_l, ln_g.reshape(1, d), ln_b.reshape(1, d))


def _gmlp_kernel(x_ref, m_ref, win_ref, bin_ref, vg_ref, vb_ref, ws_ref, bs_ref, wout_ref,
                 g_ref, b_ref, y_ref, uv_sc):
    x = x_ref[...]
    h = x * (1.0 + m_ref[0, 1:2, :]) + m_ref[0, 0:1, :]
    z = jax.nn.gelu(_mm(h.astype(BF16), win_ref[...]) + bin_ref[...])
    u = z[:, :G_DIM]
    v = _layer_norm(z[:, G_DIM:], vg_ref[...], vb_ref[...])
    tm = x.shape[0]
    for c in range(tm // G_CHUNK):
        rs = slice(c * G_CHUNK, (c + 1) * G_CHUNK)
        for g in range(G_GROUPS):
            cs = slice(g * LANES, (g + 1) * LANES)
            vm = _mm(ws_ref[g], v[rs, cs].astype(BF16)) + bs_ref[g]
            uv_sc[rs, cs] = (u[rs, cs] * vm).astype(BF16)
    y = _mm(uv_sc[...], wout_ref[...])
    zz = DN_ALPHA * x + m_ref[0, 2:3, :] * y
    y_ref[...] = _layer_norm(zz, g_ref[...], b_ref[...])


def _gmlp(x, mods_l, w_in, b_in, vg, vb, w_s, b_s, w_out, ln_g, ln_b, *, t_p, s_len):
    t, d = x.shape
    tm = TM_GMLP
    n_p = t_p // tm
    tpb = s_len // tm
    row = lambda i: (i, 0)
    fixed = lambda i: (0, 0)
    fixed3 = lambda i: (0, 0, 0)
    b_s_b = jnp.broadcast_to(b_s[:, :, None], (G_GROUPS, G_CHUNK, LANES))
    return pl.pallas_call(
        _gmlp_kernel,
        grid=(t // tm,),
        in_specs=[
            pl.BlockSpec((tm, d), row),
            pl.BlockSpec((1, 6, d), lambda i: (_tile_group(i, n_p, tpb), 0, 0)),
            pl.BlockSpec((d, 2 * G_DIM), fixed),
            pl.BlockSpec((1, 2 * G_DIM), fixed),
            pl.BlockSpec((1, G_DIM), fixed),
            pl.BlockSpec((1, G_DIM), fixed),
            pl.BlockSpec((G_GROUPS, G_CHUNK, G_CHUNK), fixed3),
            pl.BlockSpec((G_GROUPS, G_CHUNK, LANES), fixed3),
            pl.BlockSpec((G_DIM, d), fixed),
            pl.BlockSpec((1, d), fixed),
            pl.BlockSpec((1, d), fixed),
        ],
        out_specs=pl.BlockSpec((tm, d), row),
        out_shape=jax.ShapeDtypeStruct((t, d), F32),
        scratch_shapes=[pltpu.VMEM((tm, G_DIM), BF16)],
        compiler_params=_params(("parallel",)),
    )(x, mods_l, w_in.astype(BF16), b_in.reshape(1, -1), vg.reshape(1, -1), vb.reshape(1, -1),
      w_s.astype(BF16), b_s_b, w_out.astype(BF16), ln_g.reshape(1, d), ln_b.reshape(1, d))


def _top16(sc):
    n, L = sc.shape
    pos = lax.broadcasted_iota(jnp.int32, (n, L), 0).astype(F32)
    slot = lax.broadcasted_iota(jnp.int32, (P_TOPK, L), 0)

    def body(k, carry):
        cur, rank, vals = carry
        m = jnp.max(cur, axis=0, keepdims=True)
        idx = jnp.min(jnp.where(cur == m, pos, float(n)), axis=0, keepdims=True)
        hit = pos == idx
        cur = jnp.where(hit, NEG_INF, cur)
        rank = jnp.where(hit, k.astype(F32), rank)
        vals = jnp.where(slot == k, m, vals)
        return cur, rank, vals

    init = (sc, jnp.full((n, L), 64.0, F32), jnp.zeros((P_TOPK, L), F32))
    _, rank, vals = lax.fori_loop(0, P_TOPK, body, init)
    return vals, rank


_CAND_BLOCKS = ((0, 0, 8), (0, 8, 8), (1, 0, 8), (2, 0, 5), (3, 0, 4), (4, 0, 3), (5, 0, 2), (6, 0, 2), (7, 0, 2))


def _pair_select(v1, v2):
    L = v1.shape[1]
    r8 = lax.broadcasted_iota(jnp.int32, (8, L), 0).astype(F32)
    blocks = []
    codes = []
    for a, b0, nb in _CAND_BLOCKS:
        c = v1[a:a + 1] + v2[b0:b0 + 8]
        if nb < 8:
            c = jnp.where(r8 < float(nb), c, NEG_INF)
        blocks.append(c)
        codes.append(r8 + float(a * P_TOPK + b0))
    blocks.append(v1[8:16] + v2[0:1])
    codes.append((r8 + 8.0) * float(P_TOPK))
    cand0 = jnp.concatenate(blocks, axis=0)
    code = jnp.concatenate(codes, axis=0)

    def body(k, carry):
        cur, sel = carry
        m = jnp.max(cur, axis=0, keepdims=True)
        idx = jnp.min(jnp.where(cur == m, code, 1e9), axis=0, keepdims=True)
        hit = code == idx
        return jnp.where(hit, NEG_INF, cur), jnp.where(hit, 1.0, sel)

    _, sel = lax.fori_loop(0, P_TOPK, body, (cand0, jnp.zeros_like(cand0)))
    top = v1[0:1] + v2[0:1]
    z = jnp.sum(jnp.where(sel > 0.0, jnp.exp(cand0 - top), 0.0), axis=0, keepdims=True)
    cnt_lo = jnp.zeros((8, L), F32)
    starts = (0, 16, 24, 32, 40, 48, 56, 64, 72)
    for a in range(8):
        n_a = jnp.sum(sel[starts[a]:starts[a + 1]], axis=0, keepdims=True)
        cnt_lo = jnp.where(r8 == float(a), n_a, cnt_lo)
    count = jnp.concatenate([cnt_lo, sel[72:80]], axis=0)
    return count, z


def _peer_topk_kernel(x_ref, m_ref, wh_ref, wl_ref, kh_ref, kl_ref,
                      h2t_ref, ni_ref, ai_ref, rj_ref, bj_ref, qt_sc):
    hd = pl.program_id(1)

    @pl.when(hd == 0)
    def _():
        x = x_ref[...]
        h2 = x * (1.0 + m_ref[0, 4:5, :]) + m_ref[0, 3:4, :]
        h_hi, h_lo = _split(h2.T)
        h2t_ref[...] = h_hi
        qt_sc[...] = _mm3(wh_ref[...], wl_ref[...], h_hi, h_lo)

    qh = qt_sc[pl.ds(pl.multiple_of(hd * (2 * P_HALF), 2 * P_HALF), 2 * P_HALF), :]
    q1_hi, q1_lo = _split(qh[:P_HALF])
    q2_hi, q2_lo = _split(qh[P_HALF:])
    sc1 = _mm3(kh_ref[0], kl_ref[0], q1_hi, q1_lo)
    sc2 = _mm3(kh_ref[1], kl_ref[1], q2_hi, q2_lo)
    tm = sc1.shape[1]
    for c in range(tm // LANES):
        ls = slice(c * LANES, (c + 1) * LANES)
        s1 = sc1[:, ls]
        s2 = sc2[:, ls]
        v1, r1 = _top16(s1)
        v2, r2 = _top16(s2)
        count, z = _pair_select(v1, v2)
        npos = jnp.zeros_like(r1)
        for a in range(P_TOPK):
            npos = jnp.where(r1 == float(a), count[a:a + 1], npos)
        ni_ref[0, :, ls] = npos
        ai_ref[0, :, ls] = jnp.exp(s1 - v1[0:1]) / z
        rj_ref[0, :, ls] = r2.astype(BF16)
        bj_ref[0, :, ls] = jnp.exp(s2 - v2[0:1]).astype(BF16)


def _peer_topk(x, mods_l, wq_t_hi, wq_t_lo, keys_hi, keys_lo, *, t_p, s_len):
    t, d = x.shape
    tm = TM_TOPK
    n_p = t_p // tm
    tpb = s_len // tm
    row_tab = jax.ShapeDtypeStruct((P_HEADS, N_KEYS, t), F32)
    row_spec = pl.BlockSpec((1, N_KEYS, tm), lambda i, h: (h, 0, i))
    col_tab = jax.ShapeDtypeStruct((P_HEADS, N_KEYS, t), BF16)
    col_spec = row_spec
    return pl.pallas_call(
        _peer_topk_kernel,
        grid=(t // tm, P_HEADS),
        in_specs=[
            pl.BlockSpec((tm, d), lambda i, h: (i, 0)),
            pl.BlockSpec((1, 6, d), lambda i, h: (_tile_group(i, n_p, tpb), 0, 0)),
            pl.BlockSpec((d, d), lambda i, h: (0, 0)),
            pl.BlockSpec((d, d), lambda i, h: (0, 0)),
            pl.BlockSpec((2, N_KEYS, P_HALF), lambda i, h: (h, 0, 0)),
            pl.BlockSpec((2, N_KEYS, P_HALF), lambda i, h: (h, 0, 0)),
        ],
        out_specs=[pl.BlockSpec((d, tm), lambda i, h: (0, i)), row_spec, row_spec, col_spec, col_spec],
        out_shape=[jax.ShapeDtypeStruct((d, t), BF16), row_tab, row_tab, col_tab, col_tab],
        scratch_shapes=[pltpu.VMEM((d, tm), F32)],
        compiler_params=_params(("parallel", "arbitrary")),
    )(x, mods_l, wq_t_hi, wq_t_lo, keys_hi, keys_lo)


def _peer_dense_kernel(h2t_ref, eu_ref, evt_ref, ni_ref, ai_ref, rj_ref, bj_ref,
                       x_ref, m_ref, g_ref, b_ref, y_ref, acc_sc, act_sc, wt_sc, *, n_e):
    e = pl.program_id(1)

    @pl.when(e == 0)
    def _():
        acc_sc[...] = jnp.zeros_like(acc_sc)

    act_sc[...] = _mm(eu_ref[...], h2t_ref[...])
    te, tm = act_sc.shape
    zero = jnp.zeros((), BF16)

    for il in range(te // N_KEYS):
        rs = slice(il * N_KEYS, (il + 1) * N_KEYS)
        gate = None
        for h in range(P_HEADS):
            cnt = ni_ref[h, il:il + 1, :].astype(BF16)
            a_i = ai_ref[h, il:il + 1, :].astype(BF16)
            gh = jnp.where(rj_ref[h] < cnt, bj_ref[h], zero) * a_i
            gate = gh if gate is None else gate + gh
        wt_sc[rs, :] = gate * jax.nn.gelu(act_sc[rs, :].astype(BF16))
    acc_sc[...] += _mm(evt_ref[...], wt_sc[...])

    @pl.when(e == n_e - 1)
    def _():
        f = acc_sc[...].T
        z = DN_ALPHA * x_ref[...] + m_ref[0, 5:6, :] * f
        y_ref[...] = _layer_norm(z, g_ref[...], b_ref[...])


def _peer_dense(h2t, e_u, e_vt, ni, ai, rj, bj, x, mods_l, ln_g, ln_b, *, t_p, s_len):
    t, d = x.shape
    tm = TM_PEER
    te = TE_PEER
    n_p = t_p // tm
    tpb = s_len // tm
    n_exp = e_u.shape[0]
    n_e = n_exp // te
    n_i = te // N_KEYS
    return pl.pallas_call(
        functools.partial(_peer_dense_kernel, n_e=n_e),
        grid=(t // tm, n_e),
        in_specs=[
            pl.BlockSpec((d, tm), lambda i, e: (0, i)),
            pl.BlockSpec((te, d), lambda i, e: (e, 0)),
            pl.BlockSpec((d, te), lambda i, e: (0, e)),
            pl.BlockSpec((P_HEADS, n_i, tm), lambda i, e: (0, e, i)),
            pl.BlockSpec((P_HEADS, n_i, tm), lambda i, e: (0, e, i)),
            pl.BlockSpec((P_HEADS, N_KEYS, tm), lambda i, e: (0, 0, i)),
            pl.BlockSpec((P_HEADS, N_KEYS, tm), lambda i, e: (0, 0, i)),
            pl.BlockSpec((tm, d), lambda i, e: (i, 0)),
            pl.BlockSpec((1, 6, d), lambda i, e: (_tile_group(i, n_p, tpb), 0, 0)),
            pl.BlockSpec((1, d), lambda i, e: (0, 0)),
            pl.BlockSpec((1, d), lambda i, e: (0, 0)),
        ],
        out_specs=pl.BlockSpec((tm, d), lambda i, e: (i, 0)),
        out_shape=jax.ShapeDtypeStruct((t, d), F32),
        scratch_shapes=[pltpu.VMEM((d, tm), F32), pltpu.VMEM((te, tm), F32), pltpu.VMEM((te, tm), BF16)],
        compiler_params=_params(("parallel", "arbitrary")),
    )(h2t, e_u, e_vt, ni, ai, rj, bj, x, mods_l, ln_g.reshape(1, d), ln_b.reshape(1, d))


def _rope_tables(n_tok):
    rows = n_tok // GRID_W
    row = jnp.repeat(jnp.arange(rows), GRID_W).astype(F32)
    col = jnp.tile(jnp.arange(GRID_W), rows).astype(F32)
    n_freq = A_HD // 4
    inv = ROPE_THETA ** (-jnp.arange(n_freq, dtype=F32) / n_freq)
    ang = jnp.concatenate([row[:, None] * inv, col[:, None] * inv], -1)
    cos = jnp.repeat(jnp.cos(ang), 2, axis=-1)
    sin = jnp.repeat(jnp.sin(ang), 2, axis=-1)
    sign = jnp.tile(jnp.asarray([-1.0, 1.0], F32), A_HD // 2)
    reps = LANES // A_HD
    return jnp.tile(cos, (1, reps)), jnp.tile(sin * sign, (1, reps))


def _gqa_weights(w_qkv, q_norm, k_norm, w_o):
    d = w_qkv.shape[0]
    nq = A_HEADS * A_HD
    half = (np.arange(A_HEADS) // A_REP) % 2
    sel = np.stack([half == 0, half == 1], axis=1).astype(np.float32)
    wq = w_qkv[:, :nq].reshape(d, A_HEADS, 1, A_HD) * sel[None, :, :, None]
    w_exp = jnp.concatenate([wq.reshape(d, A_HEADS * LANES), w_qkv[:, nq:]], axis=1).astype(BF16)
    gq = jnp.tile(q_norm, 2 * A_HEADS).reshape(1, -1)
    gk = jnp.tile(k_norm, A_KV).reshape(1, -1)
    wo = w_o.reshape(A_HEADS, 1, A_HD, d) * sel[:, :, None, None]
    return w_exp, gq, gk, wo.reshape(A_HEADS * LANES, d).astype(BF16)


def _diff_weights(w_qkv):
    d = w_qkv.shape[0]
    eye = np.eye(2, dtype=np.float32)
    wq = w_qkv[:, :D_W].reshape(d, D_HEADS, 1, 2, D_HD) * eye[None, None, :, :, None]
    return jnp.concatenate([wq.reshape(d, D_HEADS * 2 * LANES), w_qkv[:, D_W:]], axis=1).astype(BF16)


def _split_f32(w):
    hi = w.astype(BF16)
    return hi, (w - hi.astype(F32)).astype(BF16)


def kernel(x_prompt, x_sample, cache_a_k, cache_a_v, cache_d_k, cache_d_v, c, c_ctx, w_mod, b_mod, ln_g, ln_b, a_w_qkv, a_q_norm, a_k_norm, a_w_o, d_w_qkv, d_lambda_q1, d_lambda_k1, d_lambda_q2, d_lambda_k2, d_sub_norm, d_w_o, g_w_in, g_b_in, g_v_norm_g, g_v_norm_b, g_w_s, g_b_s, g_w_out, p_w_q, p_sub_keys, p_expert_u, p_expert_v):
    batch, seq, d = x_prompt.shape
    dec_batch, dec_seq, _ = x_sample.shape
    past = cache_a_k.shape[2]
    t_p = batch * seq
    t_s = dec_batch * dec_seq
    dims = dict(t_p=t_p, s_len=dec_seq)

    x = jnp.concatenate([x_prompt.reshape(t_p, d), x_sample.reshape(t_s, d)], axis=0)
    n_cond = -(-(1 + dec_batch) // 8) * 8
    cond = jnp.zeros((n_cond, d), F32).at[0].set(c_ctx).at[1:1 + dec_batch].set(c)
    mods = _modulation(cond, w_mod, b_mod).reshape(DEPTH, n_cond, 6, d)
    rope_c, rope_s = _rope_tables(dec_seq)

    ak, av, dk, dv = [], [], [], []
    for i in range(DEPTH):
        kind, j = i % N_MIXERS, i // N_MIXERS
        mods_l = mods[i]
        if kind == 0:
            w_exp, gq, gk, wo = _gqa_weights(a_w_qkv[j], a_q_norm[j], a_k_norm[j], a_w_o[j])
            q, k_att, v_att, k_f, v_f = _qkv_project(
                x, mods_l, w_exp, gq, gk, rope_c, rope_s, nq=A_HEADS, nk=A_KV * A_HD // LANES,
                nv=A_KV * A_HD // LANES, use_norm=True, q_scale=A_HD ** -0.5, **dims)
            ctx_k = cache_a_k[:, j].reshape(dec_batch, past, A_KV * A_HD).astype(BF16)
            ctx_v = cache_a_v[:, j].reshape(dec_batch, past, A_KV * A_HD).astype(BF16)
            o = _attention(_gqa_attn_kernel, (), q, k_att, v_att, ctx_k, ctx_v, n_out=A_HEADS * LANES,
                           t_p=t_p, seq_p=seq, batch_s=dec_batch, seq_s=dec_seq)
            ak.append(k_f[:t_p].reshape(batch, seq, A_KV, A_HD))
            av.append(v_f[:t_p].reshape(batch, seq, A_KV, A_HD))
        elif kind == 1:
            lam_init = 0.8 - 0.6 * math.exp(-0.3 * i)
            w_exp = _diff_weights(d_w_qkv[j])
            q, k_att, v_att, k_f, v_f = _qkv_project(
                x, mods_l, w_exp, jnp.ones((1, 2 * D_W), F32), jnp.ones((1, D_W), F32),
                rope_c, rope_s, nq=2 * D_HEADS, nk=D_HEADS, nv=D_HEADS,
                use_norm=False, q_scale=D_HD ** -0.5, **dims)
            ctx_k = cache_d_k[:, j].reshape(dec_batch, past, D_W).astype(BF16)
            ctx_v = cache_d_v[:, j].reshape(dec_batch, past, D_HEADS * D_VD).astype(BF16)
            lvec = jnp.stack([d_lambda_q1[j], d_lambda_k1[j], d_lambda_q2[j], d_lambda_k2[j]])
            kern = functools.partial(_diff_attn_kernel, lam_init=lam_init)
            o = _attention(kern, (lvec, d_sub_norm[j].reshape(1, D_VD)), q, k_att, v_att, ctx_k, ctx_v,
                           n_out=D_HEADS * D_VD, t_p=t_p, seq_p=seq, batch_s=dec_batch, seq_s=dec_seq)
            wo = d_w_o[j].astype(BF16)
            dk.append(k_f[:t_p].reshape(batch, seq, D_HEADS, 2, D_HD))
            dv.append(v_f[:t_p].reshape(batch, seq, D_HEADS, D_VD))
        if kind == 2:
            x = _gmlp(x, mods_l, g_w_in[j], g_b_in[j], g_v_norm_g[j], g_v_norm_b[j], g_w_s[j], g_b_s[j],
                      g_w_out[j], ln_g[i, 0], ln_b[i, 0], **dims)
        else:
            x = _proj_ln(o, wo, x, mods_l, ln_g[i, 0], ln_b[i, 0], gate_idx=2, **dims)

        wq_hi, wq_lo = _split_f32(p_w_q[i].T)
        keys = p_sub_keys[i].reshape(P_HEADS * 2, N_KEYS, P_HALF)
        k_hi, k_lo = _split_f32(keys)
        h2t, ni, ai, rj, bj = _peer_topk(x, mods_l, wq_hi, wq_lo, k_hi, k_lo, **dims)
        x = _peer_dense(h2t, p_expert_u[i].astype(BF16), p_expert_v[i].T.astype(BF16),
                        ni, ai, rj, bj, x, mods_l, ln_g[i, 1], ln_b[i, 1], **dims)

    y_prompt = x[:t_p].reshape(batch, seq, d)
    y_sample = x[t_p:].reshape(dec_batch, dec_seq, d)
    return (y_prompt, y_sample, jnp.stack(ak, axis=1), jnp.stack(av, axis=1),
            jnp.stack(dk, axis=1), jnp.stack(dv, axis=1))
```

```python
import functools
import math

import numpy as np
import jax
import jax.numpy as jnp
from jax import lax
from jax.experimental import pallas as pl
from jax.experimental.pallas import tpu as pltpu

F32 = jnp.float32
BF16 = jnp.bfloat16

D_MODEL = 1024
DEPTH = 4
GRID_W = 64
N_MIXERS = 3
A_HEADS = 16
A_KV = 4
A_REP = A_HEADS // A_KV
A_HD = 64
D_HEADS = 8
D_HD = 64
D_VD = 2 * D_HD
D_W = D_HEADS * 2 * D_HD
G_DIM = 1024
G_GROUPS = 8
G_CHUNK = 128
P_HEADS = 8
N_KEYS = 128
P_HALF = 64
P_TOPK = 16
ROPE_THETA = 10000.0
DN_ALPHA = (2 * DEPTH) ** 0.25
EPS = 1e-6

LANES = 128
NEG_INF = float("-inf")
VMEM_LIMIT = 56 * 1024 * 1024

TM_QKV = 256
TM_PROJ = 512
TQ_ATTN = 256
TM_GMLP = 256
TM_TOPK = 512
TM_PEER = 512
TE_PEER = 2048


def _mm(a, b):
    return jnp.dot(a, b, preferred_element_type=F32)


def _mm_nt(a, b):
    return lax.dot_general(a, b, (((1,), (1,)), ((), ())), preferred_element_type=F32)


def _split(x):
    hi = x.astype(BF16)
    lo = (x - hi.astype(F32)).astype(BF16)
    return hi, lo


def _mm3(a_hi, a_lo, b_hi, b_lo):
    return _mm(a_hi, b_hi) + _mm(a_hi, b_lo) + _mm(a_lo, b_hi)


def _layer_norm(z, g, b):
    mu = jnp.mean(z, axis=-1, keepdims=True)
    zc = z - mu
    var = jnp.mean(zc * zc, axis=-1, keepdims=True)
    return zc * lax.rsqrt(var + EPS) * g + b


def _params(sem):
    return pltpu.CompilerParams(dimension_semantics=sem, vmem_limit_bytes=VMEM_LIMIT)


def _tile_group(i, n_p_tiles, tiles_per_batch):
    return jnp.where(i < n_p_tiles, 0, 1 + (i - n_p_tiles) // tiles_per_batch)


def _tile_pos_block(i, n_p_tiles, tiles_per_batch):
    return jnp.where(i < n_p_tiles, 0, (i - n_p_tiles) % tiles_per_batch)


def _mods_kernel(c_ref, w_ref, b_ref, o_ref):
    c = c_ref[...]
    a = c * jax.nn.sigmoid(c)
    a_hi, a_lo = _split(a)
    w_hi, w_lo = _split(w_ref[0])
    o_ref[0] = _mm3(a_hi, a_lo, w_hi, w_lo) + b_ref[0]


def _modulation(cond, w_mod, b_mod):
    depth, d, n = w_mod.shape
    rows = cond.shape[0]
    tn = 1536
    return pl.pallas_call(
        _mods_kernel,
        grid=(depth, n // tn),
        in_specs=[
            pl.BlockSpec((rows, d), lambda l, j: (0, 0)),
            pl.BlockSpec((1, d, tn), lambda l, j: (l, 0, j)),
            pl.BlockSpec((1, 1, tn), lambda l, j: (l, 0, j)),
        ],
        out_specs=pl.BlockSpec((1, rows, tn), lambda l, j: (l, 0, j)),
        out_shape=jax.ShapeDtypeStruct((depth, rows, n), F32),
        compiler_params=_params(("parallel", "parallel")),
    )(cond, w_mod, b_mod.reshape(depth, 1, n))


def _swap_pairs(x):
    lane = lax.broadcasted_iota(jnp.int32, x.shape, 1)
    nxt = pltpu.roll(x, LANES - 1, 1)
    prv = pltpu.roll(x, 1, 1)
    return jnp.where((lane & 1) == 0, nxt, prv)


def _qkv_kernel(x_ref, m_ref, w_ref, bd_ref, gq_ref, gk_ref, cs_ref, sn_ref,
                q_ref, ka_ref, va_ref, kf_ref, vf_ref,
                *, nq, nk, nv, use_norm, n_p_tiles, q_scale):
    x = x_ref[...]
    h = x * (1.0 + m_ref[0, 1:2, :]) + m_ref[0, 0:1, :]
    y = _mm(h.astype(BF16), w_ref[...])
    is_lat = pl.program_id(0) >= n_p_tiles
    cs = jnp.where(is_lat, cs_ref[...], 1.0)
    sn = jnp.where(is_lat, sn_ref[...], 0.0)
    bd = bd_ref[...]

    def head_norm(yg, gain):
        s_hi, s_lo = _split(yg * yg)
        ms = _mm(s_hi, bd) + _mm(s_lo, bd)
        return yg * lax.rsqrt(ms + EPS) * gain

    def rope(yg):
        return yg * cs + _swap_pairs(yg) * sn

    for g in range(nq):
        sl = slice(g * LANES, (g + 1) * LANES)
        yg = y[:, sl]
        if use_norm:
            yg = head_norm(yg, gq_ref[:, sl])
        q_ref[:, sl] = (rope(yg) * q_scale).astype(BF16)
    for g in range(nk):
        sl = slice(g * LANES, (g + 1) * LANES)
        yg = y[:, (nq + g) * LANES:(nq + g + 1) * LANES]
        if use_norm:
            yg = head_norm(yg, gk_ref[:, sl])
        kf_ref[:, sl] = yg
        ka_ref[:, sl] = rope(yg).astype(BF16)
    for g in range(nv):
        sl = slice(g * LANES, (g + 1) * LANES)
        yg = y[:, (nq + nk + g) * LANES:(nq + nk + g + 1) * LANES]
        vf_ref[:, sl] = yg
        va_ref[:, sl] = yg.astype(BF16)


def _qkv_project(x, mods_l, w_exp, gq, gk, rope_c, rope_s, *, nq, nk, nv, use_norm, t_p, s_len, q_scale):
    t, d = x.shape
    tm = TM_QKV
    n_p = t_p // tm
    tpb = s_len // tm
    ncol = (nq + nk + nv) * LANES
    bd = np.kron(np.eye(2, dtype=np.float32), np.full((64, 64), 1.0 / 64.0, np.float32))
    bd = jnp.asarray(bd, BF16)
    grp = lambda i: (_tile_group(i, n_p, tpb), 0, 0)
    pos = lambda i: (_tile_pos_block(i, n_p, tpb), 0)
    row = lambda i: (i, 0)
    fixed = lambda i: (0, 0)
    kern = functools.partial(_qkv_kernel, nq=nq, nk=nk, nv=nv, use_norm=use_norm,
                             n_p_tiles=n_p, q_scale=q_scale)
    return pl.pallas_call(
        kern,
        grid=(t // tm,),
        in_specs=[
            pl.BlockSpec((tm, d), row),
            pl.BlockSpec((1, 6, d), grp),
            pl.BlockSpec((d, ncol), fixed),
            pl.BlockSpec((LANES, LANES), fixed),
            pl.BlockSpec((1, nq * LANES), fixed),
            pl.BlockSpec((1, nk * LANES), fixed),
            pl.BlockSpec((tm, LANES), pos),
            pl.BlockSpec((tm, LANES), pos),
        ],
        out_specs=[
            pl.BlockSpec((tm, nq * LANES), row),
            pl.BlockSpec((tm, nk * LANES), row),
            pl.BlockSpec((tm, nv * LANES), row),
            pl.BlockSpec((tm, nk * LANES), row),
            pl.BlockSpec((tm, nv * LANES), row),
        ],
        out_shape=[
            jax.ShapeDtypeStruct((t, nq * LANES), BF16),
            jax.ShapeDtypeStruct((t, nk * LANES), BF16),
            jax.ShapeDtypeStruct((t, nv * LANES), BF16),
            jax.ShapeDtypeStruct((t, nk * LANES), F32),
            jax.ShapeDtypeStruct((t, nv * LANES), F32),
        ],
        compiler_params=_params(("parallel",)),
    )(x, mods_l, w_exp, bd, gq, gk, rope_c, rope_s)


def _softmax_pv(qh, k_all, v_all):
    scores = [_mm_nt(qh, k) for k in k_all]
    m = scores[0].max(axis=-1, keepdims=True)
    for s in scores[1:]:
        m = jnp.maximum(m, s.max(axis=-1, keepdims=True))
    o = None
    l = None
    for s, v in zip(scores, v_all):
        p = jnp.exp(s - m)
        ls = p.sum(axis=-1, keepdims=True)
        os_ = _mm(p.astype(BF16), v)
        o = os_ if o is None else o + os_
        l = ls if l is None else l + ls
    return o, l


def _gqa_attn_kernel(*refs, has_ctx):
    if has_ctx:
        q_ref, k_ref, v_ref, kc_ref, vc_ref, o_ref = refs
    else:
        q_ref, k_ref, v_ref, o_ref = refs
    for h in range(A_HEADS):
        grp = (h // A_REP) // 2
        sl = slice(h * LANES, (h + 1) * LANES)
        ks = slice(grp * LANES, (grp + 1) * LANES)
        k_all = [k_ref[:, ks]]
        v_all = [v_ref[:, ks]]
        if has_ctx:
            k_all.append(kc_ref[0, :, ks])
            v_all.append(vc_ref[0, :, ks])
        o, l = _softmax_pv(q_ref[:, sl], k_all, v_all)
        o_ref[:, sl] = (o / l).astype(BF16)


def _diff_attn_kernel(*refs, has_ctx, lam_init):
    if has_ctx:
        lv_ref, sg_ref, q_ref, k_ref, v_ref, kc_ref, vc_ref, o_ref = refs
    else:
        lv_ref, sg_ref, q_ref, k_ref, v_ref, o_ref = refs
    lv = lv_ref[...]
    lam = (jnp.exp(jnp.sum(lv[0:1] * lv[1:2], axis=-1, keepdims=True))
           - jnp.exp(jnp.sum(lv[2:3] * lv[3:4], axis=-1, keepdims=True)) + lam_init)
    for h in range(D_HEADS):
        ks = slice(h * LANES, (h + 1) * LANES)
        k_all = [k_ref[:, ks]]
        v_all = [v_ref[:, ks]]
        if has_ctx:
            k_all.append(kc_ref[0, :, ks])
            v_all.append(vc_ref[0, :, ks])
        outs = []
        for j in range(2):
            sl = slice((2 * h + j) * LANES, (2 * h + j + 1) * LANES)
            o, l = _softmax_pv(q_ref[:, sl], k_all, v_all)
            outs.append(o / l)
        o = outs[0] - lam * outs[1]
        ms = jnp.mean(o * o, axis=-1, keepdims=True)
        o = o * lax.rsqrt(ms + EPS) * sg_ref[...] * (1.0 - lam_init)
        o_ref[:, ks] = o.astype(BF16)


def _attention(kern, extra, q, k, v, ctx_k, ctx_v, *, n_out, t_p, seq_p, batch_s, seq_s):
    nqc = q.shape[1]
    nkc = k.shape[1]
    extra_specs = [pl.BlockSpec(a.shape, lambda *_: (0, 0)) for a in extra]
    n_seq_p = t_p // seq_p
    o_p = pl.pallas_call(
        functools.partial(kern, has_ctx=False),
        grid=(n_seq_p,),
        in_specs=extra_specs + [
            pl.BlockSpec((seq_p, nqc), lambda b: (b, 0)),
            pl.BlockSpec((seq_p, nkc), lambda b: (b, 0)),
            pl.BlockSpec((seq_p, nkc), lambda b: (b, 0)),
        ],
        out_specs=pl.BlockSpec((seq_p, n_out), lambda b: (b, 0)),
        out_shape=jax.ShapeDtypeStruct((t_p, n_out), BF16),
        compiler_params=_params(("parallel",)),
    )(*extra, q, k, v)
    tq = TQ_ATTN
    nqt = seq_s // tq
    assert t_p % seq_s == 0 and t_p % tq == 0
    q_off = t_p // tq
    k_off = t_p // seq_s
    past = ctx_k.shape[1]
    o_s = pl.pallas_call(
        functools.partial(kern, has_ctx=True),
        grid=(batch_s, nqt),
        in_specs=extra_specs + [
            pl.BlockSpec((tq, nqc), lambda b, i: (q_off + b * nqt + i, 0)),
            pl.BlockSpec((seq_s, nkc), lambda b, i: (k_off + b, 0)),
            pl.BlockSpec((seq_s, nkc), lambda b, i: (k_off + b, 0)),
            pl.BlockSpec((1, past, nkc), lambda b, i: (b, 0, 0)),
            pl.BlockSpec((1, past, nkc), lambda b, i: (b, 0, 0)),
        ],
        out_specs=pl.BlockSpec((tq, n_out), lambda b, i: (b * nqt + i, 0)),
        out_shape=jax.ShapeDtypeStruct((batch_s * seq_s, n_out), BF16),
        compiler_params=_params(("parallel", "arbitrary")),
    )(*extra, q, k, v, ctx_k, ctx_v)
    return jnp.concatenate([o_p, o_s], axis=0)


def _proj_ln_kernel(o_ref, w_ref, x_ref, m_ref, g_ref, b_ref, y_ref, *, gate_idx):
    y = _mm(o_ref[...], w_ref[...])
    z = DN_ALPHA * x_ref[...] + m_ref[0, gate_idx:gate_idx + 1, :] * y
    y_ref[...] = _layer_norm(z, g_ref[...], b_ref[...])


def _proj_ln(o, w, x, mods_l, ln_g, ln_b, *, gate_idx, t_p, s_len):
    t, d = x.shape
    tm = TM_PROJ
    n_p = t_p // tm
    tpb = s_len // tm
    kin = o.shape[1]
    row = lambda i: (i, 0)
    fixed = lambda i: (0, 0)
    return pl.pallas_call(
        functools.partial(_proj_ln_kernel, gate_idx=gate_idx),
        grid=(t // tm,),
        in_specs=[
            pl.BlockSpec((tm, kin), row),
            pl.BlockSpec((kin, d), fixed),
            pl.BlockSpec((tm, d), row),
            pl.BlockSpec((1, 6, d), lambda i: (_tile_group(i, n_p, tpb), 0, 0)),
            pl.BlockSpec((1, d), fixed),
            pl.BlockSpec((1, d), fixed),
        ],
        out_specs=pl.BlockSpec((tm, d), row),
        out_shape=jax.ShapeDtypeStruct((t, d), F32),
        compiler_params=_params(("parallel",)),
    )(o, w, x, mods_l, ln_g.reshape(1, d), ln_b.reshape(1, d))


def _gmlp_kernel(x_ref, m_ref, win_ref, bin_ref, vg_ref, vb_ref, ws_ref, bs_ref, wout_ref,
                 g_ref, b_ref, y_ref, uv_sc):
    x = x_ref[...]
    h = x * (1.0 + m_ref[0, 1:2, :]) + m_ref[0, 0:1, :]
    z = jax.nn.gelu(_mm(h.astype(BF16), win_ref[...]) + bin_ref[...])
    u = z[:, :G_DIM]
    v = _layer_norm(z[:, G_DIM:], vg_ref[...], vb_ref[...])
    tm = x.shape[0]
    for c in range(tm // G_CHUNK):
        rs = slice(c * G_CHUNK, (c + 1) * G_CHUNK)
        for g in range(G_GROUPS):
            cs = slice(g * LANES, (g + 1) * LANES)
            vm = _mm(ws_ref[g], v[rs, cs].astype(BF16)) + bs_ref[g]
            uv_sc[rs, cs] = (u[rs, cs] * vm).astype(BF16)
    y = _mm(uv_sc[...], wout_ref[...])
    zz = DN_ALPHA * x + m_ref[0, 2:3, :] * y
    y_ref[...] = _layer_norm(zz, g_ref[...], b_ref[...])


def _gmlp(x, mods_l, w_in, b_in, vg, vb, w_s, b_s, w_out, ln_g, ln_b, *, t_p, s_len):
    t, d = x.shape
    tm = TM_GMLP
    n_p = t_p // tm
    tpb = s_len // tm
    row = lambda i: (i, 0)
    fixed = lambda i: (0, 0)
    fixed3 = lambda i: (0, 0, 0)
    b_s_b = jnp.broadcast_to(b_s[:, :, None], (G_GROUPS, G_CHUNK, LANES))
    return pl.pallas_call(
        _gmlp_kernel,
        grid=(t // tm,),
        in_specs=[
            pl.BlockSpec((tm, d), row),
            pl.BlockSpec((1, 6, d), lambda i: (_tile_group(i, n_p, tpb), 0, 0)),
            pl.BlockSpec((d, 2 * G_DIM), fixed),
            pl.BlockSpec((1, 2 * G_DIM), fixed),
            pl.BlockSpec((1, G_DIM), fixed),
            pl.BlockSpec((1, G_DIM), fixed),
            pl.BlockSpec((G_GROUPS, G_CHUNK, G_CHUNK), fixed3),
            pl.BlockSpec((G_GROUPS, G_CHUNK, LANES), fixed3),
            pl.BlockSpec((G_DIM, d), fixed),
            pl.BlockSpec((1, d), fixed),
            pl.BlockSpec((1, d), fixed),
        ],
        out_specs=pl.BlockSpec((tm, d), row),
        out_shape=jax.ShapeDtypeStruct((t, d), F32),
        scratch_shapes=[pltpu.VMEM((tm, G_DIM), BF16)],
        compiler_params=_params(("parallel",)),
    )(x, mods_l, w_in.astype(BF16), b_in.reshape(1, -1), vg.reshape(1, -1), vb.reshape(1, -1),
      w_s.astype(BF16), b_s_b, w_out.astype(BF16), ln_g.reshape(1, d), ln_b.reshape(1, d))


def _top16(sc):
    n, L = sc.shape
    pos = lax.broadcasted_iota(jnp.int32, (n, L), 0).astype(F32)
    slot = lax.broadcasted_iota(jnp.int32, (P_TOPK, L), 0)

    def body(k, carry):
        cur, rank, vals = carry
        m = jnp.max(cur, axis=0, keepdims=True)
        idx = jnp.min(jnp.where(cur == m, pos, float(n)), axis=0, keepdims=True)
        hit = pos == idx
        cur = jnp.where(hit, NEG_INF, cur)
        rank = jnp.where(hit, lax.convert_element_type(k, F32), rank)
        vals = jnp.where(slot == k, m, vals)
        return cur, rank, vals

    init = (sc, jnp.full((n, L), 64.0, F32), jnp.zeros((P_TOPK, L), F32))
    _, rank, vals = lax.fori_loop(0, P_TOPK, body, init)
    return vals, rank


_CAND_BLOCKS = ((0, 0, 8), (0, 8, 8), (1, 0, 8), (2, 0, 5), (3, 0, 4), (4, 0, 3), (5, 0, 2), (6, 0, 2), (7, 0, 2))


def _pair_select(v1, v2):
    L = v1.shape[1]
    r8 = lax.broadcasted_iota(jnp.int32, (8, L), 0).astype(F32)
    blocks = []
    codes = []
    for a, b0, nb in _CAND_BLOCKS:
        c = v1[a:a + 1] + v2[b0:b0 + 8]
        if nb < 8:
            c = jnp.where(r8 < float(nb), c, NEG_INF)
        blocks.append(c)
        codes.append(r8 + float(a * P_TOPK + b0))
    blocks.append(v1[8:16] + v2[0:1])
    codes.append((r8 + 8.0) * float(P_TOPK))
    cand0 = jnp.concatenate(blocks, axis=0)
    code = jnp.concatenate(codes, axis=0)

    def body(k, carry):
        cur, sel = carry
        m = jnp.max(cur, axis=0, keepdims=True)
        idx = jnp.min(jnp.where(cur == m, code, 1e9), axis=0, keepdims=True)
        hit = code == idx
        return jnp.where(hit, NEG_INF, cur), jnp.where(hit, 1.0, sel)

    _, sel = lax.fori_loop(0, P_TOPK, body, (cand0, jnp.zeros_like(cand0)))
    top = v1[0:1] + v2[0:1]
    z = jnp.sum(jnp.where(sel > 0.0, jnp.exp(cand0 - top), 0.0), axis=0, keepdims=True)
    cnt_lo = jnp.zeros((8, L), F32)
    starts = (0, 16, 24, 32, 40, 48, 56, 64, 72)
    for a in range(8):
        n_a = jnp.sum(sel[starts[a]:starts[a + 1]], axis=0, keepdims=True)
        cnt_lo = jnp.where(r8 == float(a), n_a, cnt_lo)
    count = jnp.concatenate([cnt_lo, sel[72:80]], axis=0)
    return count, z


def _retrieve_exact(s1, s2):
    v1, r1 = _top16(s1)
    v2, r2 = _top16(s2)
    count, z = _pair_select(v1, v2)
    npos = jnp.zeros_like(r1)
    for a in range(P_TOPK):
        npos = jnp.where(r1 == float(a), count[a:a + 1], npos)
    return npos, jnp.exp(s1 - v1[0:1]) / z, r2, jnp.exp(s2 - v2[0:1])


def _oddeven_merge_sort_pairs(n):
    pairs = []
    p = 1
    while p < n:
        k = p
        while k >= 1:
            for j in range(k % p, n - k, 2 * k):
                for i in range(min(k, n - j - k)):
                    if (i + j) // (2 * p) == (i + j + k) // (2 * p):
                        pairs.append((i + j, i + j + k))
            k //= 2
        p *= 2
    return tuple(pairs)


_SORT16 = _oddeven_merge_sort_pairs(P_TOPK)


def _sort16_desc(v):
    for i, j in _SORT16:
        v[i], v[j] = jnp.maximum(v[i], v[j]), jnp.minimum(v[i], v[j])


def _bitonic_merge16_desc(v):
    dist = P_TOPK // 2
    while dist >= 1:
        for i in range(P_TOPK):
            if not i & dist:
                v[i], v[i + dist] = jnp.maximum(v[i], v[i + dist]), jnp.minimum(v[i], v[i + dist])
        dist //= 2


def _merge_top16(a, b):
    v = [jnp.maximum(a[i], b[P_TOPK - 1 - i]) for i in range(P_TOPK)]
    _bitonic_merge16_desc(v)
    return v


def _sorted_top16(s):
    v = [s[8 * k:8 * k + 8, :] for k in range(N_KEYS // 8)]
    _sort16_desc(v)
    for shift in (4, 2, 1):
        v = [jnp.maximum(v[i], pltpu.roll(v[P_TOPK - 1 - i], shift, 0)) for i in range(P_TOPK)]
        _bitonic_merge16_desc(v)
    return v


_ROW_LEN = tuple(P_TOPK // (a + 1) for a in range(P_TOPK))


def _pair_counts(p1, p2):
    rows = [[p1[a] + p2[b] for b in range(_ROW_LEN[a])] for a in range(P_TOPK)]
    ninf = jnp.full_like(p1[0], NEG_INF)
    g0 = list(rows[0])
    g1 = rows[1] + [rows[a][0] for a in range(P_TOPK - 1, 7, -1)]
    _bitonic_merge16_desc(g1)
    g2 = rows[2] + rows[3] + rows[4] + rows[5] + rows[6]
    _sort16_desc(g2)
    g3 = rows[7] + [ninf] * (P_TOPK - len(rows[7]))
    tau = _merge_top16(_merge_top16(g0, g1), _merge_top16(g2, g3))[P_TOPK - 1]
    top = rows[0][0]
    counts = []
    z = jnp.zeros_like(top)
    total = jnp.zeros_like(top)
    for a in range(P_TOPK):
        n = jnp.zeros_like(top)
        for cand in rows[a]:
            ge = cand >= tau
            n = n + jnp.where(ge, 1.0, 0.0)
            z = z + jnp.where(ge, jnp.exp(cand - top), 0.0)
        counts.append(n)
        total = total + n
    return counts, z, jnp.where(total != float(P_TOPK), 1.0, 0.0)


def _retrieve_tie_free(s1_chunks, s2_chunks):
    n_chunks = len(s1_chunks)
    assert n_chunks <= 8
    sub = lax.broadcasted_iota(jnp.int32, (8, LANES), 0)
    blocks = [slice(8 * k, 8 * k + 8) for k in range(N_KEYS // 8)]
    add = lambda p, q: p + q
    rank_sum_distinct = float(sum(range(P_TOPK)) + P_TOPK * (N_KEYS - P_TOPK))

    tops = []
    rank2 = []
    tie = jnp.zeros((8, LANES), F32)
    p1 = p2 = None
    for c in range(n_chunks):
        s1, s2 = s1_chunks[c], s2_chunks[c]
        v1 = _sorted_top16(s1)
        v2 = _sorted_top16(s2)
        tops.append((v1, v2))
        ranks = []
        for blk in blocks:
            r = jnp.zeros((8, LANES), F32)
            for a in range(P_TOPK):
                r = jnp.where(v2[a] > s2[blk], float(a + 1), r)
            ranks.append(r)
        rank2.append(jnp.concatenate(ranks, axis=0))
        rank_sum = jnp.sum(functools.reduce(add, ranks), axis=0, keepdims=True)
        tie = jnp.where(rank_sum != rank_sum_distinct, 1.0, tie)
        for a in range(P_TOPK - 1):
            tie = jnp.where(v1[a] == v1[a + 1], 1.0, tie)
        n_ge = functools.reduce(add, [jnp.where(s1[blk] >= v1[P_TOPK - 1], 1.0, 0.0) for blk in blocks])
        tie = jnp.where(jnp.sum(n_ge, axis=0, keepdims=True) != float(P_TOPK), 1.0, tie)
        if c == 0:
            p1, p2 = list(v1), list(v2)
        else:
            p1 = [jnp.where(sub == c, v, p) for v, p in zip(v1, p1)]
            p2 = [jnp.where(sub == c, v, p) for v, p in zip(v2, p2)]

    counts, z, pair_tie = _pair_counts(p1, p2)
    tie = jnp.where((sub < n_chunks) & (pair_tie != 0.0), 1.0, tie)

    tables = []
    for c in range(n_chunks):
        s1, s2 = s1_chunks[c], s2_chunks[c]
        v1, v2 = tops[c]
        cnt = [jnp.broadcast_to(n[c:c + 1, :], (8, LANES)) for n in counts] + [jnp.zeros((8, LANES), F32)]
        npos = []
        for blk in blocks:
            n = cnt[0]
            for a in range(P_TOPK):
                n = jnp.where(v1[a] > s1[blk], cnt[a + 1], n)
            npos.append(n)
        tables.append((jnp.concatenate(npos, axis=0), jnp.exp(s1 - v1[0][0:1]) / z[c:c + 1, :],
                       rank2[c], jnp.exp(s2 - v2[0][0:1])))
    return tables, tie


def _peer_topk_kernel(x_ref, m_ref, wh_ref, wl_ref, kh_ref, kl_ref,
                      h2t_ref, ni_ref, ai_ref, rj_ref, bj_ref, qt_sc):
    hd = pl.program_id(1)

    @pl.when(hd == 0)
    def _():
        x = x_ref[...]
        h2 = x * (1.0 + m_ref[0, 4:5, :]) + m_ref[0, 3:4, :]
        h_hi, h_lo = _split(h2.T)
        h2t_ref[...] = h_hi
        qt_sc[...] = _mm3(wh_ref[...], wl_ref[...], h_hi, h_lo)

    qh = qt_sc[pl.ds(pl.multiple_of(hd * (2 * P_HALF), 2 * P_HALF), 2 * P_HALF), :]
    q1_hi, q1_lo = _split(qh[:P_HALF])
    q2_hi, q2_lo = _split(qh[P_HALF:])
    sc1 = _mm3(kh_ref[0], kl_ref[0], q1_hi, q1_lo)
    sc2 = _mm3(kh_ref[1], kl_ref[1], q2_hi, q2_lo)
    n_chunks = sc1.shape[1] // LANES
    chunk = [slice(c * LANES, (c + 1) * LANES) for c in range(n_chunks)]

    def emit(c, tables):
        npos, a_i, r2, b_j = tables
        ni_ref[0, :, chunk[c]] = npos
        ai_ref[0, :, chunk[c]] = a_i
        rj_ref[0, :, chunk[c]] = r2.astype(BF16)
        bj_ref[0, :, chunk[c]] = b_j.astype(BF16)

    tables, tie = _retrieve_tie_free([sc1[:, ls] for ls in chunk], [sc2[:, ls] for ls in chunk])
    for c in range(n_chunks):
        emit(c, tables[c])

    @pl.when(jnp.max(tie) > 0.0)
    def _():
        for c in range(n_chunks):
            emit(c, _retrieve_exact(sc1[:, chunk[c]], sc2[:, chunk[c]]))


def _peer_topk(x, mods_l, wq_t_hi, wq_t_lo, keys_hi, keys_lo, *, t_p, s_len):
    t, d = x.shape
    tm = TM_TOPK
    n_p = t_p // tm
    tpb = s_len // tm
    row_tab = jax.ShapeDtypeStruct((P_HEADS, N_KEYS, t), F32)
    row_spec = pl.BlockSpec((1, N_KEYS, tm), lambda i, h: (h, 0, i))
    col_tab = jax.ShapeDtypeStruct((P_HEADS, N_KEYS, t), BF16)
    col_spec = row_spec
    return pl.pallas_call(
        _peer_topk_kernel,
        grid=(t // tm, P_HEADS),
        in_specs=[
            pl.BlockSpec((tm, d), lambda i, h: (i, 0)),
            pl.BlockSpec((1, 6, d), lambda i, h: (_tile_group(i, n_p, tpb), 0, 0)),
            pl.BlockSpec((d, d), lambda i, h: (0, 0)),
            pl.BlockSpec((d, d), lambda i, h: (0, 0)),
            pl.BlockSpec((2, N_KEYS, P_HALF), lambda i, h: (h, 0, 0)),
            pl.BlockSpec((2, N_KEYS, P_HALF), lambda i, h: (h, 0, 0)),
        ],
        out_specs=[pl.BlockSpec((d, tm), lambda i, h: (0, i)), row_spec, row_spec, col_spec, col_spec],
        out_shape=[jax.ShapeDtypeStruct((d, t), BF16), row_tab, row_tab, col_tab, col_tab],
        scratch_shapes=[pltpu.VMEM((d, tm), F32)],
        compiler_params=_params(("parallel", "arbitrary")),
    )(x, mods_l, wq_t_hi, wq_t_lo, keys_hi, keys_lo)


def _peer_dense_kernel(h2t_ref, eu_ref, evt_ref, ni_ref, ai_ref, rj_ref, bj_ref,
                       x_ref, m_ref, g_ref, b_ref, y_ref, acc_sc, act_sc, wt_sc, *, n_e):
    e = pl.program_id(1)

    @pl.when(e == 0)
    def _():
        acc_sc[...] = jnp.zeros_like(acc_sc)

    act_sc[...] = _mm(eu_ref[...], h2t_ref[...])
    te, tm = act_sc.shape
    zero = jnp.zeros((), BF16)

    for il in range(te // N_KEYS):
        rs = slice(il * N_KEYS, (il + 1) * N_KEYS)
        gate = None
        for h in range(P_HEADS):
            cnt = ni_ref[h, il:il + 1, :].astype(BF16)
            a_i = ai_ref[h, il:il + 1, :].astype(BF16)
            gh = jnp.where(rj_ref[h] < cnt, bj_ref[h], zero) * a_i
            gate = gh if gate is None else gate + gh
        wt_sc[rs, :] = gate * jax.nn.gelu(act_sc[rs, :].astype(BF16))
    acc_sc[...] += _mm(evt_ref[...], wt_sc[...])

    @pl.when(e == n_e - 1)
    def _():
        f = acc_sc[...].T
        z = DN_ALPHA * x_ref[...] + m_ref[0, 5:6, :] * f
        y_ref[...] = _layer_norm(z, g_ref[...], b_ref[...])


def _peer_dense(h2t, e_u, e_vt, ni, ai, rj, bj, x, mods_l, ln_g, ln_b, *, t_p, s_len):
    t, d = x.shape
    tm = TM_PEER
    te = TE_PEER
    n_p = t_p // tm
    tpb = s_len // tm
    n_exp = e_u.shape[0]
    n_e = n_exp // te
    n_i = te // N_KEYS
    return pl.pallas_call(
        functools.partial(_peer_dense_kernel, n_e=n_e),
        grid=(t // tm, n_e),
        in_specs=[
            pl.BlockSpec((d, tm), lambda i, e: (0, i)),
            pl.BlockSpec((te, d), lambda i, e: (e, 0)),
            pl.BlockSpec((d, te), lambda i, e: (0, e)),
            pl.BlockSpec((P_HEADS, n_i, tm), lambda i, e: (0, e, i)),
            pl.BlockSpec((P_HEADS, n_i, tm), lambda i, e: (0, e, i)),
            pl.BlockSpec((P_HEADS, N_KEYS, tm), lambda i, e: (0, 0, i)),
            pl.BlockSpec((P_HEADS, N_KEYS, tm), lambda i, e: (0, 0, i)),
            pl.BlockSpec((tm, d), lambda i, e: (i, 0)),
            pl.BlockSpec((1, 6, d), lambda i, e: (_tile_group(i, n_p, tpb), 0, 0)),
            pl.BlockSpec((1, d), lambda i, e: (0, 0)),
            pl.BlockSpec((1, d), lambda i, e: (0, 0)),
        ],
        out_specs=pl.BlockSpec((tm, d), lambda i, e: (i, 0)),
        out_shape=jax.ShapeDtypeStruct((t, d), F32),
        scratch_shapes=[pltpu.VMEM((d, tm), F32), pltpu.VMEM((te, tm), F32), pltpu.VMEM((te, tm), BF16)],
        compiler_params=_params(("parallel", "arbitrary")),
    )(h2t, e_u, e_vt, ni, ai, rj, bj, x, mods_l, ln_g.reshape(1, d), ln_b.reshape(1, d))


def _rope_tables(n_tok):
    rows = n_tok // GRID_W
    row = jnp.repeat(jnp.arange(rows), GRID_W).astype(F32)
    col = jnp.tile(jnp.arange(GRID_W), rows).astype(F32)
    n_freq = A_HD // 4
    inv = ROPE_THETA ** (-jnp.arange(n_freq, dtype=F32) / n_freq)
    ang = jnp.concatenate([row[:, None] * inv, col[:, None] * inv], -1)
    cos = jnp.repeat(jnp.cos(ang), 2, axis=-1)
    sin = jnp.repeat(jnp.sin(ang), 2, axis=-1)
    sign = jnp.tile(jnp.asarray([-1.0, 1.0], F32), A_HD // 2)
    reps = LANES // A_HD
    return jnp.tile(cos, (1, reps)), jnp.tile(sin * sign, (1, reps))


def _gqa_weights(w_qkv, q_norm, k_norm, w_o):
    d = w_qkv.shape[0]
    nq = A_HEADS * A_HD
    half = (np.arange(A_HEADS) // A_REP) % 2
    sel = np.stack([half == 0, half == 1], axis=1).astype(np.float32)
    wq = w_qkv[:, :nq].reshape(d, A_HEADS, 1, A_HD) * sel[None, :, :, None]
    w_exp = jnp.concatenate([wq.reshape(d, A_HEADS * LANES), w_qkv[:, nq:]], axis=1).astype(BF16)
    gq = jnp.tile(q_norm, 2 * A_HEADS).reshape(1, -1)
    gk = jnp.tile(k_norm, A_KV).reshape(1, -1)
    wo = w_o.reshape(A_HEADS, 1, A_HD, d) * sel[:, :, None, None]
    return w_exp, gq, gk, wo.reshape(A_HEADS * LANES, d).astype(BF16)


def _diff_weights(w_qkv):
    d = w_qkv.shape[0]
    eye = np.eye(2, dtype=np.float32)
    wq = w_qkv[:, :D_W].reshape(d, D_HEADS, 1, 2, D_HD) * eye[None, None, :, :, None]
    return jnp.concatenate([wq.reshape(d, D_HEADS * 2 * LANES), w_qkv[:, D_W:]], axis=1).astype(BF16)


def _split_f32(w):
    hi = w.astype(BF16)
    return hi, (w - hi.astype(F32)).astype(BF16)


def kernel(x_prompt, x_sample, cache_a_k, cache_a_v, cache_d_k, cache_d_v, c, c_ctx, w_mod, b_mod, ln_g, ln_b, a_w_qkv, a_q_norm, a_k_norm, a_w_o, d_w_qkv, d_lambda_q1, d_lambda_k1, d_lambda_q2, d_lambda_k2, d_sub_norm, d_w_o, g_w_in, g_b_in, g_v_norm_g, g_v_norm_b, g_w_s, g_b_s, g_w_out, p_w_q, p_sub_keys, p_expert_u, p_expert_v):
    batch, seq, d = x_prompt.shape
    dec_batch, dec_seq, _ = x_sample.shape
    past = cache_a_k.shape[2]
    t_p = batch * seq
    t_s = dec_batch * dec_seq
    dims = dict(t_p=t_p, s_len=dec_seq)

    x = jnp.concatenate([x_prompt.reshape(t_p, d), x_sample.reshape(t_s, d)], axis=0)
    n_cond = -(-(1 + dec_batch) // 8) * 8
    cond = jnp.zeros((n_cond, d), F32).at[0].set(c_ctx).at[1:1 + dec_batch].set(c)
    mods = _modulation(cond, w_mod, b_mod).reshape(DEPTH, n_cond, 6, d)
    rope_c, rope_s = _rope_tables(dec_seq)

    ak, av, dk, dv = [], [], [], []
    for i in range(DEPTH):
        kind, j = i % N_MIXERS, i // N_MIXERS
        mods_l = mods[i]
        if kind == 0:
            w_exp, gq, gk, wo = _gqa_weights(a_w_qkv[j], a_q_norm[j], a_k_norm[j], a_w_o[j])
            q, k_att, v_att, k_f, v_f = _qkv_project(
                x, mods_l, w_exp, gq, gk, rope_c, rope_s, nq=A_HEADS, nk=A_KV * A_HD // LANES,
                nv=A_KV * A_HD // LANES, use_norm=True, q_scale=A_HD ** -0.5, **dims)
            ctx_k = cache_a_k[:, j].reshape(dec_batch, past, A_KV * A_HD).astype(BF16)
            ctx_v = cache_a_v[:, j].reshape(dec_batch, past, A_KV * A_HD).astype(BF16)
            o = _attention(_gqa_attn_kernel, (), q, k_att, v_att, ctx_k, ctx_v, n_out=A_HEADS * LANES,
                           t_p=t_p, seq_p=seq, batch_s=dec_batch, seq_s=dec_seq)
            ak.append(k_f[:t_p].reshape(batch, seq, A_KV, A_HD))
            av.append(v_f[:t_p].reshape(batch, seq, A_KV, A_HD))
        elif kind == 1:
            lam_init = 0.8 - 0.6 * math.exp(-0.3 * i)
            w_exp = _diff_weights(d_w_qkv[j])
            q, k_att, v_att, k_f, v_f = _qkv_project(
                x, mods_l, w_exp, jnp.ones((1, 2 * D_W), F32), jnp.ones((1, D_W), F32),
                rope_c, rope_s, nq=2 * D_HEADS, nk=D_HEADS, nv=D_HEADS,
                use_norm=False, q_scale=D_HD ** -0.5, **dims)
            ctx_k = cache_d_k[:, j].reshape(dec_batch, past, D_W).astype(BF16)
            ctx_v = cache_d_v[:, j].reshape(dec_batch, past, D_HEADS * D_VD).astype(BF16)
            lvec = jnp.stack([d_lambda_q1[j], d_lambda_k1[j], d_lambda_q2[j], d_lambda_k2[j]])
            kern = functools.partial(_diff_attn_kernel, lam_init=lam_init)
            o = _attention(kern, (lvec, d_sub_norm[j].reshape(1, D_VD)), q, k_att, v_att, ctx_k, ctx_v,
                           n_out=D_HEADS * D_VD, t_p=t_p, seq_p=seq, batch_s=dec_batch, seq_s=dec_seq)
            wo = d_w_o[j].astype(BF16)
            dk.append(k_f[:t_p].reshape(batch, seq, D_HEADS, 2, D_HD))
            dv.append(v_f[:t_p].reshape(batch, seq, D_HEADS, D_VD))
        if kind == 2:
            x = _gmlp(x, mods_l, g_w_in[j], g_b_in[j], g_v_norm_g[j], g_v_norm_b[j], g_w_s[j], g_b_s[j],
                      g_w_out[j], ln_g[i, 0], ln_b[i, 0], **dims)
        else:
            x = _proj_ln(o, wo, x, mods_l, ln_g[i, 0], ln_b[i, 0], gate_idx=2, **dims)

        wq_hi, wq_lo = _split_f32(p_w_q[i].T)
        keys = p_sub_keys[i].reshape(P_HEADS * 2, N_KEYS, P_HALF)
        k_hi, k_lo = _split_f32(keys)
        h2t, ni, ai, rj, bj = _peer_topk(x, mods_l, wq_hi, wq_lo, k_hi, k_lo, **dims)
        x = _peer_dense(h2t, p_expert_u[i].astype(BF16), p_expert_v[i].T.astype(BF16),
                        ni, ai, rj, bj, x, mods_l, ln_g[i, 1], ln_b[i, 1], **dims)

    y_prompt = x[:t_p].reshape(batch, seq, d)
    y_sample = x[t_p:].reshape(dec_batch, dec_seq, d)
    return (y_prompt, y_sample, jnp.stack(ak, axis=1), jnp.stack(av, axis=1),
            jnp.stack(dk, axis=1), jnp.stack(dv, axis=1))
```

```python
import functools
import math

import numpy as np
import jax
import jax.numpy as jnp
from jax import lax
from jax.experimental import pallas as pl
from jax.experimental.pallas import tpu as pltpu

F32 = jnp.float32
BF16 = jnp.bfloat16

D_MODEL = 1024
DEPTH = 4
GRID_W = 64
N_MIXERS = 3
A_HEADS = 16
A_KV = 4
A_REP = A_HEADS // A_KV
A_HD = 64
D_HEADS = 8
D_HD = 64
D_VD = 2 * D_HD
D_W = D_HEADS * 2 * D_HD
G_DIM = 1024
G_GROUPS = 8
G_CHUNK = 128
P_HEADS = 8
N_KEYS = 128
P_HALF = 64
P_TOPK = 16
ROPE_THETA = 10000.0
DN_ALPHA = (2 * DEPTH) ** 0.25
EPS = 1e-6

LANES = 128
NEG_INF = float("-inf")
VMEM_LIMIT = 56 * 1024 * 1024

TM_QKV = 256
TM_PROJ = 512
TQ_ATTN = 256
TM_GMLP = 256
TM_TOPK = 512
TM_PEER = 512
TE_PEER = 2048


def _mm(a, b):
    return jnp.dot(a, b, preferred_element_type=F32)


def _mm_nt(a, b):
    return lax.dot_general(a, b, (((1,), (1,)), ((), ())), preferred_element_type=F32)


def _split(x):
    hi = x.astype(BF16)
    lo = (x - hi.astype(F32)).astype(BF16)
    return hi, lo


def _mm3(a_hi, a_lo, b_hi, b_lo):
    return _mm(a_hi, b_hi) + _mm(a_hi, b_lo) + _mm(a_lo, b_hi)


def _layer_norm(z, g, b):
    mu = jnp.mean(z, axis=-1, keepdims=True)
    zc = z - mu
    var = jnp.mean(zc * zc, axis=-1, keepdims=True)
    return zc * lax.rsqrt(var + EPS) * g + b


def _params(sem, flags=None):
    return pltpu.CompilerParams(dimension_semantics=sem, vmem_limit_bytes=VMEM_LIMIT, flags=flags)


def _tile_group(i, n_p_tiles, tiles_per_batch):
    return jnp.where(i < n_p_tiles, 0, 1 + (i - n_p_tiles) // tiles_per_batch)


def _tile_pos_block(i, n_p_tiles, tiles_per_batch):
    return jnp.where(i < n_p_tiles, 0, (i - n_p_tiles) % tiles_per_batch)


def _mods_kernel(c_ref, w_ref, b_ref, o_ref):
    c = c_ref[...]
    a = c * jax.nn.sigmoid(c)
    a_hi, a_lo = _split(a)
    w_hi, w_lo = _split(w_ref[0])
    o_ref[0] = _mm3(a_hi, a_lo, w_hi, w_lo) + b_ref[0]


def _modulation(cond, w_mod, b_mod):
    depth, d, n = w_mod.shape
    rows = cond.shape[0]
    tn = 1536
    return pl.pallas_call(
        _mods_kernel,
        grid=(depth, n // tn),
        in_specs=[
            pl.BlockSpec((rows, d), lambda l, j: (0, 0)),
            pl.BlockSpec((1, d, tn), lambda l, j: (l, 0, j)),
            pl.BlockSpec((1, 1, tn), lambda l, j: (l, 0, j)),
        ],
        out_specs=pl.BlockSpec((1, rows, tn), lambda l, j: (l, 0, j)),
        out_shape=jax.ShapeDtypeStruct((depth, rows, n), F32),
        compiler_params=_params(("parallel", "parallel")),
    )(cond, w_mod, b_mod.reshape(depth, 1, n))


def _swap_pairs(x):
    lane = lax.broadcasted_iota(jnp.int32, x.shape, 1)
    nxt = pltpu.roll(x, LANES - 1, 1)
    prv = pltpu.roll(x, 1, 1)
    return jnp.where((lane & 1) == 0, nxt, prv)


def _qkv_kernel(x_ref, m_ref, w_ref, bd_ref, gq_ref, gk_ref, cs_ref, sn_ref,
                q_ref, ka_ref, va_ref, kf_ref, vf_ref,
                *, nq, nk, nv, use_norm, n_p_tiles, q_scale):
    x = x_ref[...]
    h = x * (1.0 + m_ref[0, 1:2, :]) + m_ref[0, 0:1, :]
    y = _mm(h.astype(BF16), w_ref[...])
    is_lat = pl.program_id(0) >= n_p_tiles
    cs = jnp.where(is_lat, cs_ref[...], 1.0)
    sn = jnp.where(is_lat, sn_ref[...], 0.0)
    bd = bd_ref[...]

    def head_norm(yg, gain):
        s_hi, s_lo = _split(yg * yg)
        ms = _mm(s_hi, bd) + _mm(s_lo, bd)
        return yg * lax.rsqrt(ms + EPS) * gain

    def rope(yg):
        return yg * cs + _swap_pairs(yg) * sn

    for g in range(nq):
        sl = slice(g * LANES, (g + 1) * LANES)
        yg = y[:, sl]
        if use_norm:
            yg = head_norm(yg, gq_ref[:, sl])
        q_ref[:, sl] = (rope(yg) * q_scale).astype(BF16)
    for g in range(nk):
        sl = slice(g * LANES, (g + 1) * LANES)
        yg = y[:, (nq + g) * LANES:(nq + g + 1) * LANES]
        if use_norm:
            yg = head_norm(yg, gk_ref[:, sl])
        kf_ref[:, sl] = yg
        ka_ref[:, sl] = rope(yg).astype(BF16)
    for g in range(nv):
        sl = slice(g * LANES, (g + 1) * LANES)
        yg = y[:, (nq + nk + g) * LANES:(nq + nk + g + 1) * LANES]
        vf_ref[:, sl] = yg
        va_ref[:, sl] = yg.astype(BF16)


def _qkv_project(x, mods_l, w_exp, gq, gk, rope_c, rope_s, *, nq, nk, nv, use_norm, t_p, s_len, q_scale):
    t, d = x.shape
    tm = TM_QKV
    n_p = t_p // tm
    tpb = s_len // tm
    ncol = (nq + nk + nv) * LANES
    bd = np.kron(np.eye(2, dtype=np.float32), np.full((64, 64), 1.0 / 64.0, np.float32))
    bd = jnp.asarray(bd, BF16)
    grp = lambda i: (_tile_group(i, n_p, tpb), 0, 0)
    pos = lambda i: (_tile_pos_block(i, n_p, tpb), 0)
    row = lambda i: (i, 0)
    fixed = lambda i: (0, 0)
    kern = functools.partial(_qkv_kernel, nq=nq, nk=nk, nv=nv, use_norm=use_norm,
                             n_p_tiles=n_p, q_scale=q_scale)
    return pl.pallas_call(
        kern,
        grid=(t // tm,),
        in_specs=[
            pl.BlockSpec((tm, d), row),
            pl.BlockSpec((1, 6, d), grp),
            pl.BlockSpec((d, ncol), fixed),
            pl.BlockSpec((LANES, LANES), fixed),
            pl.BlockSpec((1, nq * LANES), fixed),
            pl.BlockSpec((1, nk * LANES), fixed),
            pl.BlockSpec((tm, LANES), pos),
            pl.BlockSpec((tm, LANES), pos),
        ],
        out_specs=[
            pl.BlockSpec((tm, nq * LANES), row),
            pl.BlockSpec((tm, nk * LANES), row),
            pl.BlockSpec((tm, nv * LANES), row),
            pl.BlockSpec((tm, nk * LANES), row),
            pl.BlockSpec((tm, nv * LANES), row),
        ],
        out_shape=[
            jax.ShapeDtypeStruct((t, nq * LANES), BF16),
            jax.ShapeDtypeStruct((t, nk * LANES), BF16),
            jax.ShapeDtypeStruct((t, nv * LANES), BF16),
            jax.ShapeDtypeStruct((t, nk * LANES), F32),
            jax.ShapeDtypeStruct((t, nv * LANES), F32),
        ],
        compiler_params=_params(("parallel",)),
    )(x, mods_l, w_exp, bd, gq, gk, rope_c, rope_s)


def _softmax_pv(qh, k_all, v_all):
    scores = [_mm_nt(qh, k) for k in k_all]
    m = scores[0].max(axis=-1, keepdims=True)
    for s in scores[1:]:
        m = jnp.maximum(m, s.max(axis=-1, keepdims=True))
    o = None
    l = None
    for s, v in zip(scores, v_all):
        p = jnp.exp(s - m)
        ls = p.sum(axis=-1, keepdims=True)
        os_ = _mm(p.astype(BF16), v)
        o = os_ if o is None else o + os_
        l = ls if l is None else l + ls
    return o, l


def _gqa_attn_kernel(*refs, has_ctx):
    if has_ctx:
        q_ref, k_ref, v_ref, kc_ref, vc_ref, o_ref = refs
    else:
        q_ref, k_ref, v_ref, o_ref = refs
    tq = q_ref.shape[0]
    for g in range(A_KV):
        ks = slice((g // 2) * LANES, (g // 2 + 1) * LANES)
        heads = [slice((g * A_REP + r) * LANES, (g * A_REP + r + 1) * LANES) for r in range(A_REP)]
        k_all = [k_ref[:, ks]]
        v_all = [v_ref[:, ks]]
        if has_ctx:
            k_all.append(kc_ref[0, :, ks])
            v_all.append(vc_ref[0, :, ks])
        o, l = _softmax_pv(jnp.concatenate([q_ref[:, sl] for sl in heads], axis=0), k_all, v_all)
        o = (o / l).astype(BF16)
        for r, sl in enumerate(heads):
            o_ref[:, sl] = o[r * tq:(r + 1) * tq, :]


def _diff_attn_kernel(*refs, has_ctx, lam_init):
    if has_ctx:
        lv_ref, sg_ref, q_ref, k_ref, v_ref, kc_ref, vc_ref, o_ref = refs
    else:
        lv_ref, sg_ref, q_ref, k_ref, v_ref, o_ref = refs
    lv = lv_ref[...]
    lam = (jnp.exp(jnp.sum(lv[0:1] * lv[1:2], axis=-1, keepdims=True))
           - jnp.exp(jnp.sum(lv[2:3] * lv[3:4], axis=-1, keepdims=True)) + lam_init)
    for h in range(D_HEADS):
        ks = slice(h * LANES, (h + 1) * LANES)
        k_all = [k_ref[:, ks]]
        v_all = [v_ref[:, ks]]
        if has_ctx:
            k_all.append(kc_ref[0, :, ks])
            v_all.append(vc_ref[0, :, ks])
        tq = q_ref.shape[0]
        q2 = jnp.concatenate([q_ref[:, (2 * h + j) * LANES:(2 * h + j + 1) * LANES] for j in range(2)], axis=0)
        o, l = _softmax_pv(q2, k_all, v_all)
        o = o / l
        o = o[:tq] - lam * o[tq:]
        ms = jnp.mean(o * o, axis=-1, keepdims=True)
        o = o * lax.rsqrt(ms + EPS) * sg_ref[...] * (1.0 - lam_init)
        o_ref[:, ks] = o.astype(BF16)


def _attention(kern, extra, q, k, v, ctx_k, ctx_v, *, n_out, t_p, seq_p, batch_s, seq_s):
    nqc = q.shape[1]
    nkc = k.shape[1]
    extra_specs = [pl.BlockSpec(a.shape, lambda *_: (0, 0)) for a in extra]
    n_seq_p = t_p // seq_p
    o_p = pl.pallas_call(
        functools.partial(kern, has_ctx=False),
        grid=(n_seq_p,),
        in_specs=extra_specs + [
            pl.BlockSpec((seq_p, nqc), lambda b: (b, 0)),
            pl.BlockSpec((seq_p, nkc), lambda b: (b, 0)),
            pl.BlockSpec((seq_p, nkc), lambda b: (b, 0)),
        ],
        out_specs=pl.BlockSpec((seq_p, n_out), lambda b: (b, 0)),
        out_shape=jax.ShapeDtypeStruct((t_p, n_out), BF16),
        compiler_params=_params(("parallel",)),
    )(*extra, q, k, v)
    tq = TQ_ATTN
    nqt = seq_s // tq
    assert t_p % seq_s == 0 and t_p % tq == 0
    q_off = t_p // tq
    k_off = t_p // seq_s
    past = ctx_k.shape[1]
    o_s = pl.pallas_call(
        functools.partial(kern, has_ctx=True),
        grid=(batch_s, nqt),
        in_specs=extra_specs + [
            pl.BlockSpec((tq, nqc), lambda b, i: (q_off + b * nqt + i, 0)),
            pl.BlockSpec((seq_s, nkc), lambda b, i: (k_off + b, 0)),
            pl.BlockSpec((seq_s, nkc), lambda b, i: (k_off + b, 0)),
            pl.BlockSpec((1, past, nkc), lambda b, i: (b, 0, 0)),
            pl.BlockSpec((1, past, nkc), lambda b, i: (b, 0, 0)),
        ],
        out_specs=pl.BlockSpec((tq, n_out), lambda b, i: (b * nqt + i, 0)),
        out_shape=jax.ShapeDtypeStruct((batch_s * seq_s, n_out), BF16),
        compiler_params=_params(("parallel", "arbitrary")),
    )(*extra, q, k, v, ctx_k, ctx_v)
    return jnp.concatenate([o_p, o_s], axis=0)


def _proj_ln_kernel(o_ref, w_ref, x_ref, m_ref, g_ref, b_ref, y_ref, *, gate_idx):
    y = _mm(o_ref[...], w_ref[...])
    z = DN_ALPHA * x_ref[...] + m_ref[0, gate_idx:gate_idx + 1, :] * y
    y_ref[...] = _layer_norm(z, g_ref[...], b_ref[...])


def _proj_ln(o, w, x, mods_l, ln_g, ln_b, *, gate_idx, t_p, s_len):
    t, d = x.shape
    tm = TM_PROJ
    n_p = t_p // tm
    tpb = s_len // tm
    kin = o.shape[1]
    row = lambda i: (i, 0)
    fixed = lambda i: (0, 0)
    return pl.pallas_call(
        functools.partial(_proj_ln_kernel, gate_idx=gate_idx),
        grid=(t // tm,),
        in_specs=[
            pl.BlockSpec((tm, kin), row),
            pl.BlockSpec((kin, d), fixed),
            pl.BlockSpec((tm, d), row),
            pl.BlockSpec((1, 6, d), lambda i: (_tile_group(i, n_p, tpb), 0, 0)),
            pl.BlockSpec((1, d), fixed),
            pl.BlockSpec((1, d), fixed),
        ],
        out_specs=pl.BlockSpec((tm, d), row),
        out_shape=jax.ShapeDtypeStruct((t, d), F32),
        compiler_params=_params(("parallel",)),
    )(o, w, x, mods_l, ln_g.reshape(1, d), ln_b.reshape(1, d))


def _gmlp_kernel(x_ref, m_ref, win_ref, bin_ref, vg_ref, vb_ref, ws_ref, bs_ref, wout_ref,
                 g_ref, b_ref, y_ref, uv_sc):
    x = x_ref[...]
    h = x * (1.0 + m_ref[0, 1:2, :]) + m_ref[0, 0:1, :]
    z = jax.nn.gelu(_mm(h.astype(BF16), win_ref[...]) + bin_ref[...])
    u = z[:, :G_DIM]
    v = _layer_norm(z[:, G_DIM:], vg_ref[...], vb_ref[...])
    tm = x.shape[0]
    for c in range(tm // G_CHUNK):
        rs = slice(c * G_CHUNK, (c + 1) * G_CHUNK)
        for g in range(G_GROUPS):
            cs = slice(g * LANES, (g + 1) * LANES)
            vm = _mm(ws_ref[g], v[rs, cs].astype(BF16)) + bs_ref[g]
            uv_sc[rs, cs] = (u[rs, cs] * vm).astype(BF16)
    y = _mm(uv_sc[...], wout_ref[...])
    zz = DN_ALPHA * x + m_ref[0, 2:3, :] * y
    y_ref[...] = _layer_norm(zz, g_ref[...], b_ref[...])


def _gmlp(x, mods_l, w_in, b_in, vg, vb, w_s, b_s, w_out, ln_g, ln_b, *, t_p, s_len):
    t, d = x.shape
    tm = TM_GMLP
    n_p = t_p // tm
    tpb = s_len // tm
    row = lambda i: (i, 0)
    fixed = lambda i: (0, 0)
    fixed3 = lambda i: (0, 0, 0)
    b_s_b = jnp.broadcast_to(b_s[:, :, None], (G_GROUPS, G_CHUNK, LANES))
    return pl.pallas_call(
        _gmlp_kernel,
        grid=(t // tm,),
        in_specs=[
            pl.BlockSpec((tm, d), row),
            pl.BlockSpec((1, 6, d), lambda i: (_tile_group(i, n_p, tpb), 0, 0)),
            pl.BlockSpec((d, 2 * G_DIM), fixed),
            pl.BlockSpec((1, 2 * G_DIM), fixed),
            pl.BlockSpec((1, G_DIM), fixed),
            pl.BlockSpec((1, G_DIM), fixed),
            pl.BlockSpec((G_GROUPS, G_CHUNK, G_CHUNK), fixed3),
            pl.BlockSpec((G_GROUPS, G_CHUNK, LANES), fixed3),
            pl.BlockSpec((G_DIM, d), fixed),
            pl.BlockSpec((1, d), fixed),
            pl.BlockSpec((1, d), fixed),
        ],
        out_specs=pl.BlockSpec((tm, d), row),
        out_shape=jax.ShapeDtypeStruct((t, d), F32),
        scratch_shapes=[pltpu.VMEM((tm, G_DIM), BF16)],
        compiler_params=_params(("parallel",)),
    )(x, mods_l, w_in.astype(BF16), b_in.reshape(1, -1), vg.reshape(1, -1), vb.reshape(1, -1),
      w_s.astype(BF16), b_s_b, w_out.astype(BF16), ln_g.reshape(1, d), ln_b.reshape(1, d))


def _top16(sc):
    n, L = sc.shape
    pos = lax.broadcasted_iota(jnp.int32, (n, L), 0).astype(F32)
    slot = lax.broadcasted_iota(jnp.int32, (P_TOPK, L), 0)

    def body(k, carry):
        cur, rank, vals = carry
        m = jnp.max(cur, axis=0, keepdims=True)
        idx = jnp.min(jnp.where(cur == m, pos, float(n)), axis=0, keepdims=True)
        hit = pos == idx
        cur = jnp.where(hit, NEG_INF, cur)
        rank = jnp.where(hit, lax.convert_element_type(k, F32), rank)
        vals = jnp.where(slot == k, m, vals)
        return cur, rank, vals

    init = (sc, jnp.full((n, L), 64.0, F32), jnp.zeros((P_TOPK, L), F32))
    _, rank, vals = lax.fori_loop(0, P_TOPK, body, init)
    return vals, rank


_CAND_BLOCKS = ((0, 0, 8), (0, 8, 8), (1, 0, 8), (2, 0, 5), (3, 0, 4), (4, 0, 3), (5, 0, 2), (6, 0, 2), (7, 0, 2))


def _pair_select(v1, v2):
    L = v1.shape[1]
    r8 = lax.broadcasted_iota(jnp.int32, (8, L), 0).astype(F32)
    blocks = []
    codes = []
    for a, b0, nb in _CAND_BLOCKS:
        c = v1[a:a + 1] + v2[b0:b0 + 8]
        if nb < 8:
            c = jnp.where(r8 < float(nb), c, NEG_INF)
        blocks.append(c)
        codes.append(r8 + float(a * P_TOPK + b0))
    blocks.append(v1[8:16] + v2[0:1])
    codes.append((r8 + 8.0) * float(P_TOPK))
    cand0 = jnp.concatenate(blocks, axis=0)
    code = jnp.concatenate(codes, axis=0)

    def body(k, carry):
        cur, sel = carry
        m = jnp.max(cur, axis=0, keepdims=True)
        idx = jnp.min(jnp.where(cur == m, code, 1e9), axis=0, keepdims=True)
        hit = code == idx
        return jnp.where(hit, NEG_INF, cur), jnp.where(hit, 1.0, sel)

    _, sel = lax.fori_loop(0, P_TOPK, body, (cand0, jnp.zeros_like(cand0)))
    top = v1[0:1] + v2[0:1]
    z = jnp.sum(jnp.where(sel > 0.0, jnp.exp(cand0 - top), 0.0), axis=0, keepdims=True)
    cnt_lo = jnp.zeros((8, L), F32)
    starts = (0, 16, 24, 32, 40, 48, 56, 64, 72)
    for a in range(8):
        n_a = jnp.sum(sel[starts[a]:starts[a + 1]], axis=0, keepdims=True)
        cnt_lo = jnp.where(r8 == float(a), n_a, cnt_lo)
    count = jnp.concatenate([cnt_lo, sel[72:80]], axis=0)
    return count, z


def _retrieve_exact(s1, s2):
    v1, r1 = _top16(s1)
    v2, r2 = _top16(s2)
    count, z = _pair_select(v1, v2)
    npos = jnp.zeros_like(r1)
    for a in range(P_TOPK):
        npos = jnp.where(r1 == float(a), count[a:a + 1], npos)
    return npos, jnp.exp(s1 - v1[0:1]) / z, r2, jnp.exp(s2 - v2[0:1])


def _oddeven_merge_sort_pairs(n):
    pairs = []
    p = 1
    while p < n:
        k = p
        while k >= 1:
            for j in range(k % p, n - k, 2 * k):
                for i in range(min(k, n - j - k)):
                    if (i + j) // (2 * p) == (i + j + k) // (2 * p):
                        pairs.append((i + j, i + j + k))
            k //= 2
        p *= 2
    return tuple(pairs)


_SORT16 = _oddeven_merge_sort_pairs(P_TOPK)


def _sort16_desc(v):
    for i, j in _SORT16:
        v[i], v[j] = jnp.maximum(v[i], v[j]), jnp.minimum(v[i], v[j])


def _bitonic_merge16_desc(v):
    dist = P_TOPK // 2
    while dist >= 1:
        for i in range(P_TOPK):
            if not i & dist:
                v[i], v[i + dist] = jnp.maximum(v[i], v[i + dist]), jnp.minimum(v[i], v[i + dist])
        dist //= 2


def _merge_top16(a, b):
    v = [jnp.maximum(a[i], b[P_TOPK - 1 - i]) for i in range(P_TOPK)]
    _bitonic_merge16_desc(v)
    return v


def _sorted_top16(s):
    v = [s[8 * k:8 * k + 8, :] for k in range(N_KEYS // 8)]
    _sort16_desc(v)
    for shift in (4, 2, 1):
        v = [jnp.maximum(v[i], pltpu.roll(v[P_TOPK - 1 - i], shift, 0)) for i in range(P_TOPK)]
        _bitonic_merge16_desc(v)
    return v


_ROW_LEN = tuple(P_TOPK // (a + 1) for a in range(P_TOPK))


def _pair_counts(p1, p2):
    rows = [[p1[a] + p2[b] for b in range(_ROW_LEN[a])] for a in range(P_TOPK)]
    ninf = jnp.full_like(p1[0], NEG_INF)
    g0 = list(rows[0])
    g1 = rows[1] + [rows[a][0] for a in range(P_TOPK - 1, 7, -1)]
    _bitonic_merge16_desc(g1)
    g2 = rows[2] + rows[3] + rows[4] + rows[5] + rows[6]
    _sort16_desc(g2)
    g3 = rows[7] + [ninf] * (P_TOPK - len(rows[7]))
    tau = _merge_top16(_merge_top16(g0, g1), _merge_top16(g2, g3))[P_TOPK - 1]
    top = rows[0][0]
    counts = []
    z = jnp.zeros_like(top)
    total = jnp.zeros_like(top)
    for a in range(P_TOPK):
        n = jnp.zeros_like(top)
        for cand in rows[a]:
            ge = cand >= tau
            n = n + jnp.where(ge, 1.0, 0.0)
            z = z + jnp.where(ge, jnp.exp(cand - top), 0.0)
        counts.append(n)
        total = total + n
    return counts, z, jnp.where(total != float(P_TOPK), 1.0, 0.0)


def _retrieve_tie_free(s1_chunks, s2_chunks):
    n_chunks = len(s1_chunks)
    assert n_chunks <= 8
    sub = lax.broadcasted_iota(jnp.int32, (8, LANES), 0)
    blocks = [slice(8 * k, 8 * k + 8) for k in range(N_KEYS // 8)]
    add = lambda p, q: p + q
    rank_sum_distinct = float(sum(range(P_TOPK)) + P_TOPK * (N_KEYS - P_TOPK))

    tops = []
    rank2 = []
    tie = jnp.zeros((8, LANES), F32)
    p1 = p2 = None
    for c in range(n_chunks):
        s1, s2 = s1_chunks[c], s2_chunks[c]
        v1 = _sorted_top16(s1)
        v2 = _sorted_top16(s2)
        tops.append((v1, v2))
        ranks = []
        for blk in blocks:
            r = jnp.zeros((8, LANES), F32)
            for a in range(P_TOPK):
                r = jnp.where(v2[a] > s2[blk], float(a + 1), r)
            ranks.append(r)
        rank2.append(jnp.concatenate(ranks, axis=0))
        rank_sum = jnp.sum(functools.reduce(add, ranks), axis=0, keepdims=True)
        tie = jnp.where(rank_sum != rank_sum_distinct, 1.0, tie)
        for a in range(P_TOPK - 1):
            tie = jnp.where(v1[a] == v1[a + 1], 1.0, tie)
        n_ge = functools.reduce(add, [jnp.where(s1[blk] >= v1[P_TOPK - 1], 1.0, 0.0) for blk in blocks])
        tie = jnp.where(jnp.sum(n_ge, axis=0, keepdims=True) != float(P_TOPK), 1.0, tie)
        if c == 0:
            p1, p2 = list(v1), list(v2)
        else:
            p1 = [jnp.where(sub == c, v, p) for v, p in zip(v1, p1)]
            p2 = [jnp.where(sub == c, v, p) for v, p in zip(v2, p2)]

    counts, z, pair_tie = _pair_counts(p1, p2)
    tie = jnp.where((sub < n_chunks) & (pair_tie != 0.0), 1.0, tie)

    tables = []
    for c in range(n_chunks):
        s1, s2 = s1_chunks[c], s2_chunks[c]
        v1, v2 = tops[c]
        cnt = [jnp.broadcast_to(n[c:c + 1, :], (8, LANES)) for n in counts] + [jnp.zeros((8, LANES), F32)]
        npos = []
        for blk in blocks:
            n = cnt[0]
            for a in range(P_TOPK):
                n = jnp.where(v1[a] > s1[blk], cnt[a + 1], n)
            npos.append(n)
        tables.append((jnp.concatenate(npos, axis=0), jnp.exp(s1 - v1[0][0:1]) / z[c:c + 1, :],
                       rank2[c], jnp.exp(s2 - v2[0][0:1])))
    return tables, tie


def _dup_bf16_word(x):
    u = lax.bitcast_convert_type(x.astype(BF16).astype(F32), jnp.uint32)
    return lax.bitcast_convert_type(u | (u >> 16), jnp.int32)


def _peer_topk_kernel(x_ref, m_ref, wh_ref, wl_ref, kh_ref, kl_ref,
                      h2t_ref, ni_ref, ai_ref, rj_ref, bj_ref, qt_sc):
    hd = pl.program_id(1)

    @pl.when(hd == 0)
    def _():
        x = x_ref[...]
        h2 = x * (1.0 + m_ref[0, 4:5, :]) + m_ref[0, 3:4, :]
        h_hi, h_lo = _split(h2.T)
        h2t_ref[...] = h_hi
        qt_sc[...] = _mm3(wh_ref[...], wl_ref[...], h_hi, h_lo)

    qh = qt_sc[pl.ds(pl.multiple_of(hd * (2 * P_HALF), 2 * P_HALF), 2 * P_HALF), :]
    q1_hi, q1_lo = _split(qh[:P_HALF])
    q2_hi, q2_lo = _split(qh[P_HALF:])
    sc1 = _mm3(kh_ref[0], kl_ref[0], q1_hi, q1_lo)
    sc2 = _mm3(kh_ref[1], kl_ref[1], q2_hi, q2_lo)
    n_chunks = sc1.shape[1] // LANES
    chunk = [slice(c * LANES, (c + 1) * LANES) for c in range(n_chunks)]

    def emit(c, tables):
        npos, a_i, r2, b_j = tables
        ni_ref[0, :, chunk[c]] = _dup_bf16_word(npos)
        ai_ref[0, :, chunk[c]] = _dup_bf16_word(0.5 * a_i)
        rj_ref[0, :, chunk[c]] = r2.astype(BF16)
        bj_ref[0, :, chunk[c]] = b_j.astype(BF16)

    tables, tie = _retrieve_tie_free([sc1[:, ls] for ls in chunk], [sc2[:, ls] for ls in chunk])
    for c in range(n_chunks):
        emit(c, tables[c])

    @pl.when(jnp.max(tie) > 0.0)
    def _():
        for c in range(n_chunks):
            emit(c, _retrieve_exact(sc1[:, chunk[c]], sc2[:, chunk[c]]))


def _peer_topk(x, mods_l, wq_t_hi, wq_t_lo, keys_hi, keys_lo, *, t_p, s_len):
    t, d = x.shape
    tm = TM_TOPK
    n_p = t_p // tm
    tpb = s_len // tm
    row_tab = jax.ShapeDtypeStruct((P_HEADS, N_KEYS, t), jnp.int32)
    row_spec = pl.BlockSpec((1, N_KEYS, tm), lambda i, h: (h, 0, i))
    col_tab = jax.ShapeDtypeStruct((P_HEADS, N_KEYS, t), BF16)
    col_spec = row_spec
    return pl.pallas_call(
        _peer_topk_kernel,
        grid=(t // tm, P_HEADS),
        in_specs=[
            pl.BlockSpec((tm, d), lambda i, h: (i, 0)),
            pl.BlockSpec((1, 6, d), lambda i, h: (_tile_group(i, n_p, tpb), 0, 0)),
            pl.BlockSpec((d, d), lambda i, h: (0, 0)),
            pl.BlockSpec((d, d), lambda i, h: (0, 0)),
            pl.BlockSpec((2, N_KEYS, P_HALF), lambda i, h: (h, 0, 0)),
            pl.BlockSpec((2, N_KEYS, P_HALF), lambda i, h: (h, 0, 0)),
        ],
        out_specs=[pl.BlockSpec((d, tm), lambda i, h: (0, i)), row_spec, row_spec, col_spec, col_spec],
        out_shape=[jax.ShapeDtypeStruct((d, t), BF16), row_tab, row_tab, col_tab, col_tab],
        scratch_shapes=[pltpu.VMEM((d, tm), F32)],
        compiler_params=_params(("parallel", "arbitrary")),
    )(x, mods_l, wq_t_hi, wq_t_lo, keys_hi, keys_lo)


def _row_tile(row):
    tile = pltpu.bitcast(jnp.broadcast_to(row, (8, row.shape[1])), BF16)
    return jnp.concatenate([tile] * (N_KEYS // tile.shape[0]), axis=0)


GELU_C1 = math.sqrt(2.0 / math.pi)
GELU_C2 = 0.044715 * GELU_C1
K_CHUNK = 512


def _peer_dense_kernel(h2t_ref, eu0_ref, eun_ref, evt_ref, ni_ref, ai_ref, rj_ref, bj_ref,
                       x_ref, m_ref, g_ref, b_ref, y_ref, acc_sc, act0_sc, act1_sc, wt_sc, *, n_e):
    e = pl.program_id(1)

    @pl.when(e == 0)
    def _():
        acc_sc[...] = jnp.zeros_like(acc_sc)
        act0_sc[...] = _mm(eu0_ref[...], h2t_ref[...]).astype(BF16)

    te, tm = wt_sc.shape
    zero = jnp.zeros((), BF16)

    def step(act_ref, next_ref):
        part = None
        for kc in range(te // K_CHUNK):
            for il in range(kc * K_CHUNK // N_KEYS, (kc + 1) * K_CHUNK // N_KEYS):
                rs = slice(il * N_KEYS, (il + 1) * N_KEYS)
                gate = None
                for h in range(P_HEADS):
                    cnt = _row_tile(ni_ref[h, il:il + 1, :])
                    half_a = _row_tile(ai_ref[h, il:il + 1, :])
                    gh = jnp.where(rj_ref[h] < cnt, bj_ref[h], zero) * half_a
                    gate = gh if gate is None else gate + gh
                a = act_ref[rs, :]
                t = jnp.tanh(a * (GELU_C1 + GELU_C2 * (a * a)))
                wt_sc[rs, :] = gate * (a + a * t)
            rows = slice(kc * K_CHUNK, (kc + 1) * K_CHUNK)
            p = _mm(evt_ref[:, rows], wt_sc[rows, :])
            part = p if part is None else part + p
        acc_sc[...] += part
        next_ref[...] = _mm(eun_ref[...], h2t_ref[...]).astype(BF16)

    @pl.when(e % 2 == 0)
    def _():
        step(act0_sc, act1_sc)

    @pl.when(e % 2 == 1)
    def _():
        step(act1_sc, act0_sc)

    @pl.when(e == n_e - 1)
    def _():
        f = acc_sc[...].T
        z = DN_ALPHA * x_ref[...] + m_ref[0, 5:6, :] * f
        y_ref[...] = _layer_norm(z, g_ref[...], b_ref[...])


def _peer_dense(h2t, e_u, e_vt, ni, ai, rj, bj, x, mods_l, ln_g, ln_b, *, t_p, s_len):
    t, d = x.shape
    tm = TM_PEER
    te = TE_PEER
    n_p = t_p // tm
    tpb = s_len // tm
    n_exp = e_u.shape[0]
    n_e = n_exp // te
    n_i = te // N_KEYS
    return pl.pallas_call(
        functools.partial(_peer_dense_kernel, n_e=n_e),
        grid=(t // tm, n_e),
        in_specs=[
            pl.BlockSpec((d, tm), lambda i, e: (0, i)),
            pl.BlockSpec((te, d), lambda i, e: (0, 0)),
            pl.BlockSpec((te, d), lambda i, e: (jnp.minimum(e + 1, n_e - 1), 0)),
            pl.BlockSpec((d, te), lambda i, e: (0, e)),
            pl.BlockSpec((P_HEADS, n_i, tm), lambda i, e: (0, e, i)),
            pl.BlockSpec((P_HEADS, n_i, tm), lambda i, e: (0, e, i)),
            pl.BlockSpec((P_HEADS, N_KEYS, tm), lambda i, e: (0, 0, i)),
            pl.BlockSpec((P_HEADS, N_KEYS, tm), lambda i, e: (0, 0, i)),
            pl.BlockSpec((tm, d), lambda i, e: (i, 0)),
            pl.BlockSpec((1, 6, d), lambda i, e: (_tile_group(i, n_p, tpb), 0, 0)),
            pl.BlockSpec((1, d), lambda i, e: (0, 0)),
            pl.BlockSpec((1, d), lambda i, e: (0, 0)),
        ],
        out_specs=pl.BlockSpec((tm, d), lambda i, e: (i, 0)),
        out_shape=jax.ShapeDtypeStruct((t, d), F32),
        scratch_shapes=[pltpu.VMEM((d, tm), F32), pltpu.VMEM((te, tm), BF16), pltpu.VMEM((te, tm), BF16),
                        pltpu.VMEM((te, tm), BF16)],
        compiler_params=_params(("parallel", "arbitrary")),
    )(h2t, e_u, e_u, e_vt, ni, ai, rj, bj, x, mods_l, ln_g.reshape(1, d), ln_b.reshape(1, d))


def _rope_tables(n_tok):
    rows = n_tok // GRID_W
    row = jnp.repeat(jnp.arange(rows), GRID_W).astype(F32)
    col = jnp.tile(jnp.arange(GRID_W), rows).astype(F32)
    n_freq = A_HD // 4
    inv = ROPE_THETA ** (-jnp.arange(n_freq, dtype=F32) / n_freq)
    ang = jnp.concatenate([row[:, None] * inv, col[:, None] * inv], -1)
    cos = jnp.repeat(jnp.cos(ang), 2, axis=-1)
    sin = jnp.repeat(jnp.sin(ang), 2, axis=-1)
    sign = jnp.tile(jnp.asarray([-1.0, 1.0], F32), A_HD // 2)
    reps = LANES // A_HD
    return jnp.tile(cos, (1, reps)), jnp.tile(sin * sign, (1, reps))


def _gqa_weights(w_qkv, q_norm, k_norm, w_o):
    d = w_qkv.shape[0]
    nq = A_HEADS * A_HD
    half = (np.arange(A_HEADS) // A_REP) % 2
    sel = np.stack([half == 0, half == 1], axis=1).astype(np.float32)
    wq = w_qkv[:, :nq].reshape(d, A_HEADS, 1, A_HD) * sel[None, :, :, None]
    w_exp = jnp.concatenate([wq.reshape(d, A_HEADS * LANES), w_qkv[:, nq:]], axis=1).astype(BF16)
    gq = jnp.tile(q_norm, 2 * A_HEADS).reshape(1, -1)
    gk = jnp.tile(k_norm, A_KV).reshape(1, -1)
    wo = w_o.reshape(A_HEADS, 1, A_HD, d) * sel[:, :, None, None]
    return w_exp, gq, gk, wo.reshape(A_HEADS * LANES, d).astype(BF16)


def _diff_weights(w_qkv):
    d = w_qkv.shape[0]
    eye = np.eye(2, dtype=np.float32)
    wq = w_qkv[:, :D_W].reshape(d, D_HEADS, 1, 2, D_HD) * eye[None, None, :, :, None]
    return jnp.concatenate([wq.reshape(d, D_HEADS * 2 * LANES), w_qkv[:, D_W:]], axis=1).astype(BF16)


def _split_f32(w):
    hi = w.astype(BF16)
    return hi, (w - hi.astype(F32)).astype(BF16)


def kernel(x_prompt, x_sample, cache_a_k, cache_a_v, cache_d_k, cache_d_v, c, c_ctx, w_mod, b_mod, ln_g, ln_b, a_w_qkv, a_q_norm, a_k_norm, a_w_o, d_w_qkv, d_lambda_q1, d_lambda_k1, d_lambda_q2, d_lambda_k2, d_sub_norm, d_w_o, g_w_in, g_b_in, g_v_norm_g, g_v_norm_b, g_w_s, g_b_s, g_w_out, p_w_q, p_sub_keys, p_expert_u, p_expert_v):
    batch, seq, d = x_prompt.shape
    dec_batch, dec_seq, _ = x_sample.shape
    past = cache_a_k.shape[2]
    t_p = batch * seq
    t_s = dec_batch * dec_seq
    dims = dict(t_p=t_p, s_len=dec_seq)

    x = jnp.concatenate([x_prompt.reshape(t_p, d), x_sample.reshape(t_s, d)], axis=0)
    n_cond = -(-(1 + dec_batch) // 8) * 8
    cond = jnp.zeros((n_cond, d), F32).at[0].set(c_ctx).at[1:1 + dec_batch].set(c)
    mods = _modulation(cond, w_mod, b_mod).reshape(DEPTH, n_cond, 6, d)
    rope_c, rope_s = _rope_tables(dec_seq)

    ak, av, dk, dv = [], [], [], []
    for i in range(DEPTH):
        kind, j = i % N_MIXERS, i // N_MIXERS
        mods_l = mods[i]
        if kind == 0:
            w_exp, gq, gk, wo = _gqa_weights(a_w_qkv[j], a_q_norm[j], a_k_norm[j], a_w_o[j])
            q, k_att, v_att, k_f, v_f = _qkv_project(
                x, mods_l, w_exp, gq, gk, rope_c, rope_s, nq=A_HEADS, nk=A_KV * A_HD // LANES,
                nv=A_KV * A_HD // LANES, use_norm=True, q_scale=A_HD ** -0.5, **dims)
            ctx_k = cache_a_k[:, j].reshape(dec_batch, past, A_KV * A_HD).astype(BF16)
            ctx_v = cache_a_v[:, j].reshape(dec_batch, past, A_KV * A_HD).astype(BF16)
            o = _attention(_gqa_attn_kernel, (), q, k_att, v_att, ctx_k, ctx_v, n_out=A_HEADS * LANES,
                           t_p=t_p, seq_p=seq, batch_s=dec_batch, seq_s=dec_seq)
            ak.append(k_f[:t_p].reshape(batch, seq, A_KV, A_HD))
            av.append(v_f[:t_p].reshape(batch, seq, A_KV, A_HD))
        elif kind == 1:
            lam_init = 0.8 - 0.6 * math.exp(-0.3 * i)
            w_exp = _diff_weights(d_w_qkv[j])
            q, k_att, v_att, k_f, v_f = _qkv_project(
                x, mods_l, w_exp, jnp.ones((1, 2 * D_W), F32), jnp.ones((1, D_W), F32),
                rope_c, rope_s, nq=2 * D_HEADS, nk=D_HEADS, nv=D_HEADS,
                use_norm=False, q_scale=D_HD ** -0.5, **dims)
            ctx_k = cache_d_k[:, j].reshape(dec_batch, past, D_W).astype(BF16)
            ctx_v = cache_d_v[:, j].reshape(dec_batch, past, D_HEADS * D_VD).astype(BF16)
            lvec = jnp.stack([d_lambda_q1[j], d_lambda_k1[j], d_lambda_q2[j], d_lambda_k2[j]])
            kern = functools.partial(_diff_attn_kernel, lam_init=lam_init)
            o = _attention(kern, (lvec, d_sub_norm[j].reshape(1, D_VD)), q, k_att, v_att, ctx_k, ctx_v,
                           n_out=D_HEADS * D_VD, t_p=t_p, seq_p=seq, batch_s=dec_batch, seq_s=dec_seq)
            wo = d_w_o[j].astype(BF16)
            dk.append(k_f[:t_p].reshape(batch, seq, D_HEADS, 2, D_HD))
            dv.append(v_f[:t_p].reshape(batch, seq, D_HEADS, D_VD))
        if kind == 2:
            x = _gmlp(x, mods_l, g_w_in[j], g_b_in[j], g_v_norm_g[j], g_v_norm_b[j], g_w_s[j], g_b_s[j],
                      g_w_out[j], ln_g[i, 0], ln_b[i, 0], **dims)
        else:
            x = _proj_ln(o, wo, x, mods_l, ln_g[i, 0], ln_b[i, 0], gate_idx=2, **dims)

        wq_hi, wq_lo = _split_f32(p_w_q[i].T)
        keys = p_sub_keys[i].reshape(P_HEADS * 2, N_KEYS, P_HALF)
        k_hi, k_lo = _split_f32(keys)
        h2t, ni, ai, rj, bj = _peer_topk(x, mods_l, wq_hi, wq_lo, k_hi, k_lo, **dims)
        x = _peer_dense(h2t, p_expert_u[i].astype(BF16), p_expert_v[i].T.astype(BF16),
                        ni, ai, rj, bj, x, mods_l, ln_g[i, 1], ln_b[i, 1], **dims)

    y_prompt = x[:t_p].reshape(batch, seq, d)
    y_sample = x[t_p:].reshape(dec_batch, dec_seq, d)
    return (y_prompt, y_sample, jnp.stack(ak, axis=1), jnp.stack(av, axis=1),
            jnp.stack(dk, axis=1), jnp.stack(dv, axis=1))
```

```python
import functools
import math

import numpy as np
import jax
import jax.numpy as jnp
from jax import lax
from jax.experimental import pallas as pl
from jax.experimental.pallas import tpu as pltpu

F32 = jnp.float32
BF16 = jnp.bfloat16

D_MODEL = 1024
DEPTH = 4
GRID_W = 64
N_MIXERS = 3
A_HEADS = 16
A_KV = 4
A_REP = A_HEADS // A_KV
A_HD = 64
D_HEADS = 8
D_HD = 64
D_VD = 2 * D_HD
D_W = D_HEADS * 2 * D_HD
G_DIM = 1024
G_GROUPS = 8
G_CHUNK = 128
P_HEADS = 8
N_KEYS = 128
P_HALF = 64
P_TOPK = 16
ROPE_THETA = 10000.0
DN_ALPHA = (2 * DEPTH) ** 0.25
EPS = 1e-6

LANES = 128
NEG_INF = float("-inf")
VMEM_LIMIT = 56 * 1024 * 1024

TM_QKV = 256
TM_PROJ = 512
TQ_ATTN = 256
TM_GMLP = 256
TM_TOPK = 512
TM_PEER = 512
TE_PEER = 2048


def _mm(a, b):
    return jnp.dot(a, b, preferred_element_type=F32)


def _mm_nt(a, b):
    return lax.dot_general(a, b, (((1,), (1,)), ((), ())), preferred_element_type=F32)


def _split(x):
    hi = x.astype(BF16)
    lo = (x - hi.astype(F32)).astype(BF16)
    return hi, lo


def _mm3(a_hi, a_lo, b_hi, b_lo):
    return _mm(a_hi, b_hi) + _mm(a_hi, b_lo) + _mm(a_lo, b_hi)


def _layer_norm(z, g, b):
    mu = jnp.mean(z, axis=-1, keepdims=True)
    zc = z - mu
    var = jnp.mean(zc * zc, axis=-1, keepdims=True)
    return zc * lax.rsqrt(var + EPS) * g + b


def _params(sem, flags=None):
    return pltpu.CompilerParams(dimension_semantics=sem, vmem_limit_bytes=VMEM_LIMIT, flags=flags)


def _tile_group(i, n_p_tiles, tiles_per_batch):
    return jnp.where(i < n_p_tiles, 0, 1 + (i - n_p_tiles) // tiles_per_batch)


def _tile_pos_block(i, n_p_tiles, tiles_per_batch):
    return jnp.where(i < n_p_tiles, 0, (i - n_p_tiles) % tiles_per_batch)


def _mods_kernel(c_ref, w_ref, b_ref, o_ref):
    c = c_ref[...]
    a = c * jax.nn.sigmoid(c)
    a_hi, a_lo = _split(a)
    w_hi, w_lo = _split(w_ref[0])
    o_ref[0] = _mm3(a_hi, a_lo, w_hi, w_lo) + b_ref[0]


def _modulation(cond, w_mod, b_mod):
    depth, d, n = w_mod.shape
    rows = cond.shape[0]
    tn = 1536
    return pl.pallas_call(
        _mods_kernel,
        grid=(depth, n // tn),
        in_specs=[
            pl.BlockSpec((rows, d), lambda l, j: (0, 0)),
            pl.BlockSpec((1, d, tn), lambda l, j: (l, 0, j)),
            pl.BlockSpec((1, 1, tn), lambda l, j: (l, 0, j)),
        ],
        out_specs=pl.BlockSpec((1, rows, tn), lambda l, j: (l, 0, j)),
        out_shape=jax.ShapeDtypeStruct((depth, rows, n), F32),
        compiler_params=_params(("parallel", "parallel")),
    )(cond, w_mod, b_mod.reshape(depth, 1, n))


def _swap_pairs(x):
    lane = lax.broadcasted_iota(jnp.int32, x.shape, 1)
    nxt = pltpu.roll(x, LANES - 1, 1)
    prv = pltpu.roll(x, 1, 1)
    return jnp.where((lane & 1) == 0, nxt, prv)


def _qkv_kernel(x_ref, m_ref, w_ref, bd_ref, gq_ref, gk_ref, cs_ref, sn_ref,
                q_ref, ka_ref, va_ref, kf_ref, vf_ref,
                *, nq, nk, nv, use_norm, n_p_tiles, q_scale):
    x = x_ref[...]
    h = x * (1.0 + m_ref[0, 1:2, :]) + m_ref[0, 0:1, :]
    y = _mm(h.astype(BF16), w_ref[...])
    is_lat = pl.program_id(0) >= n_p_tiles
    cs = jnp.where(is_lat, cs_ref[...], 1.0)
    sn = jnp.where(is_lat, sn_ref[...], 0.0)
    bd = bd_ref[...]

    def head_norm(yg, gain):
        s_hi, s_lo = _split(yg * yg)
        ms = _mm(s_hi, bd) + _mm(s_lo, bd)
        return yg * lax.rsqrt(ms + EPS) * gain

    def rope(yg):
        return yg * cs + _swap_pairs(yg) * sn

    for g in range(nq):
        sl = slice(g * LANES, (g + 1) * LANES)
        yg = y[:, sl]
        if use_norm:
            yg = head_norm(yg, gq_ref[:, sl])
        q_ref[:, sl] = (rope(yg) * q_scale).astype(BF16)
    for g in range(nk):
        sl = slice(g * LANES, (g + 1) * LANES)
        yg = y[:, (nq + g) * LANES:(nq + g + 1) * LANES]
        if use_norm:
            yg = head_norm(yg, gk_ref[:, sl])
        kf_ref[:, sl] = yg
        ka_ref[:, sl] = rope(yg).astype(BF16)
    for g in range(nv):
        sl = slice(g * LANES, (g + 1) * LANES)
        yg = y[:, (nq + nk + g) * LANES:(nq + nk + g + 1) * LANES]
        vf_ref[:, sl] = yg
        va_ref[:, sl] = yg.astype(BF16)


def _qkv_project(x, mods_l, w_exp, gq, gk, rope_c, rope_s, *, nq, nk, nv, use_norm, t_p, s_len, q_scale):
    t, d = x.shape
    tm = TM_QKV
    n_p = t_p // tm
    tpb = s_len // tm
    ncol = (nq + nk + nv) * LANES
    bd = np.kron(np.eye(2, dtype=np.float32), np.full((64, 64), 1.0 / 64.0, np.float32))
    bd = jnp.asarray(bd, BF16)
    grp = lambda i: (_tile_group(i, n_p, tpb), 0, 0)
    pos = lambda i: (_tile_pos_block(i, n_p, tpb), 0)
    row = lambda i: (i, 0)
    fixed = lambda i: (0, 0)
    kern = functools.partial(_qkv_kernel, nq=nq, nk=nk, nv=nv, use_norm=use_norm,
                             n_p_tiles=n_p, q_scale=q_scale)
    return pl.pallas_call(
        kern,
        grid=(t // tm,),
        in_specs=[
            pl.BlockSpec((tm, d), row),
            pl.BlockSpec((1, 6, d), grp),
            pl.BlockSpec((d, ncol), fixed),
            pl.BlockSpec((LANES, LANES), fixed),
            pl.BlockSpec((1, nq * LANES), fixed),
            pl.BlockSpec((1, nk * LANES), fixed),
            pl.BlockSpec((tm, LANES), pos),
            pl.BlockSpec((tm, LANES), pos),
        ],
        out_specs=[
            pl.BlockSpec((tm, nq * LANES), row),
            pl.BlockSpec((tm, nk * LANES), row),
            pl.BlockSpec((tm, nv * LANES), row),
            pl.BlockSpec((tm, nk * LANES), row),
            pl.BlockSpec((tm, nv * LANES), row),
        ],
        out_shape=[
            jax.ShapeDtypeStruct((t, nq * LANES), BF16),
            jax.ShapeDtypeStruct((t, nk * LANES), BF16),
            jax.ShapeDtypeStruct((t, nv * LANES), BF16),
            jax.ShapeDtypeStruct((t, nk * LANES), F32),
            jax.ShapeDtypeStruct((t, nv * LANES), F32),
        ],
        compiler_params=_params(("parallel",)),
    )(x, mods_l, w_exp, bd, gq, gk, rope_c, rope_s)


def _softmax_pv(qh, k_all, v_all):
    scores = [_mm_nt(qh, k) for k in k_all]
    m = scores[0].max(axis=-1, keepdims=True)
    for s in scores[1:]:
        m = jnp.maximum(m, s.max(axis=-1, keepdims=True))
    o = None
    l = None
    for s, v in zip(scores, v_all):
        p = jnp.exp(s - m)
        ls = p.sum(axis=-1, keepdims=True)
        os_ = _mm(p.astype(BF16), v)
        o = os_ if o is None else o + os_
        l = ls if l is None else l + ls
    return o, l


def _gqa_attn_kernel(*refs, has_ctx):
    if has_ctx:
        q_ref, k_ref, v_ref, kc_ref, vc_ref, o_ref = refs
    else:
        q_ref, k_ref, v_ref, o_ref = refs
    tq = q_ref.shape[0]
    for g in range(A_KV):
        ks = slice((g // 2) * LANES, (g // 2 + 1) * LANES)
        heads = [slice((g * A_REP + r) * LANES, (g * A_REP + r + 1) * LANES) for r in range(A_REP)]
        k_all = [k_ref[:, ks]]
        v_all = [v_ref[:, ks]]
        if has_ctx:
            k_all.append(kc_ref[0, :, ks])
            v_all.append(vc_ref[0, :, ks])
        o, l = _softmax_pv(jnp.concatenate([q_ref[:, sl] for sl in heads], axis=0), k_all, v_all)
        o = (o / l).astype(BF16)
        for r, sl in enumerate(heads):
            o_ref[:, sl] = o[r * tq:(r + 1) * tq, :]


def _diff_attn_kernel(*refs, has_ctx, lam_init):
    if has_ctx:
        lv_ref, sg_ref, q_ref, k_ref, v_ref, kc_ref, vc_ref, o_ref = refs
    else:
        lv_ref, sg_ref, q_ref, k_ref, v_ref, o_ref = refs
    lv = lv_ref[...]
    lam = (jnp.exp(jnp.sum(lv[0:1] * lv[1:2], axis=-1, keepdims=True))
           - jnp.exp(jnp.sum(lv[2:3] * lv[3:4], axis=-1, keepdims=True)) + lam_init)
    for h in range(D_HEADS):
        ks = slice(h * LANES, (h + 1) * LANES)
        k_all = [k_ref[:, ks]]
        v_all = [v_ref[:, ks]]
        if has_ctx:
            k_all.append(kc_ref[0, :, ks])
            v_all.append(vc_ref[0, :, ks])
        tq = q_ref.shape[0]
        q2 = jnp.concatenate([q_ref[:, (2 * h + j) * LANES:(2 * h + j + 1) * LANES] for j in range(2)], axis=0)
        o, l = _softmax_pv(q2, k_all, v_all)
        o = o / l
        o = o[:tq] - lam * o[tq:]
        ms = jnp.mean(o * o, axis=-1, keepdims=True)
        o = o * lax.rsqrt(ms + EPS) * sg_ref[...] * (1.0 - lam_init)
        o_ref[:, ks] = o.astype(BF16)


def _attention(kern, extra, q, k, v, ctx_k, ctx_v, *, n_out, t_p, seq_p, batch_s, seq_s):
    nqc = q.shape[1]
    nkc = k.shape[1]
    extra_specs = [pl.BlockSpec(a.shape, lambda *_: (0, 0)) for a in extra]
    n_seq_p = t_p // seq_p
    o_p = pl.pallas_call(
        functools.partial(kern, has_ctx=False),
        grid=(n_seq_p,),
        in_specs=extra_specs + [
            pl.BlockSpec((seq_p, nqc), lambda b: (b, 0)),
            pl.BlockSpec((seq_p, nkc), lambda b: (b, 0)),
            pl.BlockSpec((seq_p, nkc), lambda b: (b, 0)),
        ],
        out_specs=pl.BlockSpec((seq_p, n_out), lambda b: (b, 0)),
        out_shape=jax.ShapeDtypeStruct((t_p, n_out), BF16),
        compiler_params=_params(("parallel",)),
    )(*extra, q, k, v)
    tq = TQ_ATTN
    nqt = seq_s // tq
    assert t_p % seq_s == 0 and t_p % tq == 0
    q_off = t_p // tq
    k_off = t_p // seq_s
    past = ctx_k.shape[1]
    o_s = pl.pallas_call(
        functools.partial(kern, has_ctx=True),
        grid=(batch_s, nqt),
        in_specs=extra_specs + [
            pl.BlockSpec((tq, nqc), lambda b, i: (q_off + b * nqt + i, 0)),
            pl.BlockSpec((seq_s, nkc), lambda b, i: (k_off + b, 0)),
            pl.BlockSpec((seq_s, nkc), lambda b, i: (k_off + b, 0)),
            pl.BlockSpec((1, past, nkc), lambda b, i: (b, 0, 0)),
            pl.BlockSpec((1, past, nkc), lambda b, i: (b, 0, 0)),
        ],
        out_specs=pl.BlockSpec((tq, n_out), lambda b, i: (b * nqt + i, 0)),
        out_shape=jax.ShapeDtypeStruct((batch_s * seq_s, n_out), BF16),
        compiler_params=_params(("parallel", "arbitrary")),
    )(*extra, q, k, v, ctx_k, ctx_v)
    return o_p, o_s


def _proj_ln_kernel(op_ref, os_ref, w_ref, x_ref, m_ref, g_ref, b_ref, y_ref, *, gate_idx, n_p_tiles):
    o = jnp.where(pl.program_id(0) < n_p_tiles, op_ref[...], os_ref[...])
    y = _mm(o, w_ref[...])
    z = DN_ALPHA * x_ref[...] + m_ref[0, gate_idx:gate_idx + 1, :] * y
    y_ref[...] = _layer_norm(z, g_ref[...], b_ref[...])


def _proj_ln(o_p, o_s, w, x, mods_l, ln_g, ln_b, *, gate_idx, t_p, s_len):
    t, d = x.shape
    tm = TM_PROJ
    n_p = t_p // tm
    tpb = s_len // tm
    kin = o_p.shape[1]
    row = lambda i: (i, 0)
    fixed = lambda i: (0, 0)
    return pl.pallas_call(
        functools.partial(_proj_ln_kernel, gate_idx=gate_idx, n_p_tiles=n_p),
        grid=(t // tm,),
        in_specs=[
            pl.BlockSpec((tm, kin), lambda i: (jnp.minimum(i, n_p - 1), 0)),
            pl.BlockSpec((tm, kin), lambda i: (jnp.maximum(i - n_p, 0), 0)),
            pl.BlockSpec((kin, d), fixed),
            pl.BlockSpec((tm, d), row),
            pl.BlockSpec((1, 6, d), lambda i: (_tile_group(i, n_p, tpb), 0, 0)),
            pl.BlockSpec((1, d), fixed),
            pl.BlockSpec((1, d), fixed),
        ],
        out_specs=pl.BlockSpec((tm, d), row),
        out_shape=jax.ShapeDtypeStruct((t, d), F32),
        compiler_params=_params(("parallel",)),
    )(o_p, o_s, w, x, mods_l, ln_g.reshape(1, d), ln_b.reshape(1, d))


def _gmlp_kernel(x_ref, m_ref, win_ref, bin_ref, vg_ref, vb_ref, ws_ref, bs_ref, wout_ref,
                 g_ref, b_ref, y_ref, uv_sc):
    x = x_ref[...]
    h = x * (1.0 + m_ref[0, 1:2, :]) + m_ref[0, 0:1, :]
    z = jax.nn.gelu(_mm(h.astype(BF16), win_ref[...]) + bin_ref[...])
    u = z[:, :G_DIM]
    v = _layer_norm(z[:, G_DIM:], vg_ref[...], vb_ref[...])
    tm = x.shape[0]
    for c in range(tm // G_CHUNK):
        rs = slice(c * G_CHUNK, (c + 1) * G_CHUNK)
        for g in range(G_GROUPS):
            cs = slice(g * LANES, (g + 1) * LANES)
            vm = _mm(ws_ref[g], v[rs, cs].astype(BF16)) + bs_ref[g]
            uv_sc[rs, cs] = (u[rs, cs] * vm).astype(BF16)
    y = _mm(uv_sc[...], wout_ref[...])
    zz = DN_ALPHA * x + m_ref[0, 2:3, :] * y
    y_ref[...] = _layer_norm(zz, g_ref[...], b_ref[...])


def _gmlp(x, mods_l, w_in, b_in, vg, vb, w_s, b_s, w_out, ln_g, ln_b, *, t_p, s_len):
    t, d = x.shape
    tm = TM_GMLP
    n_p = t_p // tm
    tpb = s_len // tm
    row = lambda i: (i, 0)
    fixed = lambda i: (0, 0)
    fixed3 = lambda i: (0, 0, 0)
    b_s_b = jnp.broadcast_to(b_s[:, :, None], (G_GROUPS, G_CHUNK, LANES))
    return pl.pallas_call(
        _gmlp_kernel,
        grid=(t // tm,),
        in_specs=[
            pl.BlockSpec((tm, d), row),
            pl.BlockSpec((1, 6, d), lambda i: (_tile_group(i, n_p, tpb), 0, 0)),
            pl.BlockSpec((d, 2 * G_DIM), fixed),
            pl.BlockSpec((1, 2 * G_DIM), fixed),
            pl.BlockSpec((1, G_DIM), fixed),
            pl.BlockSpec((1, G_DIM), fixed),
            pl.BlockSpec((G_GROUPS, G_CHUNK, G_CHUNK), fixed3),
            pl.BlockSpec((G_GROUPS, G_CHUNK, LANES), fixed3),
            pl.BlockSpec((G_DIM, d), fixed),
            pl.BlockSpec((1, d), fixed),
            pl.BlockSpec((1, d), fixed),
        ],
        out_specs=pl.BlockSpec((tm, d), row),
        out_shape=jax.ShapeDtypeStruct((t, d), F32),
        scratch_shapes=[pltpu.VMEM((tm, G_DIM), BF16)],
        compiler_params=_params(("parallel",)),
    )(x, mods_l, w_in.astype(BF16), b_in.reshape(1, -1), vg.reshape(1, -1), vb.reshape(1, -1),
      w_s.astype(BF16), b_s_b, w_out.astype(BF16), ln_g.reshape(1, d), ln_b.reshape(1, d))


def _top16(sc):
    n, L = sc.shape
    pos = lax.broadcasted_iota(jnp.int32, (n, L), 0).astype(F32)
    slot = lax.broadcasted_iota(jnp.int32, (P_TOPK, L), 0)

    def body(k, carry):
        cur, rank, vals = carry
        m = jnp.max(cur, axis=0, keepdims=True)
        idx = jnp.min(jnp.where(cur == m, pos, float(n)), axis=0, keepdims=True)
        hit = pos == idx
        cur = jnp.where(hit, NEG_INF, cur)
        rank = jnp.where(hit, lax.convert_element_type(k, F32), rank)
        vals = jnp.where(slot == k, m, vals)
        return cur, rank, vals

    init = (sc, jnp.full((n, L), 64.0, F32), jnp.zeros((P_TOPK, L), F32))
    _, rank, vals = lax.fori_loop(0, P_TOPK, body, init)
    return vals, rank


_CAND_BLOCKS = ((0, 0, 8), (0, 8, 8), (1, 0, 8), (2, 0, 5), (3, 0, 4), (4, 0, 3), (5, 0, 2), (6, 0, 2), (7, 0, 2))


def _pair_select(v1, v2):
    L = v1.shape[1]
    r8 = lax.broadcasted_iota(jnp.int32, (8, L), 0).astype(F32)
    blocks = []
    codes = []
    for a, b0, nb in _CAND_BLOCKS:
        c = v1[a:a + 1] + v2[b0:b0 + 8]
        if nb < 8:
            c = jnp.where(r8 < float(nb), c, NEG_INF)
        blocks.append(c)
        codes.append(r8 + float(a * P_TOPK + b0))
    blocks.append(v1[8:16] + v2[0:1])
    codes.append((r8 + 8.0) * float(P_TOPK))
    cand0 = jnp.concatenate(blocks, axis=0)
    code = jnp.concatenate(codes, axis=0)

    def body(k, carry):
        cur, sel = carry
        m = jnp.max(cur, axis=0, keepdims=True)
        idx = jnp.min(jnp.where(cur == m, code, 1e9), axis=0, keepdims=True)
        hit = code == idx
        return jnp.where(hit, NEG_INF, cur), jnp.where(hit, 1.0, sel)

    _, sel = lax.fori_loop(0, P_TOPK, body, (cand0, jnp.zeros_like(cand0)))
    top = v1[0:1] + v2[0:1]
    z = jnp.sum(jnp.where(sel > 0.0, jnp.exp(cand0 - top), 0.0), axis=0, keepdims=True)
    cnt_lo = jnp.zeros((8, L), F32)
    starts = (0, 16, 24, 32, 40, 48, 56, 64, 72)
    for a in range(8):
        n_a = jnp.sum(sel[starts[a]:starts[a + 1]], axis=0, keepdims=True)
        cnt_lo = jnp.where(r8 == float(a), n_a, cnt_lo)
    count = jnp.concatenate([cnt_lo, sel[72:80]], axis=0)
    return count, z


def _retrieve_exact(s1, s2):
    v1, r1 = _top16(s1)
    v2, r2 = _top16(s2)
    count, z = _pair_select(v1, v2)
    npos = jnp.zeros_like(r1)
    for a in range(P_TOPK):
        npos = jnp.where(r1 == float(a), count[a:a + 1], npos)
    return npos, jnp.exp(s1 - v1[0:1]) / z, r2, jnp.exp(s2 - v2[0:1])


def _oddeven_merge_sort_pairs(n):
    pairs = []
    p = 1
    while p < n:
        k = p
        while k >= 1:
            for j in range(k % p, n - k, 2 * k):
                for i in range(min(k, n - j - k)):
                    if (i + j) // (2 * p) == (i + j + k) // (2 * p):
                        pairs.append((i + j, i + j + k))
            k //= 2
        p *= 2
    return tuple(pairs)


_SORT16 = _oddeven_merge_sort_pairs(P_TOPK)


def _sort16_desc(v):
    for i, j in _SORT16:
        v[i], v[j] = jnp.maximum(v[i], v[j]), jnp.minimum(v[i], v[j])


def _bitonic_merge16_desc(v):
    dist = P_TOPK // 2
    while dist >= 1:
        for i in range(P_TOPK):
            if not i & dist:
                v[i], v[i + dist] = jnp.maximum(v[i], v[i + dist]), jnp.minimum(v[i], v[i + dist])
        dist //= 2


def _merge_top16(a, b):
    v = [jnp.maximum(a[i], b[P_TOPK - 1 - i]) for i in range(P_TOPK)]
    _bitonic_merge16_desc(v)
    return v


def _sorted_top16(s):
    v = [s[8 * k:8 * k + 8, :] for k in range(N_KEYS // 8)]
    _sort16_desc(v)
    for shift in (4, 2, 1):
        v = [jnp.maximum(v[i], pltpu.roll(v[P_TOPK - 1 - i], shift, 0)) for i in range(P_TOPK)]
        _bitonic_merge16_desc(v)
    return v


_ROW_LEN = tuple(P_TOPK // (a + 1) for a in range(P_TOPK))


def _pair_counts(p1, p2):
    rows = [[p1[a] + p2[b] for b in range(_ROW_LEN[a])] for a in range(P_TOPK)]
    ninf = jnp.full_like(p1[0], NEG_INF)
    g0 = list(rows[0])
    g1 = rows[1] + [rows[a][0] for a in range(P_TOPK - 1, 7, -1)]
    _bitonic_merge16_desc(g1)
    g2 = rows[2] + rows[3] + rows[4] + rows[5] + rows[6]
    _sort16_desc(g2)
    g3 = rows[7] + [ninf] * (P_TOPK - len(rows[7]))
    tau = _merge_top16(_merge_top16(g0, g1), _merge_top16(g2, g3))[P_TOPK - 1]
    top = rows[0][0]
    counts = []
    z = jnp.zeros_like(top)
    total = jnp.zeros_like(top)
    for a in range(P_TOPK):
        n = jnp.zeros_like(top)
        for cand in rows[a]:
            ge = cand >= tau
            n = n + jnp.where(ge, 1.0, 0.0)
            z = z + jnp.where(ge, jnp.exp(cand - top), 0.0)
        counts.append(n)
        total = total + n
    return counts, z, jnp.where(total != float(P_TOPK), 1.0, 0.0)


def _retrieve_tie_free(s1_chunks, s2_chunks):
    n_chunks = len(s1_chunks)
    assert n_chunks <= 8
    sub = lax.broadcasted_iota(jnp.int32, (8, LANES), 0)
    blocks = [slice(8 * k, 8 * k + 8) for k in range(N_KEYS // 8)]
    add = lambda p, q: p + q
    rank_sum_distinct = float(sum(range(P_TOPK)) + P_TOPK * (N_KEYS - P_TOPK))

    tops = []
    rank2 = []
    tie = jnp.zeros((8, LANES), F32)
    p1 = p2 = None
    for c in range(n_chunks):
        s1, s2 = s1_chunks[c], s2_chunks[c]
        v1 = _sorted_top16(s1)
        v2 = _sorted_top16(s2)
        tops.append((v1, v2))
        ranks = []
        for blk in blocks:
            r = jnp.zeros((8, LANES), F32)
            for a in range(P_TOPK):
                r = jnp.where(v2[a] > s2[blk], float(a + 1), r)
            ranks.append(r)
        rank2.append(jnp.concatenate(ranks, axis=0))
        rank_sum = jnp.sum(functools.reduce(add, ranks), axis=0, keepdims=True)
        tie = jnp.where(rank_sum != rank_sum_distinct, 1.0, tie)
        for a in range(P_TOPK - 1):
            tie = jnp.where(v1[a] == v1[a + 1], 1.0, tie)
        n_ge = functools.reduce(add, [jnp.where(s1[blk] >= v1[P_TOPK - 1], 1.0, 0.0) for blk in blocks])
        tie = jnp.where(jnp.sum(n_ge, axis=0, keepdims=True) != float(P_TOPK), 1.0, tie)
        if c == 0:
            p1, p2 = list(v1), list(v2)
        else:
            p1 = [jnp.where(sub == c, v, p) for v, p in zip(v1, p1)]
            p2 = [jnp.where(sub == c, v, p) for v, p in zip(v2, p2)]

    counts, z, pair_tie = _pair_counts(p1, p2)
    tie = jnp.where((sub < n_chunks) & (pair_tie != 0.0), 1.0, tie)

    tables = []
    for c in range(n_chunks):
        s1, s2 = s1_chunks[c], s2_chunks[c]
        v1, v2 = tops[c]
        cnt = [jnp.broadcast_to(n[c:c + 1, :], (8, LANES)) for n in counts] + [jnp.zeros((8, LANES), F32)]
        npos = []
        for blk in blocks:
            n = cnt[0]
            for a in range(P_TOPK):
                n = jnp.where(v1[a] > s1[blk], cnt[a + 1], n)
            npos.append(n)
        tables.append((jnp.concatenate(npos, axis=0), jnp.exp(s1 - v1[0][0:1]) / z[c:c + 1, :],
                       rank2[c], jnp.exp(s2 - v2[0][0:1])))
    return tables, tie


def _dup_bf16_word(x):
    u = lax.bitcast_convert_type(x.astype(BF16).astype(F32), jnp.uint32)
    return lax.bitcast_convert_type(u | (u >> 16), jnp.int32)


def _peer_topk_kernel(x_ref, m_ref, wh_ref, wl_ref, kh_ref, kl_ref,
                      h2t_ref, ni_ref, ai_ref, rj_ref, bj_ref, qt_sc):
    hd = pl.program_id(1)

    @pl.when(hd == 0)
    def _():
        x = x_ref[...]
        h2 = x * (1.0 + m_ref[0, 4:5, :]) + m_ref[0, 3:4, :]
        h_hi, h_lo = _split(h2.T)
        h2t_ref[...] = h_hi
        qt_sc[...] = _mm3(wh_ref[...], wl_ref[...], h_hi, h_lo)

    qh = qt_sc[pl.ds(pl.multiple_of(hd * (2 * P_HALF), 2 * P_HALF), 2 * P_HALF), :]
    q1_hi, q1_lo = _split(qh[:P_HALF])
    q2_hi, q2_lo = _split(qh[P_HALF:])
    sc1 = _mm3(kh_ref[0], kl_ref[0], q1_hi, q1_lo)
    sc2 = _mm3(kh_ref[1], kl_ref[1], q2_hi, q2_lo)
    n_chunks = sc1.shape[1] // LANES
    chunk = [slice(c * LANES, (c + 1) * LANES) for c in range(n_chunks)]

    def emit(c, tables):
        npos, a_i, r2, b_j = tables
        ni_ref[0, :, chunk[c]] = _dup_bf16_word(npos)
        ai_ref[0, :, chunk[c]] = _dup_bf16_word(0.5 * a_i)
        rj_ref[0, :, chunk[c]] = r2.astype(BF16)
        bj_ref[0, :, chunk[c]] = b_j.astype(BF16)

    tables, tie = _retrieve_tie_free([sc1[:, ls] for ls in chunk], [sc2[:, ls] for ls in chunk])
    for c in range(n_chunks):
        emit(c, tables[c])

    @pl.when(jnp.max(tie) > 0.0)
    def _():
        for c in range(n_chunks):
            emit(c, _retrieve_exact(sc1[:, chunk[c]], sc2[:, chunk[c]]))


def _peer_topk(x, mods_l, wq_t_hi, wq_t_lo, keys_hi, keys_lo, *, t_p, s_len):
    t, d = x.shape
    tm = TM_TOPK
    n_p = t_p // tm
    tpb = s_len // tm
    row_tab = jax.ShapeDtypeStruct((P_HEADS, N_KEYS, t), jnp.int32)
    row_spec = pl.BlockSpec((1, N_KEYS, tm), lambda i, h: (h, 0, i))
    col_tab = jax.ShapeDtypeStruct((P_HEADS, N_KEYS, t), BF16)
    col_spec = row_spec
    return pl.pallas_call(
        _peer_topk_kernel,
        grid=(t // tm, P_HEADS),
        in_specs=[
            pl.BlockSpec((tm, d), lambda i, h: (i, 0)),
            pl.BlockSpec((1, 6, d), lambda i, h: (_tile_group(i, n_p, tpb), 0, 0)),
            pl.BlockSpec((d, d), lambda i, h: (0, 0)),
            pl.BlockSpec((d, d), lambda i, h: (0, 0)),
            pl.BlockSpec((2, N_KEYS, P_HALF), lambda i, h: (h, 0, 0)),
            pl.BlockSpec((2, N_KEYS, P_HALF), lambda i, h: (h, 0, 0)),
        ],
        out_specs=[pl.BlockSpec((d, tm), lambda i, h: (0, i)), row_spec, row_spec, col_spec, col_spec],
        out_shape=[jax.ShapeDtypeStruct((d, t), BF16), row_tab, row_tab, col_tab, col_tab],
        scratch_shapes=[pltpu.VMEM((d, tm), F32)],
        compiler_params=_params(("parallel", "arbitrary")),
    )(x, mods_l, wq_t_hi, wq_t_lo, keys_hi, keys_lo)


def _row_tile(row):
    tile = pltpu.bitcast(jnp.broadcast_to(row, (8, row.shape[1])), BF16)
    return jnp.concatenate([tile] * (N_KEYS // tile.shape[0]), axis=0)


GELU_C1 = math.sqrt(2.0 / math.pi)
GELU_C2 = 0.044715 * GELU_C1
K_CHUNK = 512


def _peer_dense_kernel(h2t_ref, eu0_ref, eun_ref, evt_ref, ni_ref, ai_ref, rj_ref, bj_ref,
                       x_ref, m_ref, g_ref, b_ref, y_ref, acc_sc, act0_sc, act1_sc, wt_sc, *, n_e):
    e = pl.program_id(1)

    @pl.when(e == 0)
    def _():
        acc_sc[...] = jnp.zeros_like(acc_sc)
        act0_sc[...] = _mm(eu0_ref[...], h2t_ref[...]).astype(BF16)

    te, tm = wt_sc.shape
    zero = jnp.zeros((), BF16)

    def after(words, product):
        bits = lax.bitcast_convert_type(product[0:1, :], jnp.int32)
        return words + lax.shift_right_logical(lax.shift_right_logical(bits, 16), 16)

    def step(act_ref, next_ref):
        part = None
        nxt_prev = None
        for kc in range(te // K_CHUNK):
            rows = slice(kc * K_CHUNK, (kc + 1) * K_CHUNK)
            nxt = _mm(eun_ref[rows, :], h2t_ref[...])
            next_ref[rows, :] = nxt.astype(BF16)
            for il in range(kc * K_CHUNK // N_KEYS, (kc + 1) * K_CHUNK // N_KEYS):
                rs = slice(il * N_KEYS, (il + 1) * N_KEYS)
                gate = None
                for h in range(P_HEADS):
                    cnt_words = ni_ref[h, il:il + 1, :]
                    if nxt_prev is not None and h == 0 and il * N_KEYS == kc * K_CHUNK:
                        cnt_words = after(cnt_words, nxt_prev)
                    cnt = _row_tile(cnt_words)
                    half_a = _row_tile(ai_ref[h, il:il + 1, :])
                    gh = jnp.where(rj_ref[h] < cnt, bj_ref[h], zero) * half_a
                    gate = gh if gate is None else gate + gh
                a = act_ref[rs, :]
                t = jnp.tanh(a * (GELU_C1 + GELU_C2 * (a * a)))
                wt_sc[rs, :] = gate * (a + a * t)
            p = _mm(evt_ref[:, rows], wt_sc[rows, :])
            part = p if part is None else part + p
            nxt_prev = nxt
        acc_sc[...] += part

    @pl.when(e % 2 == 0)
    def _():
        step(act0_sc, act1_sc)

    @pl.when(e % 2 == 1)
    def _():
        step(act1_sc, act0_sc)

    @pl.when(e == n_e - 1)
    def _():
        f = acc_sc[...].T
        z = DN_ALPHA * x_ref[...] + m_ref[0, 5:6, :] * f
        y_ref[...] = _layer_norm(z, g_ref[...], b_ref[...])


def _peer_dense(h2t, e_u, e_vt, ni, ai, rj, bj, x, mods_l, ln_g, ln_b, *, t_p, s_len):
    t, d = x.shape
    tm = TM_PEER
    te = TE_PEER
    n_p = t_p // tm
    tpb = s_len // tm
    n_exp = e_u.shape[0]
    n_e = n_exp // te
    n_i = te // N_KEYS
    return pl.pallas_call(
        functools.partial(_peer_dense_kernel, n_e=n_e),
        grid=(t // tm, n_e),
        in_specs=[
            pl.BlockSpec((d, tm), lambda i, e: (0, i)),
            pl.BlockSpec((te, d), lambda i, e: (0, 0)),
            pl.BlockSpec((te, d), lambda i, e: (jnp.minimum(e + 1, n_e - 1), 0)),
            pl.BlockSpec((d, te), lambda i, e: (0, e)),
            pl.BlockSpec((P_HEADS, n_i, tm), lambda i, e: (0, e, i)),
            pl.BlockSpec((P_HEADS, n_i, tm), lambda i, e: (0, e, i)),
            pl.BlockSpec((P_HEADS, N_KEYS, tm), lambda i, e: (0, 0, i)),
            pl.BlockSpec((P_HEADS, N_KEYS, tm), lambda i, e: (0, 0, i)),
            pl.BlockSpec((tm, d), lambda i, e: (i, 0)),
            pl.BlockSpec((1, 6, d), lambda i, e: (_tile_group(i, n_p, tpb), 0, 0)),
            pl.BlockSpec((1, d), lambda i, e: (0, 0)),
            pl.BlockSpec((1, d), lambda i, e: (0, 0)),
        ],
        out_specs=pl.BlockSpec((tm, d), lambda i, e: (i, 0)),
        out_shape=jax.ShapeDtypeStruct((t, d), F32),
        scratch_shapes=[pltpu.VMEM((d, tm), F32), pltpu.VMEM((te, tm), BF16), pltpu.VMEM((te, tm), BF16),
                        pltpu.VMEM((te, tm), BF16)],
        compiler_params=_params(("parallel", "arbitrary")),
    )(h2t, e_u, e_u, e_vt, ni, ai, rj, bj, x, mods_l, ln_g.reshape(1, d), ln_b.reshape(1, d))


def _rope_tables(n_tok):
    rows = n_tok // GRID_W
    row = jnp.repeat(jnp.arange(rows), GRID_W).astype(F32)
    col = jnp.tile(jnp.arange(GRID_W), rows).astype(F32)
    n_freq = A_HD // 4
    inv = ROPE_THETA ** (-jnp.arange(n_freq, dtype=F32) / n_freq)
    ang = jnp.concatenate([row[:, None] * inv, col[:, None] * inv], -1)
    cos = jnp.repeat(jnp.cos(ang), 2, axis=-1)
    sin = jnp.repeat(jnp.sin(ang), 2, axis=-1)
    sign = jnp.tile(jnp.asarray([-1.0, 1.0], F32), A_HD // 2)
    reps = LANES // A_HD
    return jnp.tile(cos, (1, reps)), jnp.tile(sin * sign, (1, reps))


def _gqa_weights(w_qkv, q_norm, k_norm, w_o):
    d = w_qkv.shape[0]
    nq = A_HEADS * A_HD
    half = (np.arange(A_HEADS) // A_REP) % 2
    sel = np.stack([half == 0, half == 1], axis=1).astype(np.float32)
    wq = w_qkv[:, :nq].reshape(d, A_HEADS, 1, A_HD) * sel[None, :, :, None]
    w_exp = jnp.concatenate([wq.reshape(d, A_HEADS * LANES), w_qkv[:, nq:]], axis=1).astype(BF16)
    gq = jnp.tile(q_norm, 2 * A_HEADS).reshape(1, -1)
    gk = jnp.tile(k_norm, A_KV).reshape(1, -1)
    wo = w_o.reshape(A_HEADS, 1, A_HD, d) * sel[:, :, None, None]
    return w_exp, gq, gk, wo.reshape(A_HEADS * LANES, d).astype(BF16)


def _diff_weights(w_qkv):
    d = w_qkv.shape[0]
    eye = np.eye(2, dtype=np.float32)
    wq = w_qkv[:, :D_W].reshape(d, D_HEADS, 1, 2, D_HD) * eye[None, None, :, :, None]
    return jnp.concatenate([wq.reshape(d, D_HEADS * 2 * LANES), w_qkv[:, D_W:]], axis=1).astype(BF16)


def _split_f32(w):
    hi = w.astype(BF16)
    return hi, (w - hi.astype(F32)).astype(BF16)


def kernel(x_prompt, x_sample, cache_a_k, cache_a_v, cache_d_k, cache_d_v, c, c_ctx, w_mod, b_mod, ln_g, ln_b, a_w_qkv, a_q_norm, a_k_norm, a_w_o, d_w_qkv, d_lambda_q1, d_lambda_k1, d_lambda_q2, d_lambda_k2, d_sub_norm, d_w_o, g_w_in, g_b_in, g_v_norm_g, g_v_norm_b, g_w_s, g_b_s, g_w_out, p_w_q, p_sub_keys, p_expert_u, p_expert_v):
    batch, seq, d = x_prompt.shape
    dec_batch, dec_seq, _ = x_sample.shape
    past = cache_a_k.shape[2]
    t_p = batch * seq
    t_s = dec_batch * dec_seq
    dims = dict(t_p=t_p, s_len=dec_seq)

    x = jnp.concatenate([x_prompt.reshape(t_p, d), x_sample.reshape(t_s, d)], axis=0)
    n_cond = -(-(1 + dec_batch) // 8) * 8
    cond = jnp.zeros((n_cond, d), F32).at[0].set(c_ctx).at[1:1 + dec_batch].set(c)
    mods = _modulation(cond, w_mod, b_mod).reshape(DEPTH, n_cond, 6, d)
    rope_c, rope_s = _rope_tables(dec_seq)

    ak, av, dk, dv = [], [], [], []
    for i in range(DEPTH):
        kind, j = i % N_MIXERS, i // N_MIXERS
        mods_l = mods[i]
        if kind == 0:
            w_exp, gq, gk, wo = _gqa_weights(a_w_qkv[j], a_q_norm[j], a_k_norm[j], a_w_o[j])
            q, k_att, v_att, k_f, v_f = _qkv_project(
                x, mods_l, w_exp, gq, gk, rope_c, rope_s, nq=A_HEADS, nk=A_KV * A_HD // LANES,
                nv=A_KV * A_HD // LANES, use_norm=True, q_scale=A_HD ** -0.5, **dims)
            ctx_k = cache_a_k[:, j].reshape(dec_batch, past, A_KV * A_HD).astype(BF16)
            ctx_v = cache_a_v[:, j].reshape(dec_batch, past, A_KV * A_HD).astype(BF16)
            o = _attention(_gqa_attn_kernel, (), q, k_att, v_att, ctx_k, ctx_v, n_out=A_HEADS * LANES,
                           t_p=t_p, seq_p=seq, batch_s=dec_batch, seq_s=dec_seq)
            ak.append(k_f[:t_p].reshape(batch, seq, A_KV, A_HD))
            av.append(v_f[:t_p].reshape(batch, seq, A_KV, A_HD))
        elif kind == 1:
            lam_init = 0.8 - 0.6 * math.exp(-0.3 * i)
            w_exp = _diff_weights(d_w_qkv[j])
            q, k_att, v_att, k_f, v_f = _qkv_project(
                x, mods_l, w_exp, jnp.ones((1, 2 * D_W), F32), jnp.ones((1, D_W), F32),
                rope_c, rope_s, nq=2 * D_HEADS, nk=D_HEADS, nv=D_HEADS,
                use_norm=False, q_scale=D_HD ** -0.5, **dims)
            ctx_k = cache_d_k[:, j].reshape(dec_batch, past, D_W).astype(BF16)
            ctx_v = cache_d_v[:, j].reshape(dec_batch, past, D_HEADS * D_VD).astype(BF16)
            lvec = jnp.stack([d_lambda_q1[j], d_lambda_k1[j], d_lambda_q2[j], d_lambda_k2[j]])
            kern = functools.partial(_diff_attn_kernel, lam_init=lam_init)
            o = _attention(kern, (lvec, d_sub_norm[j].reshape(1, D_VD)), q, k_att, v_att, ctx_k, ctx_v,
                           n_out=D_HEADS * D_VD, t_p=t_p, seq_p=seq, batch_s=dec_batch, seq_s=dec_seq)
            wo = d_w_o[j].astype(BF16)
            dk.append(k_f[:t_p].reshape(batch, seq, D_HEADS, 2, D_HD))
            dv.append(v_f[:t_p].reshape(batch, seq, D_HEADS, D_VD))
        if kind == 2:
            x = _gmlp(x, mods_l, g_w_in[j], g_b_in[j], g_v_norm_g[j], g_v_norm_b[j], g_w_s[j], g_b_s[j],
                      g_w_out[j], ln_g[i, 0], ln_b[i, 0], **dims)
        else:
            x = _proj_ln(*o, wo, x, mods_l, ln_g[i, 0], ln_b[i, 0], gate_idx=2, **dims)

        wq_hi, wq_lo = _split_f32(p_w_q[i].T)
        keys = p_sub_keys[i].reshape(P_HEADS * 2, N_KEYS, P_HALF)
        k_hi, k_lo = _split_f32(keys)
        h2t, ni, ai, rj, bj = _peer_topk(x, mods_l, wq_hi, wq_lo, k_hi, k_lo, **dims)
        x = _peer_dense(h2t, p_expert_u[i].astype(BF16), p_expert_v[i].T.astype(BF16),
                        ni, ai, rj, bj, x, mods_l, ln_g[i, 1], ln_b[i, 1], **dims)

    y_prompt = x[:t_p].reshape(batch, seq, d)
    y_sample = x[t_p:].reshape(dec_batch, dec_seq, d)
    return (y_prompt, y_sample, jnp.stack(ak, axis=1), jnp.stack(av, axis=1),
            jnp.stack(dk, axis=1), jnp.stack(dv, axis=1))
```

```python
import functools
import math

import numpy as np
import jax
import jax.numpy as jnp
from jax import lax
from jax.experimental import pallas as pl
from jax.experimental.pallas import tpu as pltpu

F32 = jnp.float32
BF16 = jnp.bfloat16

D_MODEL = 1024
DEPTH = 4
GRID_W = 64
N_MIXERS = 3
A_HEADS = 16
A_KV = 4
A_REP = A_HEADS // A_KV
A_HD = 64
D_HEADS = 8
D_HD = 64
D_VD = 2 * D_HD
D_W = D_HEADS * 2 * D_HD
G_DIM = 1024
G_GROUPS = 8
G_CHUNK = 128
P_HEADS = 8
N_KEYS = 128
P_HALF = 64
P_TOPK = 16
ROPE_THETA = 10000.0
DN_ALPHA = (2 * DEPTH) ** 0.25
EPS = 1e-6

LANES = 128
NEG_INF = float("-inf")
VMEM_LIMIT = 56 * 1024 * 1024

TM_QKV = 256
TM_PROJ = 512
TQ_ATTN = 256
TM_GMLP = 256
TM_TOPK = 1024
TM_PEER = 512
TE_PEER = 2048


def _mm(a, b):
    return jnp.dot(a, b, preferred_element_type=F32)


def _mm_nt(a, b):
    return lax.dot_general(a, b, (((1,), (1,)), ((), ())), preferred_element_type=F32)


def _split(x):
    hi = x.astype(BF16)
    lo = (x - hi.astype(F32)).astype(BF16)
    return hi, lo


def _mm3(a_hi, a_lo, b_hi, b_lo):
    return _mm(a_hi, b_hi) + _mm(a_hi, b_lo) + _mm(a_lo, b_hi)


def _layer_norm(z, g, b):
    mu = jnp.mean(z, axis=-1, keepdims=True)
    zc = z - mu
    var = jnp.mean(zc * zc, axis=-1, keepdims=True)
    return zc * lax.rsqrt(var + EPS) * g + b


def _params(sem, flags=None):
    return pltpu.CompilerParams(dimension_semantics=sem, vmem_limit_bytes=VMEM_LIMIT, flags=flags)


def _tile_group(i, n_p_tiles, tiles_per_batch):
    return jnp.where(i < n_p_tiles, 0, 1 + (i - n_p_tiles) // tiles_per_batch)


def _tile_pos_block(i, n_p_tiles, tiles_per_batch):
    return jnp.where(i < n_p_tiles, 0, (i - n_p_tiles) % tiles_per_batch)


def _mods_kernel(c_ref, w_ref, b_ref, o_ref):
    c = c_ref[...]
    a = c * jax.nn.sigmoid(c)
    a_hi, a_lo = _split(a)
    w_hi, w_lo = _split(w_ref[0])
    o_ref[0] = _mm3(a_hi, a_lo, w_hi, w_lo) + b_ref[0]


def _modulation(cond, w_mod, b_mod):
    depth, d, n = w_mod.shape
    rows = cond.shape[0]
    tn = 1536
    return pl.pallas_call(
        _mods_kernel,
        grid=(depth, n // tn),
        in_specs=[
            pl.BlockSpec((rows, d), lambda l, j: (0, 0)),
            pl.BlockSpec((1, d, tn), lambda l, j: (l, 0, j)),
            pl.BlockSpec((1, 1, tn), lambda l, j: (l, 0, j)),
        ],
        out_specs=pl.BlockSpec((1, rows, tn), lambda l, j: (l, 0, j)),
        out_shape=jax.ShapeDtypeStruct((depth, rows, n), F32),
        compiler_params=_params(("parallel", "parallel")),
    )(cond, w_mod, b_mod.reshape(depth, 1, n))


def _swap_pairs(x):
    lane = lax.broadcasted_iota(jnp.int32, x.shape, 1)
    nxt = pltpu.roll(x, LANES - 1, 1)
    prv = pltpu.roll(x, 1, 1)
    return jnp.where((lane & 1) == 0, nxt, prv)


def _qkv_kernel(x_ref, m_ref, w_ref, bd_ref, gq_ref, gk_ref, cs_ref, sn_ref,
                q_ref, ka_ref, va_ref, kf_ref, vf_ref,
                *, nq, nk, nv, use_norm, n_p_tiles, q_scale):
    x = x_ref[...]
    h = x * (1.0 + m_ref[0, 1:2, :]) + m_ref[0, 0:1, :]
    y = _mm(h.astype(BF16), w_ref[...])
    is_lat = pl.program_id(0) >= n_p_tiles
    cs = jnp.where(is_lat, cs_ref[...], 1.0)
    sn = jnp.where(is_lat, sn_ref[...], 0.0)
    bd = bd_ref[...]

    def head_norm(yg, gain):
        s_hi, s_lo = _split(yg * yg)
        ms = _mm(s_hi, bd) + _mm(s_lo, bd)
        return yg * lax.rsqrt(ms + EPS) * gain

    def rope(yg):
        return yg * cs + _swap_pairs(yg) * sn

    for g in range(nq):
        sl = slice(g * LANES, (g + 1) * LANES)
        yg = y[:, sl]
        if use_norm:
            yg = head_norm(yg, gq_ref[:, sl])
        q_ref[:, sl] = (rope(yg) * q_scale).astype(BF16)
    for g in range(nk):
        sl = slice(g * LANES, (g + 1) * LANES)
        yg = y[:, (nq + g) * LANES:(nq + g + 1) * LANES]
        if use_norm:
            yg = head_norm(yg, gk_ref[:, sl])
        kf_ref[:, sl] = yg
        ka_ref[:, sl] = rope(yg).astype(BF16)
    for g in range(nv):
        sl = slice(g * LANES, (g + 1) * LANES)
        yg = y[:, (nq + nk + g) * LANES:(nq + nk + g + 1) * LANES]
        vf_ref[:, sl] = yg
        va_ref[:, sl] = yg.astype(BF16)


def _qkv_project(x, mods_l, w_exp, gq, gk, rope_c, rope_s, *, nq, nk, nv, use_norm, t_p, s_len, q_scale):
    t, d = x.shape
    tm = TM_QKV
    n_p = t_p // tm
    tpb = s_len // tm
    ncol = (nq + nk + nv) * LANES
    bd = np.kron(np.eye(2, dtype=np.float32), np.full((64, 64), 1.0 / 64.0, np.float32))
    bd = jnp.asarray(bd, BF16)
    grp = lambda i: (_tile_group(i, n_p, tpb), 0, 0)
    pos = lambda i: (_tile_pos_block(i, n_p, tpb), 0)
    row = lambda i: (i, 0)
    fixed = lambda i: (0, 0)
    kern = functools.partial(_qkv_kernel, nq=nq, nk=nk, nv=nv, use_norm=use_norm,
                             n_p_tiles=n_p, q_scale=q_scale)
    return pl.pallas_call(
        kern,
        grid=(t // tm,),
        in_specs=[
            pl.BlockSpec((tm, d), row),
            pl.BlockSpec((1, 6, d), grp),
            pl.BlockSpec((d, ncol), fixed),
            pl.BlockSpec((LANES, LANES), fixed),
            pl.BlockSpec((1, nq * LANES), fixed),
            pl.BlockSpec((1, nk * LANES), fixed),
            pl.BlockSpec((tm, LANES), pos),
            pl.BlockSpec((tm, LANES), pos),
        ],
        out_specs=[
            pl.BlockSpec((tm, nq * LANES), row),
            pl.BlockSpec((tm, nk * LANES), row),
            pl.BlockSpec((tm, nv * LANES), row),
            pl.BlockSpec((tm, nk * LANES), row),
            pl.BlockSpec((tm, nv * LANES), row),
        ],
        out_shape=[
            jax.ShapeDtypeStruct((t, nq * LANES), BF16),
            jax.ShapeDtypeStruct((t, nk * LANES), BF16),
            jax.ShapeDtypeStruct((t, nv * LANES), BF16),
            jax.ShapeDtypeStruct((t, nk * LANES), F32),
            jax.ShapeDtypeStruct((t, nv * LANES), F32),
        ],
        compiler_params=_params(("parallel",)),
    )(x, mods_l, w_exp, bd, gq, gk, rope_c, rope_s)


def _softmax_pv(qh, k_all, v_all):
    scores = [_mm_nt(qh, k) for k in k_all]
    m = scores[0].max(axis=-1, keepdims=True)
    for s in scores[1:]:
        m = jnp.maximum(m, s.max(axis=-1, keepdims=True))
    o = None
    l = None
    for s, v in zip(scores, v_all):
        p = jnp.exp(s - m)
        ls = p.sum(axis=-1, keepdims=True)
        os_ = _mm(p.astype(BF16), v)
        o = os_ if o is None else o + os_
        l = ls if l is None else l + ls
    return o, l


def _gqa_attn_kernel(*refs, has_ctx):
    if has_ctx:
        q_ref, k_ref, v_ref, kc_ref, vc_ref, o_ref = refs
    else:
        q_ref, k_ref, v_ref, o_ref = refs
    tq = q_ref.shape[0]
    for g in range(A_KV):
        ks = slice((g // 2) * LANES, (g // 2 + 1) * LANES)
        heads = [slice((g * A_REP + r) * LANES, (g * A_REP + r + 1) * LANES) for r in range(A_REP)]
        k_all = [k_ref[:, ks]]
        v_all = [v_ref[:, ks]]
        if has_ctx:
            k_all.append(kc_ref[0, :, ks])
            v_all.append(vc_ref[0, :, ks])
        o, l = _softmax_pv(jnp.concatenate([q_ref[:, sl] for sl in heads], axis=0), k_all, v_all)
        o = (o / l).astype(BF16)
        for r, sl in enumerate(heads):
            o_ref[:, sl] = o[r * tq:(r + 1) * tq, :]


def _diff_attn_kernel(*refs, has_ctx, lam_init):
    if has_ctx:
        lv_ref, sg_ref, q_ref, k_ref, v_ref, kc_ref, vc_ref, o_ref = refs
    else:
        lv_ref, sg_ref, q_ref, k_ref, v_ref, o_ref = refs
    lv = lv_ref[...]
    lam = (jnp.exp(jnp.sum(lv[0:1] * lv[1:2], axis=-1, keepdims=True))
           - jnp.exp(jnp.sum(lv[2:3] * lv[3:4], axis=-1, keepdims=True)) + lam_init)
    for h in range(D_HEADS):
        ks = slice(h * LANES, (h + 1) * LANES)
        k_all = [k_ref[:, ks]]
        v_all = [v_ref[:, ks]]
        if has_ctx:
            k_all.append(kc_ref[0, :, ks])
            v_all.append(vc_ref[0, :, ks])
        tq = q_ref.shape[0]
        q2 = jnp.concatenate([q_ref[:, (2 * h + j) * LANES:(2 * h + j + 1) * LANES] for j in range(2)], axis=0)
        o, l = _softmax_pv(q2, k_all, v_all)
        o = o / l
        o = o[:tq] - lam * o[tq:]
        ms = jnp.mean(o * o, axis=-1, keepdims=True)
        o = o * lax.rsqrt(ms + EPS) * sg_ref[...] * (1.0 - lam_init)
        o_ref[:, ks] = o.astype(BF16)


def _attention(kern, extra, q, k, v, ctx_k, ctx_v, *, n_out, t_p, seq_p, batch_s, seq_s):
    nqc = q.shape[1]
    nkc = k.shape[1]
    extra_specs = [pl.BlockSpec(a.shape, lambda *_: (0, 0)) for a in extra]
    n_seq_p = t_p // seq_p
    o_p = pl.pallas_call(
        functools.partial(kern, has_ctx=False),
        grid=(n_seq_p,),
        in_specs=extra_specs + [
            pl.BlockSpec((seq_p, nqc), lambda b: (b, 0)),
            pl.BlockSpec((seq_p, nkc), lambda b: (b, 0)),
            pl.BlockSpec((seq_p, nkc), lambda b: (b, 0)),
        ],
        out_specs=pl.BlockSpec((seq_p, n_out), lambda b: (b, 0)),
        out_shape=jax.ShapeDtypeStruct((t_p, n_out), BF16),
        compiler_params=_params(("parallel",)),
    )(*extra, q, k, v)
    tq = TQ_ATTN
    nqt = seq_s // tq
    assert t_p % seq_s == 0 and t_p % tq == 0
    q_off = t_p // tq
    k_off = t_p // seq_s
    past = ctx_k.shape[1]
    o_s = pl.pallas_call(
        functools.partial(kern, has_ctx=True),
        grid=(batch_s, nqt),
        in_specs=extra_specs + [
            pl.BlockSpec((tq, nqc), lambda b, i: (q_off + b * nqt + i, 0)),
            pl.BlockSpec((seq_s, nkc), lambda b, i: (k_off + b, 0)),
            pl.BlockSpec((seq_s, nkc), lambda b, i: (k_off + b, 0)),
            pl.BlockSpec((1, past, nkc), lambda b, i: (b, 0, 0)),
            pl.BlockSpec((1, past, nkc), lambda b, i: (b, 0, 0)),
        ],
        out_specs=pl.BlockSpec((tq, n_out), lambda b, i: (b * nqt + i, 0)),
        out_shape=jax.ShapeDtypeStruct((batch_s * seq_s, n_out), BF16),
        compiler_params=_params(("parallel", "arbitrary")),
    )(*extra, q, k, v, ctx_k, ctx_v)
    return o_p, o_s


def _proj_ln_kernel(op_ref, os_ref, w_ref, x_ref, m_ref, g_ref, b_ref, y_ref, *, gate_idx, n_p_tiles):
    o = jnp.where(pl.program_id(0) < n_p_tiles, op_ref[...], os_ref[...])
    y = _mm(o, w_ref[...])
    z = DN_ALPHA * x_ref[...] + m_ref[0, gate_idx:gate_idx + 1, :] * y
    y_ref[...] = _layer_norm(z, g_ref[...], b_ref[...])


def _proj_ln(o_p, o_s, w, x, mods_l, ln_g, ln_b, *, gate_idx, t_p, s_len):
    t, d = x.shape
    tm = TM_PROJ
    n_p = t_p // tm
    tpb = s_len // tm
    kin = o_p.shape[1]
    row = lambda i: (i, 0)
    fixed = lambda i: (0, 0)
    return pl.pallas_call(
        functools.partial(_proj_ln_kernel, gate_idx=gate_idx, n_p_tiles=n_p),
        grid=(t // tm,),
        in_specs=[
            pl.BlockSpec((tm, kin), lambda i: (jnp.minimum(i, n_p - 1), 0)),
            pl.BlockSpec((tm, kin), lambda i: (jnp.maximum(i - n_p, 0), 0)),
            pl.BlockSpec((kin, d), fixed),
            pl.BlockSpec((tm, d), row),
            pl.BlockSpec((1, 6, d), lambda i: (_tile_group(i, n_p, tpb), 0, 0)),
            pl.BlockSpec((1, d), fixed),
            pl.BlockSpec((1, d), fixed),
        ],
        out_specs=pl.BlockSpec((tm, d), row),
        out_shape=jax.ShapeDtypeStruct((t, d), F32),
        compiler_params=_params(("parallel",)),
    )(o_p, o_s, w, x, mods_l, ln_g.reshape(1, d), ln_b.reshape(1, d))


def _gmlp_kernel(x_ref, m_ref, win_ref, bin_ref, vg_ref, vb_ref, ws_ref, bs_ref, wout_ref,
                 g_ref, b_ref, y_ref, uv_sc):
    x = x_ref[...]
    h = x * (1.0 + m_ref[0, 1:2, :]) + m_ref[0, 0:1, :]
    z = jax.nn.gelu(_mm(h.astype(BF16), win_ref[...]) + bin_ref[...])
    u = z[:, :G_DIM]
    v = _layer_norm(z[:, G_DIM:], vg_ref[...], vb_ref[...])
    tm = x.shape[0]
    for c in range(tm // G_CHUNK):
        rs = slice(c * G_CHUNK, (c + 1) * G_CHUNK)
        for g in range(G_GROUPS):
            cs = slice(g * LANES, (g + 1) * LANES)
            vm = _mm(ws_ref[g], v[rs, cs].astype(BF16)) + bs_ref[g]
            uv_sc[rs, cs] = (u[rs, cs] * vm).astype(BF16)
    y = _mm(uv_sc[...], wout_ref[...])
    zz = DN_ALPHA * x + m_ref[0, 2:3, :] * y
    y_ref[...] = _layer_norm(zz, g_ref[...], b_ref[...])


def _gmlp(x, mods_l, w_in, b_in, vg, vb, w_s, b_s, w_out, ln_g, ln_b, *, t_p, s_len):
    t, d = x.shape
    tm = TM_GMLP
    n_p = t_p // tm
    tpb = s_len // tm
    row = lambda i: (i, 0)
    fixed = lambda i: (0, 0)
    fixed3 = lambda i: (0, 0, 0)
    b_s_b = jnp.broadcast_to(b_s[:, :, None], (G_GROUPS, G_CHUNK, LANES))
    return pl.pallas_call(
        _gmlp_kernel,
        grid=(t // tm,),
        in_specs=[
            pl.BlockSpec((tm, d), row),
            pl.BlockSpec((1, 6, d), lambda i: (_tile_group(i, n_p, tpb), 0, 0)),
            pl.BlockSpec((d, 2 * G_DIM), fixed),
            pl.BlockSpec((1, 2 * G_DIM), fixed),
            pl.BlockSpec((1, G_DIM), fixed),
            pl.BlockSpec((1, G_DIM), fixed),
            pl.BlockSpec((G_GROUPS, G_CHUNK, G_CHUNK), fixed3),
            pl.BlockSpec((G_GROUPS, G_CHUNK, LANES), fixed3),
            pl.BlockSpec((G_DIM, d), fixed),
            pl.BlockSpec((1, d), fixed),
            pl.BlockSpec((1, d), fixed),
        ],
        out_specs=pl.BlockSpec((tm, d), row),
        out_shape=jax.ShapeDtypeStruct((t, d), F32),
        scratch_shapes=[pltpu.VMEM((tm, G_DIM), BF16)],
        compiler_params=_params(("parallel",)),
    )(x, mods_l, w_in.astype(BF16), b_in.reshape(1, -1), vg.reshape(1, -1), vb.reshape(1, -1),
      w_s.astype(BF16), b_s_b, w_out.astype(BF16), ln_g.reshape(1, d), ln_b.reshape(1, d))


def _top16(sc):
    n, L = sc.shape
    pos = lax.broadcasted_iota(jnp.int32, (n, L), 0).astype(F32)
    slot = lax.broadcasted_iota(jnp.int32, (P_TOPK, L), 0)

    def body(k, carry):
        cur, rank, vals = carry
        m = jnp.max(cur, axis=0, keepdims=True)
        idx = jnp.min(jnp.where(cur == m, pos, float(n)), axis=0, keepdims=True)
        hit = pos == idx
        cur = jnp.where(hit, NEG_INF, cur)
        rank = jnp.where(hit, lax.convert_element_type(k, F32), rank)
        vals = jnp.where(slot == k, m, vals)
        return cur, rank, vals

    init = (sc, jnp.full((n, L), 64.0, F32), jnp.zeros((P_TOPK, L), F32))
    _, rank, vals = lax.fori_loop(0, P_TOPK, body, init)
    return vals, rank


_CAND_BLOCKS = ((0, 0, 8), (0, 8, 8), (1, 0, 8), (2, 0, 5), (3, 0, 4), (4, 0, 3), (5, 0, 2), (6, 0, 2), (7, 0, 2))


def _pair_select(v1, v2):
    L = v1.shape[1]
    r8 = lax.broadcasted_iota(jnp.int32, (8, L), 0).astype(F32)
    blocks = []
    codes = []
    for a, b0, nb in _CAND_BLOCKS:
        c = v1[a:a + 1] + v2[b0:b0 + 8]
        if nb < 8:
            c = jnp.where(r8 < float(nb), c, NEG_INF)
        blocks.append(c)
        codes.append(r8 + float(a * P_TOPK + b0))
    blocks.append(v1[8:16] + v2[0:1])
    codes.append((r8 + 8.0) * float(P_TOPK))
    cand0 = jnp.concatenate(blocks, axis=0)
    code = jnp.concatenate(codes, axis=0)

    def body(k, carry):
        cur, sel = carry
        m = jnp.max(cur, axis=0, keepdims=True)
        idx = jnp.min(jnp.where(cur == m, code, 1e9), axis=0, keepdims=True)
        hit = code == idx
        return jnp.where(hit, NEG_INF, cur), jnp.where(hit, 1.0, sel)

    _, sel = lax.fori_loop(0, P_TOPK, body, (cand0, jnp.zeros_like(cand0)))
    top = v1[0:1] + v2[0:1]
    z = jnp.sum(jnp.where(sel > 0.0, jnp.exp(cand0 - top), 0.0), axis=0, keepdims=True)
    cnt_lo = jnp.zeros((8, L), F32)
    starts = (0, 16, 24, 32, 40, 48, 56, 64, 72)
    for a in range(8):
        n_a = jnp.sum(sel[starts[a]:starts[a + 1]], axis=0, keepdims=True)
        cnt_lo = jnp.where(r8 == float(a), n_a, cnt_lo)
    count = jnp.concatenate([cnt_lo, sel[72:80]], axis=0)
    return count, z


def _retrieve_exact(s1, s2):
    v1, r1 = _top16(s1)
    v2, r2 = _top16(s2)
    count, z = _pair_select(v1, v2)
    npos = jnp.zeros_like(r1)
    for a in range(P_TOPK):
        npos = jnp.where(r1 == float(a), count[a:a + 1], npos)
    return npos, jnp.exp(s1 - v1[0:1]) / z, r2, jnp.exp(s2 - v2[0:1])


def _oddeven_merge_sort_pairs(n):
    pairs = []
    p = 1
    while p < n:
        k = p
        while k >= 1:
            for j in range(k % p, n - k, 2 * k):
                for i in range(min(k, n - j - k)):
                    if (i + j) // (2 * p) == (i + j + k) // (2 * p):
                        pairs.append((i + j, i + j + k))
            k //= 2
        p *= 2
    return tuple(pairs)


_SORT16 = _oddeven_merge_sort_pairs(P_TOPK)


def _sort16_desc(v):
    for i, j in _SORT16:
        v[i], v[j] = jnp.maximum(v[i], v[j]), jnp.minimum(v[i], v[j])


def _bitonic_merge16_desc(v):
    dist = P_TOPK // 2
    while dist >= 1:
        for i in range(P_TOPK):
            if not i & dist:
                v[i], v[i + dist] = jnp.maximum(v[i], v[i + dist]), jnp.minimum(v[i], v[i + dist])
        dist //= 2


def _merge_top16(a, b):
    v = [jnp.maximum(a[i], b[P_TOPK - 1 - i]) for i in range(P_TOPK)]
    _bitonic_merge16_desc(v)
    return v


def _sorted_top16(s):
    v = [s[8 * k:8 * k + 8, :] for k in range(N_KEYS // 8)]
    _sort16_desc(v)
    for shift in (4, 2, 1):
        v = [jnp.maximum(v[i], pltpu.roll(v[P_TOPK - 1 - i], shift, 0)) for i in range(P_TOPK)]
        _bitonic_merge16_desc(v)
    return v


_ROW_LEN = tuple(P_TOPK // (a + 1) for a in range(P_TOPK))


def _pair_counts(p1, p2):
    rows = [[p1[a] + p2[b] for b in range(_ROW_LEN[a])] for a in range(P_TOPK)]
    ninf = jnp.full_like(p1[0], NEG_INF)
    g0 = list(rows[0])
    g1 = rows[1] + [rows[a][0] for a in range(P_TOPK - 1, 7, -1)]
    _bitonic_merge16_desc(g1)
    g2 = rows[2] + rows[3] + rows[4] + rows[5] + rows[6]
    _sort16_desc(g2)
    g3 = rows[7] + [ninf] * (P_TOPK - len(rows[7]))
    tau = _merge_top16(_merge_top16(g0, g1), _merge_top16(g2, g3))[P_TOPK - 1]
    top = rows[0][0]
    counts = []
    z = jnp.zeros_like(top)
    total = jnp.zeros_like(top)
    for a in range(P_TOPK):
        n = jnp.zeros_like(top)
        for cand in rows[a]:
            ge = cand >= tau
            n = n + jnp.where(ge, 1.0, 0.0)
            z = z + jnp.where(ge, jnp.exp(cand - top), 0.0)
        counts.append(n)
        total = total + n
    return counts, z, jnp.where(total != float(P_TOPK), 1.0, 0.0)


def _retrieve_tie_free(s1_chunks, s2_chunks):
    n_chunks = len(s1_chunks)
    assert n_chunks <= 8
    sub = lax.broadcasted_iota(jnp.int32, (8, LANES), 0)
    blocks = [slice(8 * k, 8 * k + 8) for k in range(N_KEYS // 8)]
    add = lambda p, q: p + q
    rank_sum_distinct = float(sum(range(P_TOPK)) + P_TOPK * (N_KEYS - P_TOPK))

    tops = []
    rank2 = []
    tie = jnp.zeros((8, LANES), F32)
    p1 = p2 = None
    for c in range(n_chunks):
        s1, s2 = s1_chunks[c], s2_chunks[c]
        v1 = _sorted_top16(s1)
        v2 = _sorted_top16(s2)
        tops.append((v1, v2))
        ranks = []
        for blk in blocks:
            r = jnp.zeros((8, LANES), F32)
            for a in range(P_TOPK):
                r = jnp.where(v2[a] > s2[blk], float(a + 1), r)
            ranks.append(r)
        rank2.append(jnp.concatenate(ranks, axis=0))
        rank_sum = jnp.sum(functools.reduce(add, ranks), axis=0, keepdims=True)
        tie = jnp.where(rank_sum != rank_sum_distinct, 1.0, tie)
        for a in range(P_TOPK - 1):
            tie = jnp.where(v1[a] == v1[a + 1], 1.0, tie)
        n_ge = functools.reduce(add, [jnp.where(s1[blk] >= v1[P_TOPK - 1], 1.0, 0.0) for blk in blocks])
        tie = jnp.where(jnp.sum(n_ge, axis=0, keepdims=True) != float(P_TOPK), 1.0, tie)
        if c == 0:
            p1, p2 = list(v1), list(v2)
        else:
            p1 = [jnp.where(sub == c, v, p) for v, p in zip(v1, p1)]
            p2 = [jnp.where(sub == c, v, p) for v, p in zip(v2, p2)]

    counts, z, pair_tie = _pair_counts(p1, p2)
    tie = jnp.where((sub < n_chunks) & (pair_tie != 0.0), 1.0, tie)

    tables = []
    for c in range(n_chunks):
        s1, s2 = s1_chunks[c], s2_chunks[c]
        v1, v2 = tops[c]
        cnt = [jnp.broadcast_to(n[c:c + 1, :], (8, LANES)) for n in counts] + [jnp.zeros((8, LANES), F32)]
        npos = []
        for blk in blocks:
            n = cnt[0]
            for a in range(P_TOPK):
                n = jnp.where(v1[a] > s1[blk], cnt[a + 1], n)
            npos.append(n)
        tables.append((jnp.concatenate(npos, axis=0), jnp.exp(s1 - v1[0][0:1]) / z[c:c + 1, :],
                       rank2[c], jnp.exp(s2 - v2[0][0:1])))
    return tables, tie


def _dup_bf16_word(x):
    u = lax.bitcast_convert_type(x.astype(BF16).astype(F32), jnp.uint32)
    return lax.bitcast_convert_type(u | (u >> 16), jnp.int32)


def _peer_topk_kernel(x_ref, m_ref, wh_ref, wl_ref, kh_ref, kl_ref,
                      h2t_ref, ni_ref, ai_ref, rj_ref, bj_ref, qt_sc):
    hd = pl.program_id(1)

    @pl.when(hd == 0)
    def _():
        x = x_ref[...]
        h2 = x * (1.0 + m_ref[0, 4:5, :]) + m_ref[0, 3:4, :]
        h_hi, h_lo = _split(h2.T)
        h2t_ref[...] = h_hi
        qt_sc[...] = _mm3(wh_ref[...], wl_ref[...], h_hi, h_lo)

    qh = qt_sc[pl.ds(pl.multiple_of(hd * (2 * P_HALF), 2 * P_HALF), 2 * P_HALF), :]
    q1_hi, q1_lo = _split(qh[:P_HALF])
    q2_hi, q2_lo = _split(qh[P_HALF:])
    sc1 = _mm3(kh_ref[0], kl_ref[0], q1_hi, q1_lo)
    sc2 = _mm3(kh_ref[1], kl_ref[1], q2_hi, q2_lo)
    n_chunks = sc1.shape[1] // LANES
    chunk = [slice(c * LANES, (c + 1) * LANES) for c in range(n_chunks)]

    def emit(c, tables):
        npos, a_i, r2, b_j = tables
        ni_ref[0, :, chunk[c]] = _dup_bf16_word(npos)
        ai_ref[0, :, chunk[c]] = _dup_bf16_word(0.5 * a_i)
        rj_ref[0, :, chunk[c]] = r2.astype(BF16)
        bj_ref[0, :, chunk[c]] = b_j.astype(BF16)

    tables, tie = _retrieve_tie_free([sc1[:, ls] for ls in chunk], [sc2[:, ls] for ls in chunk])
    for c in range(n_chunks):
        emit(c, tables[c])

    @pl.when(jnp.max(tie) > 0.0)
    def _():
        for c in range(n_chunks):
            emit(c, _retrieve_exact(sc1[:, chunk[c]], sc2[:, chunk[c]]))


def _peer_topk(x, mods_l, wq_t_hi, wq_t_lo, keys_hi, keys_lo, *, t_p, s_len):
    t, d = x.shape
    tm = TM_TOPK
    n_p = t_p // tm
    tpb = s_len // tm
    row_tab = jax.ShapeDtypeStruct((P_HEADS, N_KEYS, t), jnp.int32)
    row_spec = pl.BlockSpec((1, N_KEYS, tm), lambda i, h: (h, 0, i))
    col_tab = jax.ShapeDtypeStruct((P_HEADS, N_KEYS, t), BF16)
    col_spec = row_spec
    return pl.pallas_call(
        _peer_topk_kernel,
        grid=(t // tm, P_HEADS),
        in_specs=[
            pl.BlockSpec((tm, d), lambda i, h: (i, 0)),
            pl.BlockSpec((1, 6, d), lambda i, h: (_tile_group(i, n_p, tpb), 0, 0)),
            pl.BlockSpec((d, d), lambda i, h: (0, 0)),
            pl.BlockSpec((d, d), lambda i, h: (0, 0)),
            pl.BlockSpec((2, N_KEYS, P_HALF), lambda i, h: (h, 0, 0)),
            pl.BlockSpec((2, N_KEYS, P_HALF), lambda i, h: (h, 0, 0)),
        ],
        out_specs=[pl.BlockSpec((d, tm), lambda i, h: (0, i)), row_spec, row_spec, col_spec, col_spec],
        out_shape=[jax.ShapeDtypeStruct((d, t), BF16), row_tab, row_tab, col_tab, col_tab],
        scratch_shapes=[pltpu.VMEM((d, tm), F32)],
        compiler_params=_params(("parallel", "arbitrary")),
    )(x, mods_l, wq_t_hi, wq_t_lo, keys_hi, keys_lo)


def _row_tile(row):
    tile = pltpu.bitcast(jnp.broadcast_to(row, (8, row.shape[1])), BF16)
    return jnp.concatenate([tile] * (N_KEYS // tile.shape[0]), axis=0)


GELU_C1 = math.sqrt(2.0 / math.pi)
GELU_C2 = 0.044715 * GELU_C1
K_CHUNK = 512


def _peer_dense_kernel(h2t0_ref, h2tn_ref, eu0_ref, eun_ref, evt_ref, ni_ref, ai_ref, rj_ref, bj_ref,
                       x_ref, m_ref, g_ref, b_ref, y_ref, acc_sc, act0_sc, act1_sc, wt_sc, *, n_e):
    e = pl.program_id(1)

    @pl.when((pl.program_id(0) == 0) & (e == 0))
    def _():
        act0_sc[...] = _mm(eu0_ref[...], h2t0_ref[...]).astype(BF16)

    @pl.when(e == 0)
    def _():
        acc_sc[...] = jnp.zeros_like(acc_sc)

    te, tm = wt_sc.shape
    zero = jnp.zeros((), BF16)

    def after(words, product):
        bits = lax.bitcast_convert_type(product[0:1, :], jnp.int32)
        return words + lax.shift_right_logical(lax.shift_right_logical(bits, 16), 16)

    def step(act_ref, next_ref):
        part = None
        nxt_prev = None
        for kc in range(te // K_CHUNK):
            rows = slice(kc * K_CHUNK, (kc + 1) * K_CHUNK)
            nxt = _mm(eun_ref[rows, :], h2tn_ref[...])
            next_ref[rows, :] = nxt.astype(BF16)
            for il in range(kc * K_CHUNK // N_KEYS, (kc + 1) * K_CHUNK // N_KEYS):
                rs = slice(il * N_KEYS, (il + 1) * N_KEYS)
                gate = None
                for h in range(P_HEADS):
                    cnt_words = ni_ref[h, il:il + 1, :]
                    if nxt_prev is not None and h == 0 and il * N_KEYS == kc * K_CHUNK:
                        cnt_words = after(cnt_words, nxt_prev)
                    cnt = _row_tile(cnt_words)
                    half_a = _row_tile(ai_ref[h, il:il + 1, :])
                    gh = jnp.where(rj_ref[h] < cnt, bj_ref[h], zero) * half_a
                    gate = gh if gate is None else gate + gh
                a = act_ref[rs, :]
                t = jnp.tanh(a * (GELU_C1 + GELU_C2 * (a * a)))
                wt_sc[rs, :] = gate * (a + a * t)
            p = _mm(evt_ref[:, rows], wt_sc[rows, :])
            part = p if part is None else part + p
            nxt_prev = nxt
        acc_sc[...] += part

    @pl.when(e % 2 == 0)
    def _():
        step(act0_sc, act1_sc)

    @pl.when(e % 2 == 1)
    def _():
        step(act1_sc, act0_sc)

    @pl.when(e == n_e - 1)
    def _():
        f = acc_sc[...].T
        z = DN_ALPHA * x_ref[...] + m_ref[0, 5:6, :] * f
        y_ref[...] = _layer_norm(z, g_ref[...], b_ref[...])


def _peer_dense(h2t, e_u, e_vt, layer, ni, ai, rj, bj, x, mods_l, ln_g, ln_b, *, t_p, s_len):
    t, d = x.shape
    tm = TM_PEER
    te = TE_PEER
    n_p = t_p // tm
    tpb = s_len // tm
    n_exp = e_u.shape[1]
    n_e = n_exp // te
    n_i = te // N_KEYS
    n_t = t // tm
    assert n_e % 2 == 0
    nxt_tok = lambda i, e: jnp.minimum(i + (e + 1) // n_e, n_t - 1)
    return pl.pallas_call(
        functools.partial(_peer_dense_kernel, n_e=n_e),
        grid=(n_t, n_e),
        in_specs=[
            pl.BlockSpec((d, tm), lambda i, e: (0, 0)),
            pl.BlockSpec((d, tm), lambda i, e: (0, nxt_tok(i, e))),
            pl.BlockSpec((None, te, d), lambda i, e: (layer, 0, 0)),
            pl.BlockSpec((None, te, d), lambda i, e: (layer, (e + 1) % n_e, 0)),
            pl.BlockSpec((None, d, te), lambda i, e: (layer, 0, e)),
            pl.BlockSpec((P_HEADS, n_i, tm), lambda i, e: (0, e, i)),
            pl.BlockSpec((P_HEADS, n_i, tm), lambda i, e: (0, e, i)),
            pl.BlockSpec((P_HEADS, N_KEYS, tm), lambda i, e: (0, 0, i)),
            pl.BlockSpec((P_HEADS, N_KEYS, tm), lambda i, e: (0, 0, i)),
            pl.BlockSpec((tm, d), lambda i, e: (i, 0)),
            pl.BlockSpec((1, 6, d), lambda i, e: (_tile_group(i, n_p, tpb), 0, 0)),
            pl.BlockSpec((1, d), lambda i, e: (0, 0)),
            pl.BlockSpec((1, d), lambda i, e: (0, 0)),
        ],
        out_specs=pl.BlockSpec((tm, d), lambda i, e: (i, 0)),
        out_shape=jax.ShapeDtypeStruct((t, d), F32),
        scratch_shapes=[pltpu.VMEM((d, tm), F32), pltpu.VMEM((te, tm), BF16), pltpu.VMEM((te, tm), BF16),
                        pltpu.VMEM((te, tm), BF16)],
        compiler_params=_params(("arbitrary", "arbitrary")),
    )(h2t, h2t, e_u, e_u, e_vt, ni, ai, rj, bj, x, mods_l, ln_g.reshape(1, d), ln_b.reshape(1, d))


def _rope_tables(n_tok):
    rows = n_tok // GRID_W
    row = jnp.repeat(jnp.arange(rows), GRID_W).astype(F32)
    col = jnp.tile(jnp.arange(GRID_W), rows).astype(F32)
    n_freq = A_HD // 4
    inv = ROPE_THETA ** (-jnp.arange(n_freq, dtype=F32) / n_freq)
    ang = jnp.concatenate([row[:, None] * inv, col[:, None] * inv], -1)
    cos = jnp.repeat(jnp.cos(ang), 2, axis=-1)
    sin = jnp.repeat(jnp.sin(ang), 2, axis=-1)
    sign = jnp.tile(jnp.asarray([-1.0, 1.0], F32), A_HD // 2)
    reps = LANES // A_HD
    return jnp.tile(cos, (1, reps)), jnp.tile(sin * sign, (1, reps))


def _gqa_weights(w_qkv, q_norm, k_norm, w_o):
    d = w_qkv.shape[0]
    nq = A_HEADS * A_HD
    half = (np.arange(A_HEADS) // A_REP) % 2
    sel = np.stack([half == 0, half == 1], axis=1).astype(np.float32)
    wq = w_qkv[:, :nq].reshape(d, A_HEADS, 1, A_HD) * sel[None, :, :, None]
    w_exp = jnp.concatenate([wq.reshape(d, A_HEADS * LANES), w_qkv[:, nq:]], axis=1).astype(BF16)
    gq = jnp.tile(q_norm, 2 * A_HEADS).reshape(1, -1)
    gk = jnp.tile(k_norm, A_KV).reshape(1, -1)
    wo = w_o.reshape(A_HEADS, 1, A_HD, d) * sel[:, :, None, None]
    return w_exp, gq, gk, wo.reshape(A_HEADS * LANES, d).astype(BF16)


def _diff_weights(w_qkv):
    d = w_qkv.shape[0]
    eye = np.eye(2, dtype=np.float32)
    wq = w_qkv[:, :D_W].reshape(d, D_HEADS, 1, 2, D_HD) * eye[None, None, :, :, None]
    return jnp.concatenate([wq.reshape(d, D_HEADS * 2 * LANES), w_qkv[:, D_W:]], axis=1).astype(BF16)


def _split_f32(w):
    hi = w.astype(BF16)
    return hi, (w - hi.astype(F32)).astype(BF16)


def kernel(x_prompt, x_sample, cache_a_k, cache_a_v, cache_d_k, cache_d_v, c, c_ctx, w_mod, b_mod, ln_g, ln_b, a_w_qkv, a_q_norm, a_k_norm, a_w_o, d_w_qkv, d_lambda_q1, d_lambda_k1, d_lambda_q2, d_lambda_k2, d_sub_norm, d_w_o, g_w_in, g_b_in, g_v_norm_g, g_v_norm_b, g_w_s, g_b_s, g_w_out, p_w_q, p_sub_keys, p_expert_u, p_expert_v):
    batch, seq, d = x_prompt.shape
    dec_batch, dec_seq, _ = x_sample.shape
    past = cache_a_k.shape[2]
    t_p = batch * seq
    t_s = dec_batch * dec_seq
    dims = dict(t_p=t_p, s_len=dec_seq)

    x = jnp.concatenate([x_prompt.reshape(t_p, d), x_sample.reshape(t_s, d)], axis=0)
    e_u_all = p_expert_u.astype(BF16)
    e_vt_all = jnp.swapaxes(p_expert_v, 1, 2).astype(BF16)
    n_cond = -(-(1 + dec_batch) // 8) * 8
    cond = jnp.zeros((n_cond, d), F32).at[0].set(c_ctx).at[1:1 + dec_batch].set(c)
    mods = _modulation(cond, w_mod, b_mod).reshape(DEPTH, n_cond, 6, d)
    rope_c, rope_s = _rope_tables(dec_seq)

    ak, av, dk, dv = [], [], [], []
    for i in range(DEPTH):
        kind, j = i % N_MIXERS, i // N_MIXERS
        mods_l = mods[i]
        if kind == 0:
            w_exp, gq, gk, wo = _gqa_weights(a_w_qkv[j], a_q_norm[j], a_k_norm[j], a_w_o[j])
            q, k_att, v_att, k_f, v_f = _qkv_project(
                x, mods_l, w_exp, gq, gk, rope_c, rope_s, nq=A_HEADS, nk=A_KV * A_HD // LANES,
                nv=A_KV * A_HD // LANES, use_norm=True, q_scale=A_HD ** -0.5, **dims)
            ctx_k = cache_a_k[:, j].reshape(dec_batch, past, A_KV * A_HD).astype(BF16)
            ctx_v = cache_a_v[:, j].reshape(dec_batch, past, A_KV * A_HD).astype(BF16)
            o = _attention(_gqa_attn_kernel, (), q, k_att, v_att, ctx_k, ctx_v, n_out=A_HEADS * LANES,
                           t_p=t_p, seq_p=seq, batch_s=dec_batch, seq_s=dec_seq)
            ak.append(k_f[:t_p].reshape(batch, seq, A_KV, A_HD))
            av.append(v_f[:t_p].reshape(batch, seq, A_KV, A_HD))
        elif kind == 1:
            lam_init = 0.8 - 0.6 * math.exp(-0.3 * i)
            w_exp = _diff_weights(d_w_qkv[j])
            q, k_att, v_att, k_f, v_f = _qkv_project(
                x, mods_l, w_exp, jnp.ones((1, 2 * D_W), F32), jnp.ones((1, D_W), F32),
                rope_c, rope_s, nq=2 * D_HEADS, nk=D_HEADS, nv=D_HEADS,
                use_norm=False, q_scale=D_HD ** -0.5, **dims)
            ctx_k = cache_d_k[:, j].reshape(dec_batch, past, D_W).astype(BF16)
            ctx_v = cache_d_v[:, j].reshape(dec_batch, past, D_HEADS * D_VD).astype(BF16)
            lvec = jnp.stack([d_lambda_q1[j], d_lambda_k1[j], d_lambda_q2[j], d_lambda_k2[j]])
            kern = functools.partial(_diff_attn_kernel, lam_init=lam_init)
            o = _attention(kern, (lvec, d_sub_norm[j].reshape(1, D_VD)), q, k_att, v_att, ctx_k, ctx_v,
                           n_out=D_HEADS * D_VD, t_p=t_p, seq_p=seq, batch_s=dec_batch, seq_s=dec_seq)
            wo = d_w_o[j].astype(BF16)
            dk.append(k_f[:t_p].reshape(batch, seq, D_HEADS, 2, D_HD))
            dv.append(v_f[:t_p].reshape(batch, seq, D_HEADS, D_VD))
        if kind == 2:
            x = _gmlp(x, mods_l, g_w_in[j], g_b_in[j], g_v_norm_g[j], g_v_norm_b[j], g_w_s[j], g_b_s[j],
                      g_w_out[j], ln_g[i, 0], ln_b[i, 0], **dims)
        else:
            x = _proj_ln(*o, wo, x, mods_l, ln_g[i, 0], ln_b[i, 0], gate_idx=2, **dims)

        wq_hi, wq_lo = _split_f32(p_w_q[i].T)
        keys = p_sub_keys[i].reshape(P_HEADS * 2, N_KEYS, P_HALF)
        k_hi, k_lo = _split_f32(keys)
        h2t, ni, ai, rj, bj = _peer_topk(x, mods_l, wq_hi, wq_lo, k_hi, k_lo, **dims)
        x = _peer_dense(h2t, e_u_all, e_vt_all, i, ni, ai, rj, bj, x, mods_l, ln_g[i, 1], ln_b[i, 1], **dims)

    y_prompt = x[:t_p].reshape(batch, seq, d)
    y_sample = x[t_p:].reshape(dec_batch, dec_seq, d)
    return (y_prompt, y_sample, jnp.stack(ak, axis=1), jnp.stack(av, axis=1),
            jnp.stack(dk, axis=1), jnp.stack(dv, axis=1))
```

```python
import functools
import math

import numpy as np
import jax
import jax.numpy as jnp
from jax import lax
from jax.experimental import pallas as pl
from jax.experimental.pallas import tpu as pltpu

F32 = jnp.float32
BF16 = jnp.bfloat16

D_MODEL = 1024
DEPTH = 4
GRID_W = 64
N_MIXERS = 3
A_HEADS = 16
A_KV = 4
A_REP = A_HEADS // A_KV
A_HD = 64
D_HEADS = 8
D_HD = 64
D_VD = 2 * D_HD
D_W = D_HEADS * 2 * D_HD
G_DIM = 1024
G_GROUPS = 8
G_CHUNK = 128
P_HEADS = 8
N_KEYS = 128
P_HALF = 64
P_TOPK = 16
ROPE_THETA = 10000.0
DN_ALPHA = (2 * DEPTH) ** 0.25
EPS = 1e-6

LANES = 128
NEG_INF = float("-inf")
VMEM_LIMIT = 56 * 1024 * 1024

TM_QKV = 256
TM_PROJ = 512
TQ_ATTN = 256
TM_GMLP = 256
TM_TOPK = 1024
TM_PEER = 512
TE_PEER = 2048


def _mm(a, b):
    return jnp.dot(a, b, preferred_element_type=F32)


def _mm_nt(a, b):
    return lax.dot_general(a, b, (((1,), (1,)), ((), ())), preferred_element_type=F32)


def _split(x):
    hi = x.astype(BF16)
    lo = (x - hi.astype(F32)).astype(BF16)
    return hi, lo


def _mm3(a_hi, a_lo, b_hi, b_lo):
    return _mm(a_hi, b_hi) + _mm(a_hi, b_lo) + _mm(a_lo, b_hi)


def _layer_norm(z, g, b):
    mu = jnp.mean(z, axis=-1, keepdims=True)
    zc = z - mu
    var = jnp.mean(zc * zc, axis=-1, keepdims=True)
    return zc * lax.rsqrt(var + EPS) * g + b


def _params(sem, flags=None):
    return pltpu.CompilerParams(dimension_semantics=sem, vmem_limit_bytes=VMEM_LIMIT, flags=flags)


def _tile_group(i, n_p_tiles, tiles_per_batch):
    return jnp.where(i < n_p_tiles, 0, 1 + (i - n_p_tiles) // tiles_per_batch)


def _tile_pos_block(i, n_p_tiles, tiles_per_batch):
    return jnp.where(i < n_p_tiles, 0, (i - n_p_tiles) % tiles_per_batch)


def _mods_kernel(c_ref, w_ref, b_ref, o_ref):
    c = c_ref[...]
    a = c * jax.nn.sigmoid(c)
    a_hi, a_lo = _split(a)
    w_hi, w_lo = _split(w_ref[0])
    o_ref[0] = _mm3(a_hi, a_lo, w_hi, w_lo) + b_ref[0]


def _modulation(cond, w_mod, b_mod):
    depth, d, n = w_mod.shape
    rows = cond.shape[0]
    tn = 1536
    return pl.pallas_call(
        _mods_kernel,
        grid=(depth, n // tn),
        in_specs=[
            pl.BlockSpec((rows, d), lambda l, j: (0, 0)),
            pl.BlockSpec((1, d, tn), lambda l, j: (l, 0, j)),
            pl.BlockSpec((1, 1, tn), lambda l, j: (l, 0, j)),
        ],
        out_specs=pl.BlockSpec((1, rows, tn), lambda l, j: (l, 0, j)),
        out_shape=jax.ShapeDtypeStruct((depth, rows, n), F32),
        compiler_params=_params(("parallel", "parallel")),
    )(cond, w_mod, b_mod.reshape(depth, 1, n))


def _swap_pairs(x):
    lane = lax.broadcasted_iota(jnp.int32, x.shape, 1)
    nxt = pltpu.roll(x, LANES - 1, 1)
    prv = pltpu.roll(x, 1, 1)
    return jnp.where((lane & 1) == 0, nxt, prv)


def _qkv_kernel(x_ref, m_ref, w_ref, bd_ref, gq_ref, gk_ref, cs_ref, sn_ref,
                q_ref, ka_ref, va_ref, kf_ref, vf_ref,
                *, nq, nk, nv, use_norm, n_p_tiles, q_scale):
    x = x_ref[...]
    h = x * (1.0 + m_ref[0, 1:2, :]) + m_ref[0, 0:1, :]
    y = _mm(h.astype(BF16), w_ref[...])
    is_lat = pl.program_id(0) >= n_p_tiles
    cs = jnp.where(is_lat, cs_ref[...], 1.0)
    sn = jnp.where(is_lat, sn_ref[...], 0.0)
    bd = bd_ref[...]

    def head_norm(yg, gain):
        s_hi, s_lo = _split(yg * yg)
        ms = _mm(s_hi, bd) + _mm(s_lo, bd)
        return yg * lax.rsqrt(ms + EPS) * gain

    def rope(yg):
        return yg * cs + _swap_pairs(yg) * sn

    for g in range(nq):
        sl = slice(g * LANES, (g + 1) * LANES)
        yg = y[:, sl]
        if use_norm:
            yg = head_norm(yg, gq_ref[:, sl])
        q_ref[:, sl] = (rope(yg) * q_scale).astype(BF16)
    for g in range(nk):
        sl = slice(g * LANES, (g + 1) * LANES)
        yg = y[:, (nq + g) * LANES:(nq + g + 1) * LANES]
        if use_norm:
            yg = head_norm(yg, gk_ref[:, sl])
        kf_ref[:, sl] = yg
        ka_ref[:, sl] = rope(yg).astype(BF16)
    for g in range(nv):
        sl = slice(g * LANES, (g + 1) * LANES)
        yg = y[:, (nq + nk + g) * LANES:(nq + nk + g + 1) * LANES]
        vf_ref[:, sl] = yg
        va_ref[:, sl] = yg.astype(BF16)


def _qkv_project(x, mods_l, w_exp, gq, gk, rope_c, rope_s, *, nq, nk, nv, use_norm, t_p, s_len, q_scale):
    t, d = x.shape
    tm = TM_QKV
    n_p = t_p // tm
    tpb = s_len // tm
    ncol = (nq + nk + nv) * LANES
    bd = np.kron(np.eye(2, dtype=np.float32), np.full((64, 64), 1.0 / 64.0, np.float32))
    bd = jnp.asarray(bd, BF16)
    grp = lambda i: (_tile_group(i, n_p, tpb), 0, 0)
    pos = lambda i: (_tile_pos_block(i, n_p, tpb), 0)
    row = lambda i: (i, 0)
    fixed = lambda i: (0, 0)
    kern = functools.partial(_qkv_kernel, nq=nq, nk=nk, nv=nv, use_norm=use_norm,
                             n_p_tiles=n_p, q_scale=q_scale)
    return pl.pallas_call(
        kern,
        grid=(t // tm,),
        in_specs=[
            pl.BlockSpec((tm, d), row),
            pl.BlockSpec((1, 6, d), grp),
            pl.BlockSpec((d, ncol), fixed),
            pl.BlockSpec((LANES, LANES), fixed),
            pl.BlockSpec((1, nq * LANES), fixed),
            pl.BlockSpec((1, nk * LANES), fixed),
            pl.BlockSpec((tm, LANES), pos),
            pl.BlockSpec((tm, LANES), pos),
        ],
        out_specs=[
            pl.BlockSpec((tm, nq * LANES), row),
            pl.BlockSpec((tm, nk * LANES), row),
            pl.BlockSpec((tm, nv * LANES), row),
            pl.BlockSpec((tm, nk * LANES), row),
            pl.BlockSpec((tm, nv * LANES), row),
        ],
        out_shape=[
            jax.ShapeDtypeStruct((t, nq * LANES), BF16),
            jax.ShapeDtypeStruct((t, nk * LANES), BF16),
            jax.ShapeDtypeStruct((t, nv * LANES), BF16),
            jax.ShapeDtypeStruct((t, nk * LANES), F32),
            jax.ShapeDtypeStruct((t, nv * LANES), F32),
        ],
        compiler_params=_params(("parallel",)),
    )(x, mods_l, w_exp, bd, gq, gk, rope_c, rope_s)


def _softmax_pv(qh, k_all, v_all):
    scores = [_mm_nt(qh, k) for k in k_all]
    m = scores[0].max(axis=-1, keepdims=True)
    for s in scores[1:]:
        m = jnp.maximum(m, s.max(axis=-1, keepdims=True))
    o = None
    l = None
    for s, v in zip(scores, v_all):
        p = jnp.exp(s - m)
        ls = p.sum(axis=-1, keepdims=True)
        os_ = _mm(p.astype(BF16), v)
        o = os_ if o is None else o + os_
        l = ls if l is None else l + ls
    return o, l


def _gqa_attn_kernel(*refs, has_ctx):
    if has_ctx:
        q_ref, k_ref, v_ref, kc_ref, vc_ref, o_ref = refs
    else:
        q_ref, k_ref, v_ref, o_ref = refs
    tq = q_ref.shape[0]
    for g in range(A_KV):
        ks = slice((g // 2) * LANES, (g // 2 + 1) * LANES)
        heads = [slice((g * A_REP + r) * LANES, (g * A_REP + r + 1) * LANES) for r in range(A_REP)]
        k_all = [k_ref[:, ks]]
        v_all = [v_ref[:, ks]]
        if has_ctx:
            k_all.append(kc_ref[0, :, ks])
            v_all.append(vc_ref[0, :, ks])
        o, l = _softmax_pv(jnp.concatenate([q_ref[:, sl] for sl in heads], axis=0), k_all, v_all)
        o = (o / l).astype(BF16)
        for r, sl in enumerate(heads):
            o_ref[:, sl] = o[r * tq:(r + 1) * tq, :]


def _diff_attn_kernel(*refs, has_ctx, lam_init):
    if has_ctx:
        lv_ref, sg_ref, q_ref, k_ref, v_ref, kc_ref, vc_ref, o_ref = refs
    else:
        lv_ref, sg_ref, q_ref, k_ref, v_ref, o_ref = refs
    lv = lv_ref[...]
    lam = (jnp.exp(jnp.sum(lv[0:1] * lv[1:2], axis=-1, keepdims=True))
           - jnp.exp(jnp.sum(lv[2:3] * lv[3:4], axis=-1, keepdims=True)) + lam_init)
    for h in range(D_HEADS):
        ks = slice(h * LANES, (h + 1) * LANES)
        k_all = [k_ref[:, ks]]
        v_all = [v_ref[:, ks]]
        if has_ctx:
            k_all.append(kc_ref[0, :, ks])
            v_all.append(vc_ref[0, :, ks])
        tq = q_ref.shape[0]
        q2 = jnp.concatenate([q_ref[:, (2 * h + j) * LANES:(2 * h + j + 1) * LANES] for j in range(2)], axis=0)
        o, l = _softmax_pv(q2, k_all, v_all)
        o = o / l
        o = o[:tq] - lam * o[tq:]
        ms = jnp.mean(o * o, axis=-1, keepdims=True)
        o = o * lax.rsqrt(ms + EPS) * sg_ref[...] * (1.0 - lam_init)
        o_ref[:, ks] = o.astype(BF16)


def _attention(kern, extra, q, k, v, ctx_k, ctx_v, *, n_out, t_p, seq_p, batch_s, seq_s):
    nqc = q.shape[1]
    nkc = k.shape[1]
    extra_specs = [pl.BlockSpec(a.shape, lambda *_: (0, 0)) for a in extra]
    n_seq_p = t_p // seq_p
    o_p = pl.pallas_call(
        functools.partial(kern, has_ctx=False),
        grid=(n_seq_p,),
        in_specs=extra_specs + [
            pl.BlockSpec((seq_p, nqc), lambda b: (b, 0)),
            pl.BlockSpec((seq_p, nkc), lambda b: (b, 0)),
            pl.BlockSpec((seq_p, nkc), lambda b: (b, 0)),
        ],
        out_specs=pl.BlockSpec((seq_p, n_out), lambda b: (b, 0)),
        out_shape=jax.ShapeDtypeStruct((t_p, n_out), BF16),
        compiler_params=_params(("parallel",)),
    )(*extra, q, k, v)
    tq = TQ_ATTN
    nqt = seq_s // tq
    assert t_p % seq_s == 0 and t_p % tq == 0
    q_off = t_p // tq
    k_off = t_p // seq_s
    past = ctx_k.shape[1]
    o_s = pl.pallas_call(
        functools.partial(kern, has_ctx=True),
        grid=(batch_s, nqt),
        in_specs=extra_specs + [
            pl.BlockSpec((tq, nqc), lambda b, i: (q_off + b * nqt + i, 0)),
            pl.BlockSpec((seq_s, nkc), lambda b, i: (k_off + b, 0)),
            pl.BlockSpec((seq_s, nkc), lambda b, i: (k_off + b, 0)),
            pl.BlockSpec((1, past, nkc), lambda b, i: (b, 0, 0)),
            pl.BlockSpec((1, past, nkc), lambda b, i: (b, 0, 0)),
        ],
        out_specs=pl.BlockSpec((tq, n_out), lambda b, i: (b * nqt + i, 0)),
        out_shape=jax.ShapeDtypeStruct((batch_s * seq_s, n_out), BF16),
        compiler_params=_params(("parallel", "arbitrary")),
    )(*extra, q, k, v, ctx_k, ctx_v)
    return o_p, o_s


def _proj_ln_kernel(op_ref, os_ref, w_ref, x_ref, m_ref, g_ref, b_ref, y_ref, *, gate_idx, n_p_tiles):
    o = jnp.where(pl.program_id(0) < n_p_tiles, op_ref[...], os_ref[...])
    y = _mm(o, w_ref[...])
    z = DN_ALPHA * x_ref[...] + m_ref[0, gate_idx:gate_idx + 1, :] * y
    y_ref[...] = _layer_norm(z, g_ref[...], b_ref[...])


def _proj_ln(o_p, o_s, w, x, mods_l, ln_g, ln_b, *, gate_idx, t_p, s_len):
    t, d = x.shape
    tm = TM_PROJ
    n_p = t_p // tm
    tpb = s_len // tm
    kin = o_p.shape[1]
    row = lambda i: (i, 0)
    fixed = lambda i: (0, 0)
    return pl.pallas_call(
        functools.partial(_proj_ln_kernel, gate_idx=gate_idx, n_p_tiles=n_p),
        grid=(t // tm,),
        in_specs=[
            pl.BlockSpec((tm, kin), lambda i: (jnp.minimum(i, n_p - 1), 0)),
            pl.BlockSpec((tm, kin), lambda i: (jnp.maximum(i - n_p, 0), 0)),
            pl.BlockSpec((kin, d), fixed),
            pl.BlockSpec((tm, d), row),
            pl.BlockSpec((1, 6, d), lambda i: (_tile_group(i, n_p, tpb), 0, 0)),
            pl.BlockSpec((1, d), fixed),
            pl.BlockSpec((1, d), fixed),
        ],
        out_specs=pl.BlockSpec((tm, d), row),
        out_shape=jax.ShapeDtypeStruct((t, d), F32),
        compiler_params=_params(("parallel",)),
    )(o_p, o_s, w, x, mods_l, ln_g.reshape(1, d), ln_b.reshape(1, d))


def _gmlp_kernel(x_ref, m_ref, win_ref, bin_ref, vg_ref, vb_ref, ws_ref, bs_ref, wout_ref,
                 g_ref, b_ref, y_ref, uv_sc):
    x = x_ref[...]
    h = x * (1.0 + m_ref[0, 1:2, :]) + m_ref[0, 0:1, :]
    z = jax.nn.gelu(_mm(h.astype(BF16), win_ref[...]) + bin_ref[...])
    u = z[:, :G_DIM]
    v = _layer_norm(z[:, G_DIM:], vg_ref[...], vb_ref[...])
    tm = x.shape[0]
    for c in range(tm // G_CHUNK):
        rs = slice(c * G_CHUNK, (c + 1) * G_CHUNK)
        for g in range(G_GROUPS):
            cs = slice(g * LANES, (g + 1) * LANES)
            vm = _mm(ws_ref[g], v[rs, cs].astype(BF16)) + bs_ref[g]
            uv_sc[rs, cs] = (u[rs, cs] * vm).astype(BF16)
    y = _mm(uv_sc[...], wout_ref[...])
    zz = DN_ALPHA * x + m_ref[0, 2:3, :] * y
    y_ref[...] = _layer_norm(zz, g_ref[...], b_ref[...])


def _gmlp(x, mods_l, w_in, b_in, vg, vb, w_s, b_s, w_out, ln_g, ln_b, *, t_p, s_len):
    t, d = x.shape
    tm = TM_GMLP
    n_p = t_p // tm
    tpb = s_len // tm
    row = lambda i: (i, 0)
    fixed = lambda i: (0, 0)
    fixed3 = lambda i: (0, 0, 0)
    b_s_b = jnp.broadcast_to(b_s[:, :, None], (G_GROUPS, G_CHUNK, LANES))
    return pl.pallas_call(
        _gmlp_kernel,
        grid=(t // tm,),
        in_specs=[
            pl.BlockSpec((tm, d), row),
            pl.BlockSpec((1, 6, d), lambda i: (_tile_group(i, n_p, tpb), 0, 0)),
            pl.BlockSpec((d, 2 * G_DIM), fixed),
            pl.BlockSpec((1, 2 * G_DIM), fixed),
            pl.BlockSpec((1, G_DIM), fixed),
            pl.BlockSpec((1, G_DIM), fixed),
            pl.BlockSpec((G_GROUPS, G_CHUNK, G_CHUNK), fixed3),
            pl.BlockSpec((G_GROUPS, G_CHUNK, LANES), fixed3),
            pl.BlockSpec((G_DIM, d), fixed),
            pl.BlockSpec((1, d), fixed),
            pl.BlockSpec((1, d), fixed),
        ],
        out_specs=pl.BlockSpec((tm, d), row),
        out_shape=jax.ShapeDtypeStruct((t, d), F32),
        scratch_shapes=[pltpu.VMEM((tm, G_DIM), BF16)],
        compiler_params=_params(("parallel",)),
    )(x, mods_l, w_in.astype(BF16), b_in.reshape(1, -1), vg.reshape(1, -1), vb.reshape(1, -1),
      w_s.astype(BF16), b_s_b, w_out.astype(BF16), ln_g.reshape(1, d), ln_b.reshape(1, d))


def _top16(sc):
    n, L = sc.shape
    pos = lax.broadcasted_iota(jnp.int32, (n, L), 0).astype(F32)
    slot = lax.broadcasted_iota(jnp.int32, (P_TOPK, L), 0)

    def body(k, carry):
        cur, rank, vals = carry
        m = jnp.max(cur, axis=0, keepdims=True)
        idx = jnp.min(jnp.where(cur == m, pos, float(n)), axis=0, keepdims=True)
        hit = pos == idx
        cur = jnp.where(hit, NEG_INF, cur)
        rank = jnp.where(hit, lax.convert_element_type(k, F32), rank)
        vals = jnp.where(slot == k, m, vals)
        return cur, rank, vals

    init = (sc, jnp.full((n, L), 64.0, F32), jnp.zeros((P_TOPK, L), F32))
    _, rank, vals = lax.fori_loop(0, P_TOPK, body, init)
    return vals, rank


_CAND_BLOCKS = ((0, 0, 8), (0, 8, 8), (1, 0, 8), (2, 0, 5), (3, 0, 4), (4, 0, 3), (5, 0, 2), (6, 0, 2), (7, 0, 2))


def _pair_select(v1, v2):
    L = v1.shape[1]
    r8 = lax.broadcasted_iota(jnp.int32, (8, L), 0).astype(F32)
    blocks = []
    codes = []
    for a, b0, nb in _CAND_BLOCKS:
        c = v1[a:a + 1] + v2[b0:b0 + 8]
        if nb < 8:
            c = jnp.where(r8 < float(nb), c, NEG_INF)
        blocks.append(c)
        codes.append(r8 + float(a * P_TOPK + b0))
    blocks.append(v1[8:16] + v2[0:1])
    codes.append((r8 + 8.0) * float(P_TOPK))
    cand0 = jnp.concatenate(blocks, axis=0)
    code = jnp.concatenate(codes, axis=0)

    def body(k, carry):
        cur, sel = carry
        m = jnp.max(cur, axis=0, keepdims=True)
        idx = jnp.min(jnp.where(cur == m, code, 1e9), axis=0, keepdims=True)
        hit = code == idx
        return jnp.where(hit, NEG_INF, cur), jnp.where(hit, 1.0, sel)

    _, sel = lax.fori_loop(0, P_TOPK, body, (cand0, jnp.zeros_like(cand0)))
    top = v1[0:1] + v2[0:1]
    z = jnp.sum(jnp.where(sel > 0.0, jnp.exp(cand0 - top), 0.0), axis=0, keepdims=True)
    cnt_lo = jnp.zeros((8, L), F32)
    starts = (0, 16, 24, 32, 40, 48, 56, 64, 72)
    for a in range(8):
        n_a = jnp.sum(sel[starts[a]:starts[a + 1]], axis=0, keepdims=True)
        cnt_lo = jnp.where(r8 == float(a), n_a, cnt_lo)
    count = jnp.concatenate([cnt_lo, sel[72:80]], axis=0)
    return count, z


def _retrieve_exact(s1, s2):
    v1, r1 = _top16(s1)
    v2, r2 = _top16(s2)
    count, z = _pair_select(v1, v2)
    npos = jnp.zeros_like(r1)
    for a in range(P_TOPK):
        npos = jnp.where(r1 == float(a), count[a:a + 1], npos)
    return npos, jnp.exp(s1 - v1[0:1]) / z, r2, jnp.exp(s2 - v2[0:1])


def _oddeven_merge_sort_pairs(n):
    pairs = []
    p = 1
    while p < n:
        k = p
        while k >= 1:
            for j in range(k % p, n - k, 2 * k):
                for i in range(min(k, n - j - k)):
                    if (i + j) // (2 * p) == (i + j + k) // (2 * p):
                        pairs.append((i + j, i + j + k))
            k //= 2
        p *= 2
    return tuple(pairs)


_SORT16 = _oddeven_merge_sort_pairs(P_TOPK)


def _sort16_desc(v):
    for i, j in _SORT16:
        v[i], v[j] = jnp.maximum(v[i], v[j]), jnp.minimum(v[i], v[j])


def _bitonic_merge16_desc(v):
    dist = P_TOPK // 2
    while dist >= 1:
        for i in range(P_TOPK):
            if not i & dist:
                v[i], v[i + dist] = jnp.maximum(v[i], v[i + dist]), jnp.minimum(v[i], v[i + dist])
        dist //= 2


def _merge_top16(a, b):
    v = [jnp.maximum(a[i], b[P_TOPK - 1 - i]) for i in range(P_TOPK)]
    _bitonic_merge16_desc(v)
    return v


def _sorted_top16(s):
    v = [s[8 * k:8 * k + 8, :] for k in range(N_KEYS // 8)]
    _sort16_desc(v)
    for shift in (4, 2, 1):
        v = [jnp.maximum(v[i], pltpu.roll(v[P_TOPK - 1 - i], shift, 0)) for i in range(P_TOPK)]
        _bitonic_merge16_desc(v)
    return v


_ROW_LEN = tuple(P_TOPK // (a + 1) for a in range(P_TOPK))


def _pair_counts(p1, p2):
    rows = [[p1[a] + p2[b] for b in range(_ROW_LEN[a])] for a in range(P_TOPK)]
    ninf = jnp.full_like(p1[0], NEG_INF)
    g0 = list(rows[0])
    g1 = rows[1] + [rows[a][0] for a in range(P_TOPK - 1, 7, -1)]
    _bitonic_merge16_desc(g1)
    g2 = rows[2] + rows[3] + rows[4] + rows[5] + rows[6]
    _sort16_desc(g2)
    g3 = rows[7] + [ninf] * (P_TOPK - len(rows[7]))
    tau = _merge_top16(_merge_top16(g0, g1), _merge_top16(g2, g3))[P_TOPK - 1]
    top = rows[0][0]
    counts = []
    z = jnp.zeros_like(top)
    total = jnp.zeros_like(top)
    for a in range(P_TOPK):
        n = jnp.zeros_like(top)
        for cand in rows[a]:
            ge = cand >= tau
            n = n + jnp.where(ge, 1.0, 0.0)
            z = z + jnp.where(ge, jnp.exp(cand - top), 0.0)
        counts.append(n)
        total = total + n
    return counts, z, jnp.where(total != float(P_TOPK), 1.0, 0.0)


def _retrieve_tie_free(s1_chunks, s2_chunks):
    n_chunks = len(s1_chunks)
    assert n_chunks <= 8
    sub = lax.broadcasted_iota(jnp.int32, (8, LANES), 0)
    blocks = [slice(8 * k, 8 * k + 8) for k in range(N_KEYS // 8)]
    add = lambda p, q: p + q
    rank_sum_distinct = float(sum(range(P_TOPK)) + P_TOPK * (N_KEYS - P_TOPK))

    tops = []
    rank2 = []
    ties = []
    p1 = p2 = None
    for c in range(n_chunks):
        s1, s2 = s1_chunks[c], s2_chunks[c]
        v1 = _sorted_top16(s1)
        v2 = _sorted_top16(s2)
        tops.append((v1, v2))
        tie = jnp.zeros((8, LANES), F32)
        ranks = []
        for blk in blocks:
            r = jnp.zeros((8, LANES), F32)
            for a in range(P_TOPK):
                r = jnp.where(v2[a] > s2[blk], float(a + 1), r)
            ranks.append(r)
        rank2.append(jnp.concatenate(ranks, axis=0))
        rank_sum = jnp.sum(functools.reduce(add, ranks), axis=0, keepdims=True)
        tie = jnp.where(rank_sum != rank_sum_distinct, 1.0, tie)
        for a in range(P_TOPK - 1):
            tie = jnp.where(v1[a] == v1[a + 1], 1.0, tie)
        n_ge = functools.reduce(add, [jnp.where(s1[blk] >= v1[P_TOPK - 1], 1.0, 0.0) for blk in blocks])
        ties.append(jnp.where(jnp.sum(n_ge, axis=0, keepdims=True) != float(P_TOPK), 1.0, tie))
        if c == 0:
            p1, p2 = list(v1), list(v2)
        else:
            p1 = [jnp.where(sub == c, v, p) for v, p in zip(v1, p1)]
            p2 = [jnp.where(sub == c, v, p) for v, p in zip(v2, p2)]

    counts, z, pair_tie = _pair_counts(p1, p2)
    ties = [jnp.maximum(t, pair_tie[c:c + 1, :]) for c, t in enumerate(ties)]

    tables = []
    for c in range(n_chunks):
        s1, s2 = s1_chunks[c], s2_chunks[c]
        v1, v2 = tops[c]
        cnt = [jnp.broadcast_to(n[c:c + 1, :], (8, LANES)) for n in counts] + [jnp.zeros((8, LANES), F32)]
        npos = []
        for blk in blocks:
            n = cnt[0]
            for a in range(P_TOPK):
                n = jnp.where(v1[a] > s1[blk], cnt[a + 1], n)
            npos.append(n)
        tables.append((jnp.concatenate(npos, axis=0), jnp.exp(s1 - v1[0][0:1]) / z[c:c + 1, :],
                       rank2[c], jnp.exp(s2 - v2[0][0:1])))
    return tables, ties


def _dup_bf16_word(x):
    u = lax.bitcast_convert_type(x.astype(BF16).astype(F32), jnp.uint32)
    return lax.bitcast_convert_type(u | (u >> 16), jnp.int32)


def _peer_topk_kernel(x_ref, m_ref, wh_ref, wl_ref, kh_ref, kl_ref,
                      h2t_ref, ni_ref, ai_ref, rj_ref, bj_ref, qt_sc):
    hd = pl.program_id(1)

    @pl.when(hd == 0)
    def _():
        x = x_ref[...]
        h2 = x * (1.0 + m_ref[0, 4:5, :]) + m_ref[0, 3:4, :]
        h_hi, h_lo = _split(h2.T)
        h2t_ref[...] = h_hi
        qt_sc[...] = _mm3(wh_ref[...], wl_ref[...], h_hi, h_lo)

    qh = qt_sc[pl.ds(pl.multiple_of(hd * (2 * P_HALF), 2 * P_HALF), 2 * P_HALF), :]
    q1_hi, q1_lo = _split(qh[:P_HALF])
    q2_hi, q2_lo = _split(qh[P_HALF:])
    sc1 = _mm3(kh_ref[0], kl_ref[0], q1_hi, q1_lo)
    sc2 = _mm3(kh_ref[1], kl_ref[1], q2_hi, q2_lo)
    n_chunks = sc1.shape[1] // LANES
    chunk = [slice(c * LANES, (c + 1) * LANES) for c in range(n_chunks)]

    def emit(c, tables):
        npos, a_i, r2, b_j = tables
        ni_ref[0, :, chunk[c]] = _dup_bf16_word(npos)
        ai_ref[0, :, chunk[c]] = _dup_bf16_word(0.5 * a_i)
        rj_ref[0, :, chunk[c]] = r2.astype(BF16)
        bj_ref[0, :, chunk[c]] = b_j.astype(BF16)

    tables, ties = _retrieve_tie_free([sc1[:, ls] for ls in chunk], [sc2[:, ls] for ls in chunk])
    for c in range(n_chunks):
        emit(c, tables[c])

    def redo(c):
        emit(c, _retrieve_exact(sc1[:, chunk[c]], sc2[:, chunk[c]]))

    for c in range(n_chunks):
        pl.when(jnp.max(ties[c]) > 0.0)(functools.partial(redo, c))


def _peer_topk(x, mods_l, wq_t_hi, wq_t_lo, keys_hi, keys_lo, *, t_p, s_len):
    t, d = x.shape
    tm = TM_TOPK
    n_p = t_p // tm
    tpb = s_len // tm
    row_tab = jax.ShapeDtypeStruct((P_HEADS, N_KEYS, t), jnp.int32)
    row_spec = pl.BlockSpec((1, N_KEYS, tm), lambda i, h: (h, 0, i))
    col_tab = jax.ShapeDtypeStruct((P_HEADS, N_KEYS, t), BF16)
    col_spec = row_spec
    return pl.pallas_call(
        _peer_topk_kernel,
        grid=(t // tm, P_HEADS),
        in_specs=[
            pl.BlockSpec((tm, d), lambda i, h: (i, 0)),
            pl.BlockSpec((1, 6, d), lambda i, h: (_tile_group(i, n_p, tpb), 0, 0)),
            pl.BlockSpec((d, d), lambda i, h: (0, 0)),
            pl.BlockSpec((d, d), lambda i, h: (0, 0)),
            pl.BlockSpec((2, N_KEYS, P_HALF), lambda i, h: (h, 0, 0)),
            pl.BlockSpec((2, N_KEYS, P_HALF), lambda i, h: (h, 0, 0)),
        ],
        out_specs=[pl.BlockSpec((d, tm), lambda i, h: (0, i)), row_spec, row_spec, col_spec, col_spec],
        out_shape=[jax.ShapeDtypeStruct((d, t), BF16), row_tab, row_tab, col_tab, col_tab],
        scratch_shapes=[pltpu.VMEM((d, tm), F32)],
        compiler_params=_params(("parallel", "arbitrary")),
    )(x, mods_l, wq_t_hi, wq_t_lo, keys_hi, keys_lo)


def _row_tile(row):
    tile = pltpu.bitcast(jnp.broadcast_to(row, (8, row.shape[1])), BF16)
    return jnp.concatenate([tile] * (N_KEYS // tile.shape[0]), axis=0)


GELU_C1 = math.sqrt(2.0 / math.pi)
GELU_C2 = 0.044715 * GELU_C1
K_CHUNK = 512


def _peer_dense_kernel(h2t0_ref, h2tn_ref, eu0_ref, eun_ref, evt_ref, ni_ref, ai_ref, rj_ref, bj_ref,
                       x_ref, m_ref, g_ref, b_ref, y_ref, acc_sc, act0_sc, act1_sc, wt_sc, *, n_e):
    e = pl.program_id(1)

    @pl.when((pl.program_id(0) == 0) & (e == 0))
    def _():
        act0_sc[...] = _mm(eu0_ref[...], h2t0_ref[...]).astype(BF16)

    @pl.when(e == 0)
    def _():
        acc_sc[...] = jnp.zeros_like(acc_sc)

    te, tm = wt_sc.shape
    zero = jnp.zeros((), BF16)

    def after(words, product):
        bits = lax.bitcast_convert_type(product[0:1, :], jnp.int32)
        return words + lax.shift_right_logical(lax.shift_right_logical(bits, 16), 16)

    def step(act_ref, next_ref):
        part = None
        nxt_prev = None
        for kc in range(te // K_CHUNK):
            rows = slice(kc * K_CHUNK, (kc + 1) * K_CHUNK)
            nxt = _mm(eun_ref[rows, :], h2tn_ref[...])
            next_ref[rows, :] = nxt.astype(BF16)
            for il in range(kc * K_CHUNK // N_KEYS, (kc + 1) * K_CHUNK // N_KEYS):
                rs = slice(il * N_KEYS, (il + 1) * N_KEYS)
                gate = None
                for h in range(P_HEADS):
                    cnt_words = ni_ref[h, il:il + 1, :]
                    if nxt_prev is not None and h == 0 and il * N_KEYS == kc * K_CHUNK:
                        cnt_words = after(cnt_words, nxt_prev)
                    cnt = _row_tile(cnt_words)
                    half_a = _row_tile(ai_ref[h, il:il + 1, :])
                    gh = jnp.where(rj_ref[h] < cnt, bj_ref[h], zero) * half_a
                    gate = gh if gate is None else gate + gh
                a = act_ref[rs, :]
                t = jnp.tanh(a * (GELU_C1 + GELU_C2 * (a * a)))
                wt_sc[rs, :] = gate * (a + a * t)
            p = _mm(evt_ref[:, rows], wt_sc[rows, :])
            part = p if part is None else part + p
            nxt_prev = nxt
        acc_sc[...] += part

    @pl.when(e % 2 == 0)
    def _():
        step(act0_sc, act1_sc)

    @pl.when(e % 2 == 1)
    def _():
        step(act1_sc, act0_sc)

    @pl.when(e == n_e - 1)
    def _():
        f = acc_sc[...].T
        z = DN_ALPHA * x_ref[...] + m_ref[0, 5:6, :] * f
        y_ref[...] = _layer_norm(z, g_ref[...], b_ref[...])


def _peer_dense(h2t, e_u, e_vt, layer, ni, ai, rj, bj, x, mods_l, ln_g, ln_b, *, t_p, s_len):
    t, d = x.shape
    tm = TM_PEER
    te = TE_PEER
    n_p = t_p // tm
    tpb = s_len // tm
    n_exp = e_u.shape[1]
    n_e = n_exp // te
    n_i = te // N_KEYS
    n_t = t // tm
    assert n_e % 2 == 0
    nxt_tok = lambda i, e: jnp.minimum(i + (e + 1) // n_e, n_t - 1)
    return pl.pallas_call(
        functools.partial(_peer_dense_kernel, n_e=n_e),
        grid=(n_t, n_e),
        in_specs=[
            pl.BlockSpec((d, tm), lambda i, e: (0, 0)),
            pl.BlockSpec((d, tm), lambda i, e: (0, nxt_tok(i, e))),
            pl.BlockSpec((None, te, d), lambda i, e: (layer, 0, 0)),
            pl.BlockSpec((None, te, d), lambda i, e: (layer, (e + 1) % n_e, 0)),
            pl.BlockSpec((None, d, te), lambda i, e: (layer, 0, e)),
            pl.BlockSpec((P_HEADS, n_i, tm), lambda i, e: (0, e, i)),
            pl.BlockSpec((P_HEADS, n_i, tm), lambda i, e: (0, e, i)),
            pl.BlockSpec((P_HEADS, N_KEYS, tm), lambda i, e: (0, 0, i)),
            pl.BlockSpec((P_HEADS, N_KEYS, tm), lambda i, e: (0, 0, i)),
            pl.BlockSpec((tm, d), lambda i, e: (i, 0)),
            pl.BlockSpec((1, 6, d), lambda i, e: (_tile_group(i, n_p, tpb), 0, 0)),
            pl.BlockSpec((1, d), lambda i, e: (0, 0)),
            pl.BlockSpec((1, d), lambda i, e: (0, 0)),
        ],
        out_specs=pl.BlockSpec((tm, d), lambda i, e: (i, 0)),
        out_shape=jax.ShapeDtypeStruct((t, d), F32),
        scratch_shapes=[pltpu.VMEM((d, tm), F32), pltpu.VMEM((te, tm), BF16), pltpu.VMEM((te, tm), BF16),
                        pltpu.VMEM((te, tm), BF16)],
        compiler_params=_params(("arbitrary", "arbitrary")),
    )(h2t, h2t, e_u, e_u, e_vt, ni, ai, rj, bj, x, mods_l, ln_g.reshape(1, d), ln_b.reshape(1, d))


def _rope_tables(n_tok):
    rows = n_tok // GRID_W
    row = jnp.repeat(jnp.arange(rows), GRID_W).astype(F32)
    col = jnp.tile(jnp.arange(GRID_W), rows).astype(F32)
    n_freq = A_HD // 4
    inv = ROPE_THETA ** (-jnp.arange(n_freq, dtype=F32) / n_freq)
    ang = jnp.concatenate([row[:, None] * inv, col[:, None] * inv], -1)
    cos = jnp.repeat(jnp.cos(ang), 2, axis=-1)
    sin = jnp.repeat(jnp.sin(ang), 2, axis=-1)
    sign = jnp.tile(jnp.asarray([-1.0, 1.0], F32), A_HD // 2)
    reps = LANES // A_HD
    return jnp.tile(cos, (1, reps)), jnp.tile(sin * sign, (1, reps))


def _gqa_weights(w_qkv, q_norm, k_norm, w_o):
    d = w_qkv.shape[0]
    nq = A_HEADS * A_HD
    half = (np.arange(A_HEADS) // A_REP) % 2
    sel = np.stack([half == 0, half == 1], axis=1).astype(np.float32)
    wq = w_qkv[:, :nq].reshape(d, A_HEADS, 1, A_HD) * sel[None, :, :, None]
    w_exp = jnp.concatenate([wq.reshape(d, A_HEADS * LANES), w_qkv[:, nq:]], axis=1).astype(BF16)
    gq = jnp.tile(q_norm, 2 * A_HEADS).reshape(1, -1)
    gk = jnp.tile(k_norm, A_KV).reshape(1, -1)
    wo = w_o.reshape(A_HEADS, 1, A_HD, d) * sel[:, :, None, None]
    return w_exp, gq, gk, wo.reshape(A_HEADS * LANES, d).astype(BF16)


def _diff_weights(w_qkv):
    d = w_qkv.shape[0]
    eye = np.eye(2, dtype=np.float32)
    wq = w_qkv[:, :D_W].reshape(d, D_HEADS, 1, 2, D_HD) * eye[None, None, :, :, None]
    return jnp.concatenate([wq.reshape(d, D_HEADS * 2 * LANES), w_qkv[:, D_W:]], axis=1).astype(BF16)


def _split_f32(w):
    hi = w.astype(BF16)
    return hi, (w - hi.astype(F32)).astype(BF16)


def kernel(x_prompt, x_sample, cache_a_k, cache_a_v, cache_d_k, cache_d_v, c, c_ctx, w_mod, b_mod, ln_g, ln_b, a_w_qkv, a_q_norm, a_k_norm, a_w_o, d_w_qkv, d_lambda_q1, d_lambda_k1, d_lambda_q2, d_lambda_k2, d_sub_norm, d_w_o, g_w_in, g_b_in, g_v_norm_g, g_v_norm_b, g_w_s, g_b_s, g_w_out, p_w_q, p_sub_keys, p_expert_u, p_expert_v):
    batch, seq, d = x_prompt.shape
    dec_batch, dec_seq, _ = x_sample.shape
    past = cache_a_k.shape[2]
    t_p = batch * seq
    t_s = dec_batch * dec_seq
    dims = dict(t_p=t_p, s_len=dec_seq)

    x = jnp.concatenate([x_prompt.reshape(t_p, d), x_sample.reshape(t_s, d)], axis=0)
    e_u_all = p_expert_u.astype(BF16)
    e_vt_all = jnp.swapaxes(p_expert_v, 1, 2).astype(BF16)
    n_cond = -(-(1 + dec_batch) // 8) * 8
    cond = jnp.zeros((n_cond, d), F32).at[0].set(c_ctx).at[1:1 + dec_batch].set(c)
    mods = _modulation(cond, w_mod, b_mod).reshape(DEPTH, n_cond, 6, d)
    rope_c, rope_s = _rope_tables(dec_seq)

    ak, av, dk, dv = [], [], [], []
    for i in range(DEPTH):
        kind, j = i % N_MIXERS, i // N_MIXERS
        mods_l = mods[i]
        if kind == 0:
            w_exp, gq, gk, wo = _gqa_weights(a_w_qkv[j], a_q_norm[j], a_k_norm[j], a_w_o[j])
            q, k_att, v_att, k_f, v_f = _qkv_project(
                x, mods_l, w_exp, gq, gk, rope_c, rope_s, nq=A_HEADS, nk=A_KV * A_HD // LANES,
                nv=A_KV * A_HD // LANES, use_norm=True, q_scale=A_HD ** -0.5, **dims)
            ctx_k = cache_a_k[:, j].reshape(dec_batch, past, A_KV * A_HD).astype(BF16)
            ctx_v = cache_a_v[:, j].reshape(dec_batch, past, A_KV * A_HD).astype(BF16)
            o = _attention(_gqa_attn_kernel, (), q, k_att, v_att, ctx_k, ctx_v, n_out=A_HEADS * LANES,
                           t_p=t_p, seq_p=seq, batch_s=dec_batch, seq_s=dec_seq)
            ak.append(k_f[:t_p].reshape(batch, seq, A_KV, A_HD))
            av.append(v_f[:t_p].reshape(batch, seq, A_KV, A_HD))
        elif kind == 1:
            lam_init = 0.8 - 0.6 * math.exp(-0.3 * i)
            w_exp = _diff_weights(d_w_qkv[j])
            q, k_att, v_att, k_f, v_f = _qkv_project(
                x, mods_l, w_exp, jnp.ones((1, 2 * D_W), F32), jnp.ones((1, D_W), F32),
                rope_c, rope_s, nq=2 * D_HEADS, nk=D_HEADS, nv=D_HEADS,
                use_norm=False, q_scale=D_HD ** -0.5, **dims)
            ctx_k = cache_d_k[:, j].reshape(dec_batch, past, D_W).astype(BF16)
            ctx_v = cache_d_v[:, j].reshape(dec_batch, past, D_HEADS * D_VD).astype(BF16)
            lvec = jnp.stack([d_lambda_q1[j], d_lambda_k1[j], d_lambda_q2[j], d_lambda_k2[j]])
            kern = functools.partial(_diff_attn_kernel, lam_init=lam_init)
            o = _attention(kern, (lvec, d_sub_norm[j].reshape(1, D_VD)), q, k_att, v_att, ctx_k, ctx_v,
                           n_out=D_HEADS * D_VD, t_p=t_p, seq_p=seq, batch_s=dec_batch, seq_s=dec_seq)
            wo = d_w_o[j].astype(BF16)
            dk.append(k_f[:t_p].reshape(batch, seq, D_HEADS, 2, D_HD))
            dv.append(v_f[:t_p].reshape(batch, seq, D_HEADS, D_VD))
        if kind == 2:
            x = _gmlp(x, mods_l, g_w_in[j], g_b_in[j], g_v_norm_g[j], g_v_norm_b[j], g_w_s[j], g_b_s[j],
                      g_w_out[j], ln_g[i, 0], ln_b[i, 0], **dims)
        else:
            x = _proj_ln(*o, wo, x, mods_l, ln_g[i, 0], ln_b[i, 0], gate_idx=2, **dims)

        wq_hi, wq_lo = _split_f32(p_w_q[i].T)
        keys = p_sub_keys[i].reshape(P_HEADS * 2, N_KEYS, P_HALF)
        k_hi, k_lo = _split_f32(keys)
        h2t, ni, ai, rj, bj = _peer_topk(x, mods_l, wq_hi, wq_lo, k_hi, k_lo, **dims)
        x = _peer_dense(h2t, e_u_all, e_vt_all, i, ni, ai, rj, bj, x, mods_l, ln_g[i, 1], ln_b[i, 1], **dims)

    y_prompt = x[:t_p].reshape(batch, seq, d)
    y_sample = x[t_p:].reshape(dec_batch, dec_seq, d)
    return (y_prompt, y_sample, jnp.stack(ak, axis=1), jnp.stack(av, axis=1),
            jnp.stack(dk, axis=1), jnp.stack(dv, axis=1))
```

```python
import functools
import math

import numpy as np
import jax
import jax.numpy as jnp
from jax import lax
from jax.experimental import pallas as pl
from jax.experimental.pallas import tpu as pltpu

F32 = jnp.float32
BF16 = jnp.bfloat16

D_MODEL = 1024
DEPTH = 4
GRID_W = 64
N_MIXERS = 3
A_HEADS = 16
A_KV = 4
A_REP = A_HEADS // A_KV
A_HD = 64
D_HEADS = 8
D_HD = 64
D_VD = 2 * D_HD
D_W = D_HEADS * 2 * D_HD
G_DIM = 1024
G_GROUPS = 8
G_CHUNK = 128
P_HEADS = 8
N_KEYS = 128
P_HALF = 64
P_TOPK = 16
ROPE_THETA = 10000.0
DN_ALPHA = (2 * DEPTH) ** 0.25
EPS = 1e-6

LANES = 128
NEG_INF = float("-inf")
VMEM_LIMIT = 56 * 1024 * 1024

TM_QKV = 256
TM_PROJ = 512
TQ_ATTN = 256
TM_GMLP = 256
TM_TOPK = 1024
REDO_GROUP = 4
TM_PEER = 512
TE_PEER = 2048


def _mm(a, b):
    return jnp.dot(a, b, preferred_element_type=F32)


def _mm_nt(a, b):
    return lax.dot_general(a, b, (((1,), (1,)), ((), ())), preferred_element_type=F32)


def _split(x):
    hi = x.astype(BF16)
    lo = (x - hi.astype(F32)).astype(BF16)
    return hi, lo


def _mm3(a_hi, a_lo, b_hi, b_lo):
    return _mm(a_hi, b_hi) + _mm(a_hi, b_lo) + _mm(a_lo, b_hi)


def _layer_norm(z, g, b):
    mu = jnp.mean(z, axis=-1, keepdims=True)
    zc = z - mu
    var = jnp.mean(zc * zc, axis=-1, keepdims=True)
    return zc * lax.rsqrt(var + EPS) * g + b


def _params(sem, flags=None):
    return pltpu.CompilerParams(dimension_semantics=sem, vmem_limit_bytes=VMEM_LIMIT, flags=flags)


def _tile_group(i, n_p_tiles, tiles_per_batch):
    return jnp.where(i < n_p_tiles, 0, 1 + (i - n_p_tiles) // tiles_per_batch)


def _tile_pos_block(i, n_p_tiles, tiles_per_batch):
    return jnp.where(i < n_p_tiles, 0, (i - n_p_tiles) % tiles_per_batch)


def _mods_kernel(c_ref, w_ref, b_ref, o_ref):
    c = c_ref[...]
    a = c * jax.nn.sigmoid(c)
    a_hi, a_lo = _split(a)
    w_hi, w_lo = _split(w_ref[0])
    o_ref[0] = _mm3(a_hi, a_lo, w_hi, w_lo) + b_ref[0]


def _modulation(cond, w_mod, b_mod):
    depth, d, n = w_mod.shape
    rows = cond.shape[0]
    tn = 1536
    return pl.pallas_call(
        _mods_kernel,
        grid=(depth, n // tn),
        in_specs=[
            pl.BlockSpec((rows, d), lambda l, j: (0, 0)),
            pl.BlockSpec((1, d, tn), lambda l, j: (l, 0, j)),
            pl.BlockSpec((1, 1, tn), lambda l, j: (l, 0, j)),
        ],
        out_specs=pl.BlockSpec((1, rows, tn), lambda l, j: (l, 0, j)),
        out_shape=jax.ShapeDtypeStruct((depth, rows, n), F32),
        compiler_params=_params(("parallel", "parallel")),
    )(cond, w_mod, b_mod.reshape(depth, 1, n))


def _swap_pairs(x):
    lane = lax.broadcasted_iota(jnp.int32, x.shape, 1)
    nxt = pltpu.roll(x, LANES - 1, 1)
    prv = pltpu.roll(x, 1, 1)
    return jnp.where((lane & 1) == 0, nxt, prv)


def _qkv_kernel(x_ref, m_ref, w_ref, bd_ref, gq_ref, gk_ref, cs_ref, sn_ref,
                q_ref, ka_ref, va_ref, kf_ref, vf_ref,
                *, nq, nk, nv, use_norm, n_p_tiles, q_scale):
    x = x_ref[...]
    h = x * (1.0 + m_ref[0, 1:2, :]) + m_ref[0, 0:1, :]
    y = _mm(h.astype(BF16), w_ref[...])
    is_lat = pl.program_id(0) >= n_p_tiles
    cs = jnp.where(is_lat, cs_ref[...], 1.0)
    sn = jnp.where(is_lat, sn_ref[...], 0.0)
    bd = bd_ref[...]

    def head_norm(yg, gain):
        s_hi, s_lo = _split(yg * yg)
        ms = _mm(s_hi, bd) + _mm(s_lo, bd)
        return yg * lax.rsqrt(ms + EPS) * gain

    def rope(yg):
        return yg * cs + _swap_pairs(yg) * sn

    for g in range(nq):
        sl = slice(g * LANES, (g + 1) * LANES)
        yg = y[:, sl]
        if use_norm:
            yg = head_norm(yg, gq_ref[:, sl])
        q_ref[:, sl] = (rope(yg) * q_scale).astype(BF16)
    for g in range(nk):
        sl = slice(g * LANES, (g + 1) * LANES)
        yg = y[:, (nq + g) * LANES:(nq + g + 1) * LANES]
        if use_norm:
            yg = head_norm(yg, gk_ref[:, sl])
        kf_ref[:, sl] = yg
        ka_ref[:, sl] = rope(yg).astype(BF16)
    for g in range(nv):
        sl = slice(g * LANES, (g + 1) * LANES)
        yg = y[:, (nq + nk + g) * LANES:(nq + nk + g + 1) * LANES]
        vf_ref[:, sl] = yg
        va_ref[:, sl] = yg.astype(BF16)


def _qkv_project(x, mods_l, w_exp, gq, gk, rope_c, rope_s, *, nq, nk, nv, use_norm, t_p, s_len, q_scale):
    t, d = x.shape
    tm = TM_QKV
    n_p = t_p // tm
    tpb = s_len // tm
    ncol = (nq + nk + nv) * LANES
    bd = np.kron(np.eye(2, dtype=np.float32), np.full((64, 64), 1.0 / 64.0, np.float32))
    bd = jnp.asarray(bd, BF16)
    grp = lambda i: (_tile_group(i, n_p, tpb), 0, 0)
    pos = lambda i: (_tile_pos_block(i, n_p, tpb), 0)
    row = lambda i: (i, 0)
    fixed = lambda i: (0, 0)
    kern = functools.partial(_qkv_kernel, nq=nq, nk=nk, nv=nv, use_norm=use_norm,
                             n_p_tiles=n_p, q_scale=q_scale)
    return pl.pallas_call(
        kern,
        grid=(t // tm,),
        in_specs=[
            pl.BlockSpec((tm, d), row),
            pl.BlockSpec((1, 6, d), grp),
            pl.BlockSpec((d, ncol), fixed),
            pl.BlockSpec((LANES, LANES), fixed),
            pl.BlockSpec((1, nq * LANES), fixed),
            pl.BlockSpec((1, nk * LANES), fixed),
            pl.BlockSpec((tm, LANES), pos),
            pl.BlockSpec((tm, LANES), pos),
        ],
        out_specs=[
            pl.BlockSpec((tm, nq * LANES), row),
            pl.BlockSpec((tm, nk * LANES), row),
            pl.BlockSpec((tm, nv * LANES), row),
            pl.BlockSpec((tm, nk * LANES), row),
            pl.BlockSpec((tm, nv * LANES), row),
        ],
        out_shape=[
            jax.ShapeDtypeStruct((t, nq * LANES), BF16),
            jax.ShapeDtypeStruct((t, nk * LANES), BF16),
            jax.ShapeDtypeStruct((t, nv * LANES), BF16),
            jax.ShapeDtypeStruct((t, nk * LANES), F32),
            jax.ShapeDtypeStruct((t, nv * LANES), F32),
        ],
        compiler_params=_params(("parallel",)),
    )(x, mods_l, w_exp, bd, gq, gk, rope_c, rope_s)


def _softmax_pv(qh, k_all, v_all):
    scores = [_mm_nt(qh, k) for k in k_all]
    m = scores[0].max(axis=-1, keepdims=True)
    for s in scores[1:]:
        m = jnp.maximum(m, s.max(axis=-1, keepdims=True))
    o = None
    l = None
    for s, v in zip(scores, v_all):
        p = jnp.exp(s - m)
        ls = p.sum(axis=-1, keepdims=True)
        os_ = _mm(p.astype(BF16), v)
        o = os_ if o is None else o + os_
        l = ls if l is None else l + ls
    return o, l


def _gqa_attn_kernel(*refs, has_ctx):
    if has_ctx:
        q_ref, k_ref, v_ref, kc_ref, vc_ref, o_ref = refs
    else:
        q_ref, k_ref, v_ref, o_ref = refs
    tq = q_ref.shape[0]
    for g in range(A_KV):
        ks = slice((g // 2) * LANES, (g // 2 + 1) * LANES)
        heads = [slice((g * A_REP + r) * LANES, (g * A_REP + r + 1) * LANES) for r in range(A_REP)]
        k_all = [k_ref[:, ks]]
        v_all = [v_ref[:, ks]]
        if has_ctx:
            k_all.append(kc_ref[0, :, ks])
            v_all.append(vc_ref[0, :, ks])
        o, l = _softmax_pv(jnp.concatenate([q_ref[:, sl] for sl in heads], axis=0), k_all, v_all)
        o = (o / l).astype(BF16)
        for r, sl in enumerate(heads):
            o_ref[:, sl] = o[r * tq:(r + 1) * tq, :]


def _diff_attn_kernel(*refs, has_ctx, lam_init):
    if has_ctx:
        lv_ref, sg_ref, q_ref, k_ref, v_ref, kc_ref, vc_ref, o_ref = refs
    else:
        lv_ref, sg_ref, q_ref, k_ref, v_ref, o_ref = refs
    lv = lv_ref[...]
    lam = (jnp.exp(jnp.sum(lv[0:1] * lv[1:2], axis=-1, keepdims=True))
           - jnp.exp(jnp.sum(lv[2:3] * lv[3:4], axis=-1, keepdims=True)) + lam_init)
    for h in range(D_HEADS):
        ks = slice(h * LANES, (h + 1) * LANES)
        k_all = [k_ref[:, ks]]
        v_all = [v_ref[:, ks]]
        if has_ctx:
            k_all.append(kc_ref[0, :, ks])
            v_all.append(vc_ref[0, :, ks])
        tq = q_ref.shape[0]
        q2 = jnp.concatenate([q_ref[:, (2 * h + j) * LANES:(2 * h + j + 1) * LANES] for j in range(2)], axis=0)
        o, l = _softmax_pv(q2, k_all, v_all)
        o = o / l
        o = o[:tq] - lam * o[tq:]
        ms = jnp.mean(o * o, axis=-1, keepdims=True)
        o = o * lax.rsqrt(ms + EPS) * sg_ref[...] * (1.0 - lam_init)
        o_ref[:, ks] = o.astype(BF16)


def _attention(kern, extra, q, k, v, ctx_k, ctx_v, *, n_out, t_p, seq_p, batch_s, seq_s):
    nqc = q.shape[1]
    nkc = k.shape[1]
    extra_specs = [pl.BlockSpec(a.shape, lambda *_: (0, 0)) for a in extra]
    n_seq_p = t_p // seq_p
    o_p = pl.pallas_call(
        functools.partial(kern, has_ctx=False),
        grid=(n_seq_p,),
        in_specs=extra_specs + [
            pl.BlockSpec((seq_p, nqc), lambda b: (b, 0)),
            pl.BlockSpec((seq_p, nkc), lambda b: (b, 0)),
            pl.BlockSpec((seq_p, nkc), lambda b: (b, 0)),
        ],
        out_specs=pl.BlockSpec((seq_p, n_out), lambda b: (b, 0)),
        out_shape=jax.ShapeDtypeStruct((t_p, n_out), BF16),
        compiler_params=_params(("parallel",)),
    )(*extra, q, k, v)
    tq = TQ_ATTN
    nqt = seq_s // tq
    assert t_p % seq_s == 0 and t_p % tq == 0
    q_off = t_p // tq
    k_off = t_p // seq_s
    past = ctx_k.shape[1]
    o_s = pl.pallas_call(
        functools.partial(kern, has_ctx=True),
        grid=(batch_s, nqt),
        in_specs=extra_specs + [
            pl.BlockSpec((tq, nqc), lambda b, i: (q_off + b * nqt + i, 0)),
            pl.BlockSpec((seq_s, nkc), lambda b, i: (k_off + b, 0)),
            pl.BlockSpec((seq_s, nkc), lambda b, i: (k_off + b, 0)),
            pl.BlockSpec((1, past, nkc), lambda b, i: (b, 0, 0)),
            pl.BlockSpec((1, past, nkc), lambda b, i: (b, 0, 0)),
        ],
        out_specs=pl.BlockSpec((tq, n_out), lambda b, i: (b * nqt + i, 0)),
        out_shape=jax.ShapeDtypeStruct((batch_s * seq_s, n_out), BF16),
        compiler_params=_params(("parallel", "arbitrary")),
    )(*extra, q, k, v, ctx_k, ctx_v)
    return o_p, o_s


def _proj_ln_kernel(op_ref, os_ref, w_ref, x_ref, m_ref, g_ref, b_ref, y_ref, *, gate_idx, n_p_tiles):
    o = jnp.where(pl.program_id(0) < n_p_tiles, op_ref[...], os_ref[...])
    y = _mm(o, w_ref[...])
    z = DN_ALPHA * x_ref[...] + m_ref[0, gate_idx:gate_idx + 1, :] * y
    y_ref[...] = _layer_norm(z, g_ref[...], b_ref[...])


def _proj_ln(o_p, o_s, w, x, mods_l, ln_g, ln_b, *, gate_idx, t_p, s_len):
    t, d = x.shape
    tm = TM_PROJ
    n_p = t_p // tm
    tpb = s_len // tm
    kin = o_p.shape[1]
    row = lambda i: (i, 0)
    fixed = lambda i: (0, 0)
    return pl.pallas_call(
        functools.partial(_proj_ln_kernel, gate_idx=gate_idx, n_p_tiles=n_p),
        grid=(t // tm,),
        in_specs=[
            pl.BlockSpec((tm, kin), lambda i: (jnp.minimum(i, n_p - 1), 0)),
            pl.BlockSpec((tm, kin), lambda i: (jnp.maximum(i - n_p, 0), 0)),
            pl.BlockSpec((kin, d), fixed),
            pl.BlockSpec((tm, d), row),
            pl.BlockSpec((1, 6, d), lambda i: (_tile_group(i, n_p, tpb), 0, 0)),
            pl.BlockSpec((1, d), fixed),
            pl.BlockSpec((1, d), fixed),
        ],
        out_specs=pl.BlockSpec((tm, d), row),
        out_shape=jax.ShapeDtypeStruct((t, d), F32),
        compiler_params=_params(("parallel",)),
    )(o_p, o_s, w, x, mods_l, ln_g.reshape(1, d), ln_b.reshape(1, d))


def _gmlp_kernel(x_ref, m_ref, win_ref, bin_ref, vg_ref, vb_ref, ws_ref, bs_ref, wout_ref,
                 g_ref, b_ref, y_ref, uv_sc):
    x = x_ref[...]
    h = x * (1.0 + m_ref[0, 1:2, :]) + m_ref[0, 0:1, :]
    z = jax.nn.gelu(_mm(h.astype(BF16), win_ref[...]) + bin_ref[...])
    u = z[:, :G_DIM]
    v = _layer_norm(z[:, G_DIM:], vg_ref[...], vb_ref[...])
    tm = x.shape[0]
    for c in range(tm // G_CHUNK):
        rs = slice(c * G_CHUNK, (c + 1) * G_CHUNK)
        for g in range(G_GROUPS):
            cs = slice(g * LANES, (g + 1) * LANES)
            vm = _mm(ws_ref[g], v[rs, cs].astype(BF16)) + bs_ref[g]
            uv_sc[rs, cs] = (u[rs, cs] * vm).astype(BF16)
    y = _mm(uv_sc[...], wout_ref[...])
    zz = DN_ALPHA * x + m_ref[0, 2:3, :] * y
    y_ref[...] = _layer_norm(zz, g_ref[...], b_ref[...])


def _gmlp(x, mods_l, w_in, b_in, vg, vb, w_s, b_s, w_out, ln_g, ln_b, *, t_p, s_len):
    t, d = x.shape
    tm = TM_GMLP
    n_p = t_p // tm
    tpb = s_len // tm
    row = lambda i: (i, 0)
    fixed = lambda i: (0, 0)
    fixed3 = lambda i: (0, 0, 0)
    b_s_b = jnp.broadcast_to(b_s[:, :, None], (G_GROUPS, G_CHUNK, LANES))
    return pl.pallas_call(
        _gmlp_kernel,
        grid=(t // tm,),
        in_specs=[
            pl.BlockSpec((tm, d), row),
            pl.BlockSpec((1, 6, d), lambda i: (_tile_group(i, n_p, tpb), 0, 0)),
            pl.BlockSpec((d, 2 * G_DIM), fixed),
            pl.BlockSpec((1, 2 * G_DIM), fixed),
            pl.BlockSpec((1, G_DIM), fixed),
            pl.BlockSpec((1, G_DIM), fixed),
            pl.BlockSpec((G_GROUPS, G_CHUNK, G_CHUNK), fixed3),
            pl.BlockSpec((G_GROUPS, G_CHUNK, LANES), fixed3),
            pl.BlockSpec((G_DIM, d), fixed),
            pl.BlockSpec((1, d), fixed),
            pl.BlockSpec((1, d), fixed),
        ],
        out_specs=pl.BlockSpec((tm, d), row),
        out_shape=jax.ShapeDtypeStruct((t, d), F32),
        scratch_shapes=[pltpu.VMEM((tm, G_DIM), BF16)],
        compiler_params=_params(("parallel",)),
    )(x, mods_l, w_in.astype(BF16), b_in.reshape(1, -1), vg.reshape(1, -1), vb.reshape(1, -1),
      w_s.astype(BF16), b_s_b, w_out.astype(BF16), ln_g.reshape(1, d), ln_b.reshape(1, d))


def _top16(sc):
    n, L = sc.shape
    pos = lax.broadcasted_iota(jnp.int32, (n, L), 0).astype(F32)
    slot = lax.broadcasted_iota(jnp.int32, (P_TOPK, L), 0)

    def body(k, carry):
        cur, rank, vals = carry
        m = jnp.max(cur, axis=0, keepdims=True)
        idx = jnp.min(jnp.where(cur == m, pos, float(n)), axis=0, keepdims=True)
        hit = pos == idx
        cur = jnp.where(hit, NEG_INF, cur)
        rank = jnp.where(hit, lax.convert_element_type(k, F32), rank)
        vals = jnp.where(slot == k, m, vals)
        return cur, rank, vals

    init = (sc, jnp.full((n, L), 64.0, F32), jnp.zeros((P_TOPK, L), F32))
    _, rank, vals = lax.fori_loop(0, P_TOPK, body, init)
    return vals, rank


_CAND_BLOCKS = ((0, 0, 8), (0, 8, 8), (1, 0, 8), (2, 0, 5), (3, 0, 4), (4, 0, 3), (5, 0, 2), (6, 0, 2), (7, 0, 2))


def _pair_select(v1, v2):
    L = v1.shape[1]
    r8 = lax.broadcasted_iota(jnp.int32, (8, L), 0).astype(F32)
    blocks = []
    codes = []
    for a, b0, nb in _CAND_BLOCKS:
        c = v1[a:a + 1] + v2[b0:b0 + 8]
        if nb < 8:
            c = jnp.where(r8 < float(nb), c, NEG_INF)
        blocks.append(c)
        codes.append(r8 + float(a * P_TOPK + b0))
    blocks.append(v1[8:16] + v2[0:1])
    codes.append((r8 + 8.0) * float(P_TOPK))
    cand0 = jnp.concatenate(blocks, axis=0)
    code = jnp.concatenate(codes, axis=0)

    def body(k, carry):
        cur, sel = carry
        m = jnp.max(cur, axis=0, keepdims=True)
        idx = jnp.min(jnp.where(cur == m, code, 1e9), axis=0, keepdims=True)
        hit = code == idx
        return jnp.where(hit, NEG_INF, cur), jnp.where(hit, 1.0, sel)

    _, sel = lax.fori_loop(0, P_TOPK, body, (cand0, jnp.zeros_like(cand0)))
    top = v1[0:1] + v2[0:1]
    z = jnp.sum(jnp.where(sel > 0.0, jnp.exp(cand0 - top), 0.0), axis=0, keepdims=True)
    cnt_lo = jnp.zeros((8, L), F32)
    starts = (0, 16, 24, 32, 40, 48, 56, 64, 72)
    for a in range(8):
        n_a = jnp.sum(sel[starts[a]:starts[a + 1]], axis=0, keepdims=True)
        cnt_lo = jnp.where(r8 == float(a), n_a, cnt_lo)
    count = jnp.concatenate([cnt_lo, sel[72:80]], axis=0)
    return count, z


def _retrieve_exact(s1, s2):
    v1, r1 = _top16(s1)
    v2, r2 = _top16(s2)
    count, z = _pair_select(v1, v2)
    npos = jnp.zeros_like(r1)
    for a in range(P_TOPK):
        npos = jnp.where(r1 == float(a), count[a:a + 1], npos)
    return npos, jnp.exp(s1 - v1[0:1]) / z, r2, jnp.exp(s2 - v2[0:1])


def _oddeven_merge_sort_pairs(n):
    pairs = []
    p = 1
    while p < n:
        k = p
        while k >= 1:
            for j in range(k % p, n - k, 2 * k):
                for i in range(min(k, n - j - k)):
                    if (i + j) // (2 * p) == (i + j + k) // (2 * p):
                        pairs.append((i + j, i + j + k))
            k //= 2
        p *= 2
    return tuple(pairs)


_SORT16 = _oddeven_merge_sort_pairs(P_TOPK)


def _sort16_desc(v):
    for i, j in _SORT16:
        v[i], v[j] = jnp.maximum(v[i], v[j]), jnp.minimum(v[i], v[j])


def _bitonic_merge16_desc(v):
    dist = P_TOPK // 2
    while dist >= 1:
        for i in range(P_TOPK):
            if not i & dist:
                v[i], v[i + dist] = jnp.maximum(v[i], v[i + dist]), jnp.minimum(v[i], v[i + dist])
        dist //= 2


def _merge_top16(a, b):
    v = [jnp.maximum(a[i], b[P_TOPK - 1 - i]) for i in range(P_TOPK)]
    _bitonic_merge16_desc(v)
    return v


def _sorted_top16(s):
    v = [s[8 * k:8 * k + 8, :] for k in range(N_KEYS // 8)]
    _sort16_desc(v)
    for shift in (4, 2, 1):
        v = [jnp.maximum(v[i], pltpu.roll(v[P_TOPK - 1 - i], shift, 0)) for i in range(P_TOPK)]
        _bitonic_merge16_desc(v)
    return v


_ROW_LEN = tuple(P_TOPK // (a + 1) for a in range(P_TOPK))


def _pair_counts(p1, p2):
    rows = [[p1[a] + p2[b] for b in range(_ROW_LEN[a])] for a in range(P_TOPK)]
    ninf = jnp.full_like(p1[0], NEG_INF)
    g0 = list(rows[0])
    g1 = rows[1] + [rows[a][0] for a in range(P_TOPK - 1, 7, -1)]
    _bitonic_merge16_desc(g1)
    g2 = rows[2] + rows[3] + rows[4] + rows[5] + rows[6]
    _sort16_desc(g2)
    g3 = rows[7] + [ninf] * (P_TOPK - len(rows[7]))
    tau = _merge_top16(_merge_top16(g0, g1), _merge_top16(g2, g3))[P_TOPK - 1]
    top = rows[0][0]
    counts = []
    z = jnp.zeros_like(top)
    total = jnp.zeros_like(top)
    for a in range(P_TOPK):
        n = jnp.zeros_like(top)
        for cand in rows[a]:
            ge = cand >= tau
            n = n + jnp.where(ge, 1.0, 0.0)
            z = z + jnp.where(ge, jnp.exp(cand - top), 0.0)
        counts.append(n)
        total = total + n
    return counts, z, jnp.where(total != float(P_TOPK), 1.0, 0.0)


def _retrieve_tie_free(s1_chunks, s2_chunks):
    n_chunks = len(s1_chunks)
    assert n_chunks <= 8
    sub = lax.broadcasted_iota(jnp.int32, (8, LANES), 0)
    blocks = [slice(8 * k, 8 * k + 8) for k in range(N_KEYS // 8)]
    add = lambda p, q: p + q
    rank_sum_distinct = float(sum(range(P_TOPK)) + P_TOPK * (N_KEYS - P_TOPK))

    tops = []
    rank2 = []
    ties = []
    p1 = p2 = None
    for c in range(n_chunks):
        s1, s2 = s1_chunks[c], s2_chunks[c]
        v1 = _sorted_top16(s1)
        v2 = _sorted_top16(s2)
        tops.append((v1, v2))
        tie = jnp.zeros((8, LANES), F32)
        ranks = []
        for blk in blocks:
            r = jnp.zeros((8, LANES), F32)
            for a in range(P_TOPK):
                r = jnp.where(v2[a] > s2[blk], float(a + 1), r)
            ranks.append(r)
        rank2.append(jnp.concatenate(ranks, axis=0))
        rank_sum = jnp.sum(functools.reduce(add, ranks), axis=0, keepdims=True)
        tie = jnp.where(rank_sum != rank_sum_distinct, 1.0, tie)
        for a in range(P_TOPK - 1):
            tie = jnp.where(v1[a] == v1[a + 1], 1.0, tie)
        n_ge = functools.reduce(add, [jnp.where(s1[blk] >= v1[P_TOPK - 1], 1.0, 0.0) for blk in blocks])
        ties.append(jnp.where(jnp.sum(n_ge, axis=0, keepdims=True) != float(P_TOPK), 1.0, tie))
        if c == 0:
            p1, p2 = list(v1), list(v2)
        else:
            p1 = [jnp.where(sub == c, v, p) for v, p in zip(v1, p1)]
            p2 = [jnp.where(sub == c, v, p) for v, p in zip(v2, p2)]

    counts, z, pair_tie = _pair_counts(p1, p2)
    ties = [jnp.maximum(t, pair_tie[c:c + 1, :]) for c, t in enumerate(ties)]

    tables = []
    for c in range(n_chunks):
        s1, s2 = s1_chunks[c], s2_chunks[c]
        v1, v2 = tops[c]
        cnt = [jnp.broadcast_to(n[c:c + 1, :], (8, LANES)) for n in counts] + [jnp.zeros((8, LANES), F32)]
        npos = []
        for blk in blocks:
            n = cnt[0]
            for a in range(P_TOPK):
                n = jnp.where(v1[a] > s1[blk], cnt[a + 1], n)
            npos.append(n)
        tables.append((jnp.concatenate(npos, axis=0), jnp.exp(s1 - v1[0][0:1]) / z[c:c + 1, :],
                       rank2[c], jnp.exp(s2 - v2[0][0:1])))
    return tables, ties


def _dup_bf16_word(x):
    u = lax.bitcast_convert_type(x.astype(BF16).astype(F32), jnp.uint32)
    return lax.bitcast_convert_type(u | (u >> 16), jnp.int32)


def _peer_topk_kernel(x_ref, m_ref, wh_ref, wl_ref, kh_ref, kl_ref,
                      h2t_ref, ni_ref, ai_ref, rj_ref, bj_ref, qt_sc):
    hd = pl.program_id(1)

    @pl.when(hd == 0)
    def _():
        x = x_ref[...]
        h2 = x * (1.0 + m_ref[0, 4:5, :]) + m_ref[0, 3:4, :]
        h_hi, h_lo = _split(h2.T)
        h2t_ref[...] = h_hi
        qt_sc[...] = _mm3(wh_ref[...], wl_ref[...], h_hi, h_lo)

    qh = qt_sc[pl.ds(pl.multiple_of(hd * (2 * P_HALF), 2 * P_HALF), 2 * P_HALF), :]
    q1_hi, q1_lo = _split(qh[:P_HALF])
    q2_hi, q2_lo = _split(qh[P_HALF:])
    sc1 = _mm3(kh_ref[0], kl_ref[0], q1_hi, q1_lo)
    sc2 = _mm3(kh_ref[1], kl_ref[1], q2_hi, q2_lo)
    n_chunks = sc1.shape[1] // LANES
    chunk = [slice(c * LANES, (c + 1) * LANES) for c in range(n_chunks)]

    def emit(c, tables):
        npos, a_i, r2, b_j = tables
        ni_ref[0, :, chunk[c]] = _dup_bf16_word(npos)
        ai_ref[0, :, chunk[c]] = _dup_bf16_word(0.5 * a_i)
        rj_ref[0, :, chunk[c]] = r2.astype(BF16)
        bj_ref[0, :, chunk[c]] = b_j.astype(BF16)

    tables, ties = _retrieve_tie_free([sc1[:, ls] for ls in chunk], [sc2[:, ls] for ls in chunk])
    for c in range(n_chunks):
        emit(c, tables[c])

    def redo(chunks):
        for c in chunks:
            emit(c, _retrieve_exact(sc1[:, chunk[c]], sc2[:, chunk[c]]))

    for g0 in range(0, n_chunks, REDO_GROUP):
        group = range(g0, min(g0 + REDO_GROUP, n_chunks))
        flag = functools.reduce(jnp.maximum, [ties[c] for c in group])
        pl.when(jnp.max(flag) > 0.0)(functools.partial(redo, group))


def _peer_topk(x, mods_l, wq_t_hi, wq_t_lo, keys_hi, keys_lo, *, t_p, s_len):
    t, d = x.shape
    tm = TM_TOPK
    n_p = t_p // tm
    tpb = s_len // tm
    row_tab = jax.ShapeDtypeStruct((P_HEADS, N_KEYS, t), jnp.int32)
    row_spec = pl.BlockSpec((1, N_KEYS, tm), lambda i, h: (h, 0, i))
    col_tab = jax.ShapeDtypeStruct((P_HEADS, N_KEYS, t), BF16)
    col_spec = row_spec
    return pl.pallas_call(
        _peer_topk_kernel,
        grid=(t // tm, P_HEADS),
        in_specs=[
            pl.BlockSpec((tm, d), lambda i, h: (i, 0)),
            pl.BlockSpec((1, 6, d), lambda i, h: (_tile_group(i, n_p, tpb), 0, 0)),
            pl.BlockSpec((d, d), lambda i, h: (0, 0)),
            pl.BlockSpec((d, d), lambda i, h: (0, 0)),
            pl.BlockSpec((2, N_KEYS, P_HALF), lambda i, h: (h, 0, 0)),
            pl.BlockSpec((2, N_KEYS, P_HALF), lambda i, h: (h, 0, 0)),
        ],
        out_specs=[pl.BlockSpec((d, tm), lambda i, h: (0, i)), row_spec, row_spec, col_spec, col_spec],
        out_shape=[jax.ShapeDtypeStruct((d, t), BF16), row_tab, row_tab, col_tab, col_tab],
        scratch_shapes=[pltpu.VMEM((d, tm), F32)],
        compiler_params=_params(("parallel", "arbitrary")),
    )(x, mods_l, wq_t_hi, wq_t_lo, keys_hi, keys_lo)


def _row_tile(row):
    tile = pltpu.bitcast(jnp.broadcast_to(row, (8, row.shape[1])), BF16)
    return jnp.concatenate([tile] * (N_KEYS // tile.shape[0]), axis=0)


GELU_C1 = math.sqrt(2.0 / math.pi)
GELU_C2 = 0.044715 * GELU_C1
K_CHUNK = 512


def _peer_dense_kernel(h2t0_ref, h2tn_ref, eu0_ref, eun_ref, evt_ref, ni_ref, ai_ref, rj_ref, bj_ref,
                       x_ref, m_ref, g_ref, b_ref, y_ref, acc_sc, act0_sc, act1_sc, wt_sc, *, n_e):
    e = pl.program_id(1)

    @pl.when((pl.program_id(0) == 0) & (e == 0))
    def _():
        act0_sc[...] = _mm(eu0_ref[...], h2t0_ref[...]).astype(BF16)

    @pl.when(e == 0)
    def _():
        acc_sc[...] = jnp.zeros_like(acc_sc)

    te, tm = wt_sc.shape
    zero = jnp.zeros((), BF16)

    def after(words, product):
        bits = lax.bitcast_convert_type(product[0:1, :], jnp.int32)
        return words + lax.shift_right_logical(lax.shift_right_logical(bits, 16), 16)

    def step(act_ref, next_ref):
        part = None
        nxt_prev = None
        for kc in range(te // K_CHUNK):
            rows = slice(kc * K_CHUNK, (kc + 1) * K_CHUNK)
            nxt = _mm(eun_ref[rows, :], h2tn_ref[...])
            next_ref[rows, :] = nxt.astype(BF16)
            for il in range(kc * K_CHUNK // N_KEYS, (kc + 1) * K_CHUNK // N_KEYS):
                rs = slice(il * N_KEYS, (il + 1) * N_KEYS)
                gate = None
                for h in range(P_HEADS):
                    cnt_words = ni_ref[h, il:il + 1, :]
                    if nxt_prev is not None and h == 0 and il * N_KEYS == kc * K_CHUNK:
                        cnt_words = after(cnt_words, nxt_prev)
                    cnt = _row_tile(cnt_words)
                    half_a = _row_tile(ai_ref[h, il:il + 1, :])
                    gh = jnp.where(rj_ref[h] < cnt, bj_ref[h], zero) * half_a
                    gate = gh if gate is None else gate + gh
                a = act_ref[rs, :]
                t = jnp.tanh(a * (GELU_C1 + GELU_C2 * (a * a)))
                wt_sc[rs, :] = gate * (a + a * t)
            p = _mm(evt_ref[:, rows], wt_sc[rows, :])
            part = p if part is None else part + p
            nxt_prev = nxt
        acc_sc[...] += part

    @pl.when(e % 2 == 0)
    def _():
        step(act0_sc, act1_sc)

    @pl.when(e % 2 == 1)
    def _():
        step(act1_sc, act0_sc)

    @pl.when(e == n_e - 1)
    def _():
        f = acc_sc[...].T
        z = DN_ALPHA * x_ref[...] + m_ref[0, 5:6, :] * f
        y_ref[...] = _layer_norm(z, g_ref[...], b_ref[...])


def _peer_dense(h2t, e_u, e_vt, layer, ni, ai, rj, bj, x, mods_l, ln_g, ln_b, *, t_p, s_len):
    t, d = x.shape
    tm = TM_PEER
    te = TE_PEER
    n_p = t_p // tm
    tpb = s_len // tm
    n_exp = e_u.shape[1]
    n_e = n_exp // te
    n_i = te // N_KEYS
    n_t = t // tm
    assert n_e % 2 == 0
    nxt_tok = lambda i, e: jnp.minimum(i + (e + 1) // n_e, n_t - 1)
    return pl.pallas_call(
        functools.partial(_peer_dense_kernel, n_e=n_e),
        grid=(n_t, n_e),
        in_specs=[
            pl.BlockSpec((d, tm), lambda i, e: (0, 0)),
            pl.BlockSpec((d, tm), lambda i, e: (0, nxt_tok(i, e))),
            pl.BlockSpec((None, te, d), lambda i, e: (layer, 0, 0)),
            pl.BlockSpec((None, te, d), lambda i, e: (layer, (e + 1) % n_e, 0)),
            pl.BlockSpec((None, d, te), lambda i, e: (layer, 0, e)),
            pl.BlockSpec((P_HEADS, n_i, tm), lambda i, e: (0, e, i)),
            pl.BlockSpec((P_HEADS, n_i, tm), lambda i, e: (0, e, i)),
            pl.BlockSpec((P_HEADS, N_KEYS, tm), lambda i, e: (0, 0, i)),
            pl.BlockSpec((P_HEADS, N_KEYS, tm), lambda i, e: (0, 0, i)),
            pl.BlockSpec((tm, d), lambda i, e: (i, 0)),
            pl.BlockSpec((1, 6, d), lambda i, e: (_tile_group(i, n_p, tpb), 0, 0)),
            pl.BlockSpec((1, d), lambda i, e: (0, 0)),
            pl.BlockSpec((1, d), lambda i, e: (0, 0)),
        ],
        out_specs=pl.BlockSpec((tm, d), lambda i, e: (i, 0)),
        out_shape=jax.ShapeDtypeStruct((t, d), F32),
        scratch_shapes=[pltpu.VMEM((d, tm), F32), pltpu.VMEM((te, tm), BF16), pltpu.VMEM((te, tm), BF16),
                        pltpu.VMEM((te, tm), BF16)],
        compiler_params=_params(("arbitrary", "arbitrary")),
    )(h2t, h2t, e_u, e_u, e_vt, ni, ai, rj, bj, x, mods_l, ln_g.reshape(1, d), ln_b.reshape(1, d))


def _rope_tables(n_tok):
    rows = n_tok // GRID_W
    row = jnp.repeat(jnp.arange(rows), GRID_W).astype(F32)
    col = jnp.tile(jnp.arange(GRID_W), rows).astype(F32)
    n_freq = A_HD // 4
    inv = ROPE_THETA ** (-jnp.arange(n_freq, dtype=F32) / n_freq)
    ang = jnp.concatenate([row[:, None] * inv, col[:, None] * inv], -1)
    cos = jnp.repeat(jnp.cos(ang), 2, axis=-1)
    sin = jnp.repeat(jnp.sin(ang), 2, axis=-1)
    sign = jnp.tile(jnp.asarray([-1.0, 1.0], F32), A_HD // 2)
    reps = LANES // A_HD
    return jnp.tile(cos, (1, reps)), jnp.tile(sin * sign, (1, reps))


def _gqa_weights(w_qkv, q_norm, k_norm, w_o):
    d = w_qkv.shape[0]
    nq = A_HEADS * A_HD
    half = (np.arange(A_HEADS) // A_REP) % 2
    sel = np.stack([half == 0, half == 1], axis=1).astype(np.float32)
    wq = w_qkv[:, :nq].reshape(d, A_HEADS, 1, A_HD) * sel[None, :, :, None]
    w_exp = jnp.concatenate([wq.reshape(d, A_HEADS * LANES), w_qkv[:, nq:]], axis=1).astype(BF16)
    gq = jnp.tile(q_norm, 2 * A_HEADS).reshape(1, -1)
    gk = jnp.tile(k_norm, A_KV).reshape(1, -1)
    wo = w_o.reshape(A_HEADS, 1, A_HD, d) * sel[:, :, None, None]
    return w_exp, gq, gk, wo.reshape(A_HEADS * LANES, d).astype(BF16)


def _diff_weights(w_qkv):
    d = w_qkv.shape[0]
    eye = np.eye(2, dtype=np.float32)
    wq = w_qkv[:, :D_W].reshape(d, D_HEADS, 1, 2, D_HD) * eye[None, None, :, :, None]
    return jnp.concatenate([wq.reshape(d, D_HEADS * 2 * LANES), w_qkv[:, D_W:]], axis=1).astype(BF16)


def _split_f32(w):
    hi = w.astype(BF16)
    return hi, (w - hi.astype(F32)).astype(BF16)


def kernel(x_prompt, x_sample, cache_a_k, cache_a_v, cache_d_k, cache_d_v, c, c_ctx, w_mod, b_mod, ln_g, ln_b, a_w_qkv, a_q_norm, a_k_norm, a_w_o, d_w_qkv, d_lambda_q1, d_lambda_k1, d_lambda_q2, d_lambda_k2, d_sub_norm, d_w_o, g_w_in, g_b_in, g_v_norm_g, g_v_norm_b, g_w_s, g_b_s, g_w_out, p_w_q, p_sub_keys, p_expert_u, p_expert_v):
    batch, seq, d = x_prompt.shape
    dec_batch, dec_seq, _ = x_sample.shape
    past = cache_a_k.shape[2]
    t_p = batch * seq
    t_s = dec_batch * dec_seq
    dims = dict(t_p=t_p, s_len=dec_seq)

    x = jnp.concatenate([x_prompt.reshape(t_p, d), x_sample.reshape(t_s, d)], axis=0)
    e_u_all = p_expert_u.astype(BF16)
    e_vt_all = jnp.swapaxes(p_expert_v, 1, 2).astype(BF16)
    n_cond = -(-(1 + dec_batch) // 8) * 8
    cond = jnp.zeros((n_cond, d), F32).at[0].set(c_ctx).at[1:1 + dec_batch].set(c)
    mods = _modulation(cond, w_mod, b_mod).reshape(DEPTH, n_cond, 6, d)
    rope_c, rope_s = _rope_tables(dec_seq)

    ak, av, dk, dv = [], [], [], []
    for i in range(DEPTH):
        kind, j = i % N_MIXERS, i // N_MIXERS
        mods_l = mods[i]
        if kind == 0:
            w_exp, gq, gk, wo = _gqa_weights(a_w_qkv[j], a_q_norm[j], a_k_norm[j], a_w_o[j])
            q, k_att, v_att, k_f, v_f = _qkv_project(
                x, mods_l, w_exp, gq, gk, rope_c, rope_s, nq=A_HEADS, nk=A_KV * A_HD // LANES,
                nv=A_KV * A_HD // LANES, use_norm=True, q_scale=A_HD ** -0.5, **dims)
            ctx_k = cache_a_k[:, j].reshape(dec_batch, past, A_KV * A_HD).astype(BF16)
            ctx_v = cache_a_v[:, j].reshape(dec_batch, past, A_KV * A_HD).astype(BF16)
            o = _attention(_gqa_attn_kernel, (), q, k_att, v_att, ctx_k, ctx_v, n_out=A_HEADS * LANES,
                           t_p=t_p, seq_p=seq, batch_s=dec_batch, seq_s=dec_seq)
            ak.append(k_f[:t_p].reshape(batch, seq, A_KV, A_HD))
            av.append(v_f[:t_p].reshape(batch, seq, A_KV, A_HD))
        elif kind == 1:
            lam_init = 0.8 - 0.6 * math.exp(-0.3 * i)
            w_exp = _diff_weights(d_w_qkv[j])
            q, k_att, v_att, k_f, v_f = _qkv_project(
                x, mods_l, w_exp, jnp.ones((1, 2 * D_W), F32), jnp.ones((1, D_W), F32),
                rope_c, rope_s, nq=2 * D_HEADS, nk=D_HEADS, nv=D_HEADS,
                use_norm=False, q_scale=D_HD ** -0.5, **dims)
            ctx_k = cache_d_k[:, j].reshape(dec_batch, past, D_W).astype(BF16)
            ctx_v = cache_d_v[:, j].reshape(dec_batch, past, D_HEADS * D_VD).astype(BF16)
            lvec = jnp.stack([d_lambda_q1[j], d_lambda_k1[j], d_lambda_q2[j], d_lambda_k2[j]])
            kern = functools.partial(_diff_attn_kernel, lam_init=lam_init)
            o = _attention(kern, (lvec, d_sub_norm[j].reshape(1, D_VD)), q, k_att, v_att, ctx_k, ctx_v,
                           n_out=D_HEADS * D_VD, t_p=t_p, seq_p=seq, batch_s=dec_batch, seq_s=dec_seq)
            wo = d_w_o[j].astype(BF16)
            dk.append(k_f[:t_p].reshape(batch, seq, D_HEADS, 2, D_HD))
            dv.append(v_f[:t_p].reshape(batch, seq, D_HEADS, D_VD))
        if kind == 2:
            x = _gmlp(x, mods_l, g_w_in[j], g_b_in[j], g_v_norm_g[j], g_v_norm_b[j], g_w_s[j], g_b_s[j],
                      g_w_out[j], ln_g[i, 0], ln_b[i, 0], **dims)
        else:
            x = _proj_ln(*o, wo, x, mods_l, ln_g[i, 0], ln_b[i, 0], gate_idx=2, **dims)

        wq_hi, wq_lo = _split_f32(p_w_q[i].T)
        keys = p_sub_keys[i].reshape(P_HEADS * 2, N_KEYS, P_HALF)
        k_hi, k_lo = _split_f32(keys)
        h2t, ni, ai, rj, bj = _peer_topk(x, mods_l, wq_hi, wq_lo, k_hi, k_lo, **dims)
        x = _peer_dense(h2t, e_u_all, e_vt_all, i, ni, ai, rj, bj, x, mods_l, ln_g[i, 1], ln_b[i, 1], **dims)

    y_prompt = x[:t_p].reshape(batch, seq, d)
    y_sample = x[t_p:].reshape(dec_batch, dec_seq, d)
    return (y_prompt, y_sample, jnp.stack(ak, axis=1), jnp.stack(av, axis=1),
            jnp.stack(dk, axis=1), jnp.stack(dv, axis=1))
```

```python
import functools
import math

import numpy as np
import jax
import jax.numpy as jnp
from jax import lax
from jax.experimental import pallas as pl
from jax.experimental.pallas import tpu as pltpu

F32 = jnp.float32
BF16 = jnp.bfloat16

D_MODEL = 1024
DEPTH = 4
GRID_W = 64
N_MIXERS = 3
A_HEADS = 16
A_KV = 4
A_REP = A_HEADS // A_KV
A_HD = 64
D_HEADS = 8
D_HD = 64
D_VD = 2 * D_HD
D_W = D_HEADS * 2 * D_HD
G_DIM = 1024
G_GROUPS = 8
G_CHUNK = 128
P_HEADS = 8
N_KEYS = 128
P_HALF = 64
P_TOPK = 16
ROPE_THETA = 10000.0
DN_ALPHA = (2 * DEPTH) ** 0.25
EPS = 1e-6

LANES = 128
NEG_INF = float("-inf")
VMEM_LIMIT = 56 * 1024 * 1024

TM_QKV = 256
TM_PROJ = 512
TQ_ATTN = 256
TM_GMLP = 256
TM_TOPK = 1024
REDO_GROUP = 4
TM_PEER = 512
TE_PEER = 2048


def _mm(a, b):
    return jnp.dot(a, b, preferred_element_type=F32)


def _mm_nt(a, b):
    return lax.dot_general(a, b, (((1,), (1,)), ((), ())), preferred_element_type=F32)


def _split(x):
    hi = x.astype(BF16)
    lo = (x - hi.astype(F32)).astype(BF16)
    return hi, lo


def _mm3(a_hi, a_lo, b_hi, b_lo):
    return _mm(a_hi, b_hi) + _mm(a_hi, b_lo) + _mm(a_lo, b_hi)


def _layer_norm(z, g, b):
    mu = jnp.mean(z, axis=-1, keepdims=True)
    zc = z - mu
    var = jnp.mean(zc * zc, axis=-1, keepdims=True)
    return zc * lax.rsqrt(var + EPS) * g + b


def _params(sem, flags=None):
    return pltpu.CompilerParams(dimension_semantics=sem, vmem_limit_bytes=VMEM_LIMIT, flags=flags)


def _tile_group(i, n_p_tiles, tiles_per_batch):
    return jnp.where(i < n_p_tiles, 0, 1 + (i - n_p_tiles) // tiles_per_batch)


def _tile_pos_block(i, n_p_tiles, tiles_per_batch):
    return jnp.where(i < n_p_tiles, 0, (i - n_p_tiles) % tiles_per_batch)


def _mods_kernel(c_ref, w_ref, b_ref, o_ref):
    c = c_ref[...]
    a = c * jax.nn.sigmoid(c)
    a_hi, a_lo = _split(a)
    w_hi, w_lo = _split(w_ref[0])
    o_ref[0] = _mm3(a_hi, a_lo, w_hi, w_lo) + b_ref[0]


def _modulation(cond, w_mod, b_mod):
    depth, d, n = w_mod.shape
    rows = cond.shape[0]
    tn = 1536
    return pl.pallas_call(
        _mods_kernel,
        grid=(depth, n // tn),
        in_specs=[
            pl.BlockSpec((rows, d), lambda l, j: (0, 0)),
            pl.BlockSpec((1, d, tn), lambda l, j: (l, 0, j)),
            pl.BlockSpec((1, 1, tn), lambda l, j: (l, 0, j)),
        ],
        out_specs=pl.BlockSpec((1, rows, tn), lambda l, j: (l, 0, j)),
        out_shape=jax.ShapeDtypeStruct((depth, rows, n), F32),
        compiler_params=_params(("parallel", "parallel")),
    )(cond, w_mod, b_mod.reshape(depth, 1, n))


def _swap_pairs(x):
    lane = lax.broadcasted_iota(jnp.int32, x.shape, 1)
    nxt = pltpu.roll(x, LANES - 1, 1)
    prv = pltpu.roll(x, 1, 1)
    return jnp.where((lane & 1) == 0, nxt, prv)


def _x_operands(x, tm, n_p):
    if isinstance(x, tuple):
        xp, xs = x
        shift, lo, t = n_p, 0, xp.shape[0] + xs.shape[0]
    else:
        xp = xs = x
        shift, lo, t = 0, n_p, x.shape[0]
    d = xp.shape[1]
    specs = [pl.BlockSpec((tm, d), lambda i: (jnp.minimum(i, n_p - 1), 0)),
             pl.BlockSpec((tm, d), lambda i: (jnp.maximum(i - shift, lo), 0))]
    return (xp, xs), specs, t


def _select_x(xp_ref, xs_ref, n_p_tiles):
    return jnp.where(pl.program_id(0) < n_p_tiles, xp_ref[...], xs_ref[...])


def _qkv_kernel(xp_ref, xs_ref, m_ref, w_ref, bd_ref, gq_ref, gk_ref, cs_ref, sn_ref,
                q_ref, ka_ref, va_ref, kf_ref, vf_ref,
                *, nq, nk, nv, use_norm, n_p_tiles, q_scale):
    x = _select_x(xp_ref, xs_ref, n_p_tiles)
    h = x * (1.0 + m_ref[0, 1:2, :]) + m_ref[0, 0:1, :]
    y = _mm(h.astype(BF16), w_ref[...])
    is_lat = pl.program_id(0) >= n_p_tiles
    cs = jnp.where(is_lat, cs_ref[...], 1.0)
    sn = jnp.where(is_lat, sn_ref[...], 0.0)
    bd = bd_ref[...]

    def head_norm(yg, gain):
        s_hi, s_lo = _split(yg * yg)
        ms = _mm(s_hi, bd) + _mm(s_lo, bd)
        return yg * lax.rsqrt(ms + EPS) * gain

    def rope(yg):
        return yg * cs + _swap_pairs(yg) * sn

    for g in range(nq):
        sl = slice(g * LANES, (g + 1) * LANES)
        yg = y[:, sl]
        if use_norm:
            yg = head_norm(yg, gq_ref[:, sl])
        q_ref[:, sl] = (rope(yg) * q_scale).astype(BF16)
    for g in range(nk):
        sl = slice(g * LANES, (g + 1) * LANES)
        yg = y[:, (nq + g) * LANES:(nq + g + 1) * LANES]
        if use_norm:
            yg = head_norm(yg, gk_ref[:, sl])
        kf_ref[:, sl] = yg
        ka_ref[:, sl] = rope(yg).astype(BF16)
    for g in range(nv):
        sl = slice(g * LANES, (g + 1) * LANES)
        yg = y[:, (nq + nk + g) * LANES:(nq + nk + g + 1) * LANES]
        vf_ref[:, sl] = yg
        va_ref[:, sl] = yg.astype(BF16)


def _qkv_project(x, mods_l, w_exp, gq, gk, rope_c, rope_s, *, nq, nk, nv, use_norm, t_p, s_len, q_scale):
    tm = TM_QKV
    n_p = t_p // tm
    tpb = s_len // tm
    x_arrays, x_specs, t = _x_operands(x, tm, n_p)
    d = x_arrays[0].shape[1]
    ncol = (nq + nk + nv) * LANES
    bd = np.kron(np.eye(2, dtype=np.float32), np.full((64, 64), 1.0 / 64.0, np.float32))
    bd = jnp.asarray(bd, BF16)
    grp = lambda i: (_tile_group(i, n_p, tpb), 0, 0)
    pos = lambda i: (_tile_pos_block(i, n_p, tpb), 0)
    row = lambda i: (i, 0)
    fixed = lambda i: (0, 0)
    kern = functools.partial(_qkv_kernel, nq=nq, nk=nk, nv=nv, use_norm=use_norm,
                             n_p_tiles=n_p, q_scale=q_scale)
    return pl.pallas_call(
        kern,
        grid=(t // tm,),
        in_specs=x_specs + [
            pl.BlockSpec((1, 6, d), grp),
            pl.BlockSpec((d, ncol), fixed),
            pl.BlockSpec((LANES, LANES), fixed),
            pl.BlockSpec((1, nq * LANES), fixed),
            pl.BlockSpec((1, nk * LANES), fixed),
            pl.BlockSpec((tm, LANES), pos),
            pl.BlockSpec((tm, LANES), pos),
        ],
        out_specs=[
            pl.BlockSpec((tm, nq * LANES), row),
            pl.BlockSpec((tm, nk * LANES), row),
            pl.BlockSpec((tm, nv * LANES), row),
            pl.BlockSpec((tm, nk * LANES), row),
            pl.BlockSpec((tm, nv * LANES), row),
        ],
        out_shape=[
            jax.ShapeDtypeStruct((t, nq * LANES), BF16),
            jax.ShapeDtypeStruct((t, nk * LANES), BF16),
            jax.ShapeDtypeStruct((t, nv * LANES), BF16),
            jax.ShapeDtypeStruct((t, nk * LANES), F32),
            jax.ShapeDtypeStruct((t, nv * LANES), F32),
        ],
        compiler_params=_params(("parallel",)),
    )(*x_arrays, mods_l, w_exp, bd, gq, gk, rope_c, rope_s)


def _softmax_pv(qh, k_all, v_all):
    scores = [_mm_nt(qh, k) for k in k_all]
    m = scores[0].max(axis=-1, keepdims=True)
    for s in scores[1:]:
        m = jnp.maximum(m, s.max(axis=-1, keepdims=True))
    o = None
    l = None
    for s, v in zip(scores, v_all):
        p = jnp.exp(s - m)
        ls = p.sum(axis=-1, keepdims=True)
        os_ = _mm(p.astype(BF16), v)
        o = os_ if o is None else o + os_
        l = ls if l is None else l + ls
    return o, l


def _gqa_attn_kernel(*refs, has_ctx):
    if has_ctx:
        q_ref, k_ref, v_ref, kc_ref, vc_ref, o_ref = refs
    else:
        q_ref, k_ref, v_ref, o_ref = refs
    tq = q_ref.shape[0]
    for g in range(A_KV):
        ks = slice((g // 2) * LANES, (g // 2 + 1) * LANES)
        heads = [slice((g * A_REP + r) * LANES, (g * A_REP + r + 1) * LANES) for r in range(A_REP)]
        k_all = [k_ref[:, ks]]
        v_all = [v_ref[:, ks]]
        if has_ctx:
            k_all.append(kc_ref[0, :, ks])
            v_all.append(vc_ref[0, :, ks])
        o, l = _softmax_pv(jnp.concatenate([q_ref[:, sl] for sl in heads], axis=0), k_all, v_all)
        o = (o / l).astype(BF16)
        for r, sl in enumerate(heads):
            o_ref[:, sl] = o[r * tq:(r + 1) * tq, :]


def _diff_attn_kernel(*refs, has_ctx, lam_init):
    if has_ctx:
        lv_ref, sg_ref, q_ref, k_ref, v_ref, kc_ref, vc_ref, o_ref = refs
    else:
        lv_ref, sg_ref, q_ref, k_ref, v_ref, o_ref = refs
    lv = lv_ref[...]
    lam = (jnp.exp(jnp.sum(lv[0:1] * lv[1:2], axis=-1, keepdims=True))
           - jnp.exp(jnp.sum(lv[2:3] * lv[3:4], axis=-1, keepdims=True)) + lam_init)
    for h in range(D_HEADS):
        ks = slice(h * LANES, (h + 1) * LANES)
        k_all = [k_ref[:, ks]]
        v_all = [v_ref[:, ks]]
        if has_ctx:
            k_all.append(kc_ref[0, :, ks])
            v_all.append(vc_ref[0, :, ks])
        tq = q_ref.shape[0]
        q2 = jnp.concatenate([q_ref[:, (2 * h + j) * LANES:(2 * h + j + 1) * LANES] for j in range(2)], axis=0)
        o, l = _softmax_pv(q2, k_all, v_all)
        o = o / l
        o = o[:tq] - lam * o[tq:]
        ms = jnp.mean(o * o, axis=-1, keepdims=True)
        o = o * lax.rsqrt(ms + EPS) * sg_ref[...] * (1.0 - lam_init)
        o_ref[:, ks] = o.astype(BF16)


def _attention(kern, extra, q, k, v, ctx_k, ctx_v, *, n_out, t_p, seq_p, batch_s, seq_s):
    nqc = q.shape[1]
    nkc = k.shape[1]
    extra_specs = [pl.BlockSpec(a.shape, lambda *_: (0, 0)) for a in extra]
    n_seq_p = t_p // seq_p
    o_p = pl.pallas_call(
        functools.partial(kern, has_ctx=False),
        grid=(n_seq_p,),
        in_specs=extra_specs + [
            pl.BlockSpec((seq_p, nqc), lambda b: (b, 0)),
            pl.BlockSpec((seq_p, nkc), lambda b: (b, 0)),
            pl.BlockSpec((seq_p, nkc), lambda b: (b, 0)),
        ],
        out_specs=pl.BlockSpec((seq_p, n_out), lambda b: (b, 0)),
        out_shape=jax.ShapeDtypeStruct((t_p, n_out), BF16),
        compiler_params=_params(("parallel",)),
    )(*extra, q, k, v)
    tq = TQ_ATTN
    nqt = seq_s // tq
    assert t_p % seq_s == 0 and t_p % tq == 0
    q_off = t_p // tq
    k_off = t_p // seq_s
    past = ctx_k.shape[1]
    o_s = pl.pallas_call(
        functools.partial(kern, has_ctx=True),
        grid=(batch_s, nqt),
        in_specs=extra_specs + [
            pl.BlockSpec((tq, nqc), lambda b, i: (q_off + b * nqt + i, 0)),
            pl.BlockSpec((seq_s, nkc), lambda b, i: (k_off + b, 0)),
            pl.BlockSpec((seq_s, nkc), lambda b, i: (k_off + b, 0)),
            pl.BlockSpec((1, past, nkc), lambda b, i: (b, 0, 0)),
            pl.BlockSpec((1, past, nkc), lambda b, i: (b, 0, 0)),
        ],
        out_specs=pl.BlockSpec((tq, n_out), lambda b, i: (b * nqt + i, 0)),
        out_shape=jax.ShapeDtypeStruct((batch_s * seq_s, n_out), BF16),
        compiler_params=_params(("parallel", "arbitrary")),
    )(*extra, q, k, v, ctx_k, ctx_v)
    return o_p, o_s


def _proj_ln_kernel(op_ref, os_ref, w_ref, xp_ref, xs_ref, m_ref, g_ref, b_ref, y_ref, *, gate_idx, n_p_tiles):
    o = _select_x(op_ref, os_ref, n_p_tiles)
    y = _mm(o, w_ref[...])
    z = DN_ALPHA * _select_x(xp_ref, xs_ref, n_p_tiles) + m_ref[0, gate_idx:gate_idx + 1, :] * y
    y_ref[...] = _layer_norm(z, g_ref[...], b_ref[...])


def _proj_ln(o_p, o_s, w, x, mods_l, ln_g, ln_b, *, gate_idx, t_p, s_len):
    tm = TM_PROJ
    n_p = t_p // tm
    tpb = s_len // tm
    x_arrays, x_specs, t = _x_operands(x, tm, n_p)
    d = x_arrays[0].shape[1]
    kin = o_p.shape[1]
    row = lambda i: (i, 0)
    fixed = lambda i: (0, 0)
    return pl.pallas_call(
        functools.partial(_proj_ln_kernel, gate_idx=gate_idx, n_p_tiles=n_p),
        grid=(t // tm,),
        in_specs=[
            pl.BlockSpec((tm, kin), lambda i: (jnp.minimum(i, n_p - 1), 0)),
            pl.BlockSpec((tm, kin), lambda i: (jnp.maximum(i - n_p, 0), 0)),
            pl.BlockSpec((kin, d), fixed),
        ] + x_specs + [
            pl.BlockSpec((1, 6, d), lambda i: (_tile_group(i, n_p, tpb), 0, 0)),
            pl.BlockSpec((1, d), fixed),
            pl.BlockSpec((1, d), fixed),
        ],
        out_specs=pl.BlockSpec((tm, d), row),
        out_shape=jax.ShapeDtypeStruct((t, d), F32),
        compiler_params=_params(("parallel",)),
    )(o_p, o_s, w, *x_arrays, mods_l, ln_g.reshape(1, d), ln_b.reshape(1, d))


def _gmlp_kernel(x_ref, m_ref, win_ref, bin_ref, vg_ref, vb_ref, ws_ref, bs_ref, wout_ref,
                 g_ref, b_ref, y_ref, uv_sc):
    x = x_ref[...]
    h = x * (1.0 + m_ref[0, 1:2, :]) + m_ref[0, 0:1, :]
    z = jax.nn.gelu(_mm(h.astype(BF16), win_ref[...]) + bin_ref[...])
    u = z[:, :G_DIM]
    v = _layer_norm(z[:, G_DIM:], vg_ref[...], vb_ref[...])
    tm = x.shape[0]
    for c in range(tm // G_CHUNK):
        rs = slice(c * G_CHUNK, (c + 1) * G_CHUNK)
        for g in range(G_GROUPS):
            cs = slice(g * LANES, (g + 1) * LANES)
            vm = _mm(ws_ref[g], v[rs, cs].astype(BF16)) + bs_ref[g]
            uv_sc[rs, cs] = (u[rs, cs] * vm).astype(BF16)
    y = _mm(uv_sc[...], wout_ref[...])
    zz = DN_ALPHA * x + m_ref[0, 2:3, :] * y
    y_ref[...] = _layer_norm(zz, g_ref[...], b_ref[...])


def _gmlp(x, mods_l, w_in, b_in, vg, vb, w_s, b_s, w_out, ln_g, ln_b, *, t_p, s_len):
    t, d = x.shape
    tm = TM_GMLP
    n_p = t_p // tm
    tpb = s_len // tm
    row = lambda i: (i, 0)
    fixed = lambda i: (0, 0)
    fixed3 = lambda i: (0, 0, 0)
    b_s_b = jnp.broadcast_to(b_s[:, :, None], (G_GROUPS, G_CHUNK, LANES))
    return pl.pallas_call(
        _gmlp_kernel,
        grid=(t // tm,),
        in_specs=[
            pl.BlockSpec((tm, d), row),
            pl.BlockSpec((1, 6, d), lambda i: (_tile_group(i, n_p, tpb), 0, 0)),
            pl.BlockSpec((d, 2 * G_DIM), fixed),
            pl.BlockSpec((1, 2 * G_DIM), fixed),
            pl.BlockSpec((1, G_DIM), fixed),
            pl.BlockSpec((1, G_DIM), fixed),
            pl.BlockSpec((G_GROUPS, G_CHUNK, G_CHUNK), fixed3),
            pl.BlockSpec((G_GROUPS, G_CHUNK, LANES), fixed3),
            pl.BlockSpec((G_DIM, d), fixed),
            pl.BlockSpec((1, d), fixed),
            pl.BlockSpec((1, d), fixed),
        ],
        out_specs=pl.BlockSpec((tm, d), row),
        out_shape=jax.ShapeDtypeStruct((t, d), F32),
        scratch_shapes=[pltpu.VMEM((tm, G_DIM), BF16)],
        compiler_params=_params(("parallel",)),
    )(x, mods_l, w_in.astype(BF16), b_in.reshape(1, -1), vg.reshape(1, -1), vb.reshape(1, -1),
      w_s.astype(BF16), b_s_b, w_out.astype(BF16), ln_g.reshape(1, d), ln_b.reshape(1, d))


def _top16(sc):
    n, L = sc.shape
    pos = lax.broadcasted_iota(jnp.int32, (n, L), 0).astype(F32)
    slot = lax.broadcasted_iota(jnp.int32, (P_TOPK, L), 0)

    def body(k, carry):
        cur, rank, vals = carry
        m = jnp.max(cur, axis=0, keepdims=True)
        idx = jnp.min(jnp.where(cur == m, pos, float(n)), axis=0, keepdims=True)
        hit = pos == idx
        cur = jnp.where(hit, NEG_INF, cur)
        rank = jnp.where(hit, lax.convert_element_type(k, F32), rank)
        vals = jnp.where(slot == k, m, vals)
        return cur, rank, vals

    init = (sc, jnp.full((n, L), 64.0, F32), jnp.zeros((P_TOPK, L), F32))
    _, rank, vals = lax.fori_loop(0, P_TOPK, body, init)
    return vals, rank


_CAND_BLOCKS = ((0, 0, 8), (0, 8, 8), (1, 0, 8), (2, 0, 5), (3, 0, 4), (4, 0, 3), (5, 0, 2), (6, 0, 2), (7, 0, 2))


def _pair_select(v1, v2):
    L = v1.shape[1]
    r8 = lax.broadcasted_iota(jnp.int32, (8, L), 0).astype(F32)
    blocks = []
    codes = []
    for a, b0, nb in _CAND_BLOCKS:
        c = v1[a:a + 1] + v2[b0:b0 + 8]
        if nb < 8:
            c = jnp.where(r8 < float(nb), c, NEG_INF)
        blocks.append(c)
        codes.append(r8 + float(a * P_TOPK + b0))
    blocks.append(v1[8:16] + v2[0:1])
    codes.append((r8 + 8.0) * float(P_TOPK))
    cand0 = jnp.concatenate(blocks, axis=0)
    code = jnp.concatenate(codes, axis=0)

    def body(k, carry):
        cur, sel = carry
        m = jnp.max(cur, axis=0, keepdims=True)
        idx = jnp.min(jnp.where(cur == m, code, 1e9), axis=0, keepdims=True)
        hit = code == idx
        return jnp.where(hit, NEG_INF, cur), jnp.where(hit, 1.0, sel)

    _, sel = lax.fori_loop(0, P_TOPK, body, (cand0, jnp.zeros_like(cand0)))
    top = v1[0:1] + v2[0:1]
    z = jnp.sum(jnp.where(sel > 0.0, jnp.exp(cand0 - top), 0.0), axis=0, keepdims=True)
    cnt_lo = jnp.zeros((8, L), F32)
    starts = (0, 16, 24, 32, 40, 48, 56, 64, 72)
    for a in range(8):
        n_a = jnp.sum(sel[starts[a]:starts[a + 1]], axis=0, keepdims=True)
        cnt_lo = jnp.where(r8 == float(a), n_a, cnt_lo)
    count = jnp.concatenate([cnt_lo, sel[72:80]], axis=0)
    return count, z


def _retrieve_exact(s1, s2):
    v1, r1 = _top16(s1)
    v2, r2 = _top16(s2)
    count, z = _pair_select(v1, v2)
    npos = jnp.zeros_like(r1)
    for a in range(P_TOPK):
        npos = jnp.where(r1 == float(a), count[a:a + 1], npos)
    return npos, jnp.exp(s1 - v1[0:1]) / z, r2, jnp.exp(s2 - v2[0:1])


def _oddeven_merge_sort_pairs(n):
    pairs = []
    p = 1
    while p < n:
        k = p
        while k >= 1:
            for j in range(k % p, n - k, 2 * k):
                for i in range(min(k, n - j - k)):
                    if (i + j) // (2 * p) == (i + j + k) // (2 * p):
                        pairs.append((i + j, i + j + k))
            k //= 2
        p *= 2
    return tuple(pairs)


_SORT16 = _oddeven_merge_sort_pairs(P_TOPK)


def _sort16_desc(v):
    for i, j in _SORT16:
        v[i], v[j] = jnp.maximum(v[i], v[j]), jnp.minimum(v[i], v[j])


def _bitonic_merge16_desc(v):
    dist = P_TOPK // 2
    while dist >= 1:
        for i in range(P_TOPK):
            if not i & dist:
                v[i], v[i + dist] = jnp.maximum(v[i], v[i + dist]), jnp.minimum(v[i], v[i + dist])
        dist //= 2


def _merge_top16(a, b):
    v = [jnp.maximum(a[i], b[P_TOPK - 1 - i]) for i in range(P_TOPK)]
    _bitonic_merge16_desc(v)
    return v


def _sorted_top16(s):
    v = [s[8 * k:8 * k + 8, :] for k in range(N_KEYS // 8)]
    _sort16_desc(v)
    for shift in (4, 2, 1):
        v = [jnp.maximum(v[i], pltpu.roll(v[P_TOPK - 1 - i], shift, 0)) for i in range(P_TOPK)]
        _bitonic_merge16_desc(v)
    return v


_ROW_LEN = tuple(P_TOPK // (a + 1) for a in range(P_TOPK))


def _pair_counts(p1, p2):
    rows = [[p1[a] + p2[b] for b in range(_ROW_LEN[a])] for a in range(P_TOPK)]
    ninf = jnp.full_like(p1[0], NEG_INF)
    g0 = list(rows[0])
    g1 = rows[1] + [rows[a][0] for a in range(P_TOPK - 1, 7, -1)]
    _bitonic_merge16_desc(g1)
    g2 = rows[2] + rows[3] + rows[4] + rows[5] + rows[6]
    _sort16_desc(g2)
    g3 = rows[7] + [ninf] * (P_TOPK - len(rows[7]))
    tau = _merge_top16(_merge_top16(g0, g1), _merge_top16(g2, g3))[P_TOPK - 1]
    top = rows[0][0]
    counts = []
    z = jnp.zeros_like(top)
    total = jnp.zeros_like(top)
    for a in range(P_TOPK):
        n = jnp.zeros_like(top)
        for cand in rows[a]:
            ge = cand >= tau
            n = n + jnp.where(ge, 1.0, 0.0)
            z = z + jnp.where(ge, jnp.exp(cand - top), 0.0)
        counts.append(n)
        total = total + n
    return counts, z, jnp.where(total != float(P_TOPK), 1.0, 0.0)


def _retrieve_tie_free(s1_chunks, s2_chunks):
    n_chunks = len(s1_chunks)
    assert n_chunks <= 8
    sub = lax.broadcasted_iota(jnp.int32, (8, LANES), 0)
    blocks = [slice(8 * k, 8 * k + 8) for k in range(N_KEYS // 8)]
    add = lambda p, q: p + q
    rank_sum_distinct = float(sum(range(P_TOPK)) + P_TOPK * (N_KEYS - P_TOPK))

    tops = []
    rank2 = []
    ties = []
    p1 = p2 = None
    for c in range(n_chunks):
        s1, s2 = s1_chunks[c], s2_chunks[c]
        v1 = _sorted_top16(s1)
        v2 = _sorted_top16(s2)
        tops.append((v1, v2))
        tie = jnp.zeros((8, LANES), F32)
        ranks = []
        for blk in blocks:
            r = jnp.zeros((8, LANES), F32)
            for a in range(P_TOPK):
                r = jnp.where(v2[a] > s2[blk], float(a + 1), r)
            ranks.append(r)
        rank2.append(jnp.concatenate(ranks, axis=0))
        rank_sum = jnp.sum(functools.reduce(add, ranks), axis=0, keepdims=True)
        tie = jnp.where(rank_sum != rank_sum_distinct, 1.0, tie)
        for a in range(P_TOPK - 1):
            tie = jnp.where(v1[a] == v1[a + 1], 1.0, tie)
        n_ge = functools.reduce(add, [jnp.where(s1[blk] >= v1[P_TOPK - 1], 1.0, 0.0) for blk in blocks])
        ties.append(jnp.where(jnp.sum(n_ge, axis=0, keepdims=True) != float(P_TOPK), 1.0, tie))
        if c == 0:
            p1, p2 = list(v1), list(v2)
        else:
            p1 = [jnp.where(sub == c, v, p) for v, p in zip(v1, p1)]
            p2 = [jnp.where(sub == c, v, p) for v, p in zip(v2, p2)]

    counts, z, pair_tie = _pair_counts(p1, p2)
    ties = [jnp.maximum(t, pair_tie[c:c + 1, :]) for c, t in enumerate(ties)]

    tables = []
    for c in range(n_chunks):
        s1, s2 = s1_chunks[c], s2_chunks[c]
        v1, v2 = tops[c]
        cnt = [jnp.broadcast_to(n[c:c + 1, :], (8, LANES)) for n in counts] + [jnp.zeros((8, LANES), F32)]
        npos = []
        for blk in blocks:
            n = cnt[0]
            for a in range(P_TOPK):
                n = jnp.where(v1[a] > s1[blk], cnt[a + 1], n)
            npos.append(n)
        tables.append((jnp.concatenate(npos, axis=0), jnp.exp(s1 - v1[0][0:1]) / z[c:c + 1, :],
                       rank2[c], jnp.exp(s2 - v2[0][0:1])))
    return tables, ties


def _dup_bf16_word(x):
    u = lax.bitcast_convert_type(x.astype(BF16).astype(F32), jnp.uint32)
    return lax.bitcast_convert_type(u | (u >> 16), jnp.int32)


def _peer_topk_kernel(x_ref, m_ref, wh_ref, wl_ref, kh_ref, kl_ref,
                      h2t_ref, ni_ref, ai_ref, rj_ref, bj_ref, qt_sc):
    hd = pl.program_id(1)

    @pl.when(hd == 0)
    def _():
        x = x_ref[...]
        h2 = x * (1.0 + m_ref[0, 4:5, :]) + m_ref[0, 3:4, :]
        h_hi, h_lo = _split(h2.T)
        h2t_ref[...] = h_hi
        qt_sc[...] = _mm3(wh_ref[...], wl_ref[...], h_hi, h_lo)

    qh = qt_sc[pl.ds(pl.multiple_of(hd * (2 * P_HALF), 2 * P_HALF), 2 * P_HALF), :]
    q1_hi, q1_lo = _split(qh[:P_HALF])
    q2_hi, q2_lo = _split(qh[P_HALF:])
    sc1 = _mm3(kh_ref[0], kl_ref[0], q1_hi, q1_lo)
    sc2 = _mm3(kh_ref[1], kl_ref[1], q2_hi, q2_lo)
    n_chunks = sc1.shape[1] // LANES
    chunk = [slice(c * LANES, (c + 1) * LANES) for c in range(n_chunks)]

    def emit(c, tables):
        npos, a_i, r2, b_j = tables
        ni_ref[0, :, chunk[c]] = _dup_bf16_word(npos)
        ai_ref[0, :, chunk[c]] = _dup_bf16_word(0.5 * a_i)
        rj_ref[0, :, chunk[c]] = r2.astype(BF16)
        bj_ref[0, :, chunk[c]] = b_j.astype(BF16)

    tables, ties = _retrieve_tie_free([sc1[:, ls] for ls in chunk], [sc2[:, ls] for ls in chunk])
    for c in range(n_chunks):
        emit(c, tables[c])

    def redo(chunks):
        for c in chunks:
            emit(c, _retrieve_exact(sc1[:, chunk[c]], sc2[:, chunk[c]]))

    for g0 in range(0, n_chunks, REDO_GROUP):
        group = range(g0, min(g0 + REDO_GROUP, n_chunks))
        flag = functools.reduce(jnp.maximum, [ties[c] for c in group])
        pl.when(jnp.max(flag) > 0.0)(functools.partial(redo, group))


def _peer_topk(x, mods_l, wq_t_hi, wq_t_lo, keys_hi, keys_lo, *, t_p, s_len):
    t, d = x.shape
    tm = TM_TOPK
    n_p = t_p // tm
    tpb = s_len // tm
    row_tab = jax.ShapeDtypeStruct((P_HEADS, N_KEYS, t), jnp.int32)
    row_spec = pl.BlockSpec((1, N_KEYS, tm), lambda i, h: (h, 0, i))
    col_tab = jax.ShapeDtypeStruct((P_HEADS, N_KEYS, t), BF16)
    col_spec = row_spec
    return pl.pallas_call(
        _peer_topk_kernel,
        grid=(t // tm, P_HEADS),
        in_specs=[
            pl.BlockSpec((tm, d), lambda i, h: (i, 0)),
            pl.BlockSpec((1, 6, d), lambda i, h: (_tile_group(i, n_p, tpb), 0, 0)),
            pl.BlockSpec((d, d), lambda i, h: (0, 0)),
            pl.BlockSpec((d, d), lambda i, h: (0, 0)),
            pl.BlockSpec((2, N_KEYS, P_HALF), lambda i, h: (h, 0, 0)),
            pl.BlockSpec((2, N_KEYS, P_HALF), lambda i, h: (h, 0, 0)),
        ],
        out_specs=[pl.BlockSpec((d, tm), lambda i, h: (0, i)), row_spec, row_spec, col_spec, col_spec],
        out_shape=[jax.ShapeDtypeStruct((d, t), BF16), row_tab, row_tab, col_tab, col_tab],
        scratch_shapes=[pltpu.VMEM((d, tm), F32)],
        compiler_params=_params(("parallel", "arbitrary")),
    )(x, mods_l, wq_t_hi, wq_t_lo, keys_hi, keys_lo)


def _row_tile(row):
    tile = pltpu.bitcast(jnp.broadcast_to(row, (8, row.shape[1])), BF16)
    return jnp.concatenate([tile] * (N_KEYS // tile.shape[0]), axis=0)


GELU_C1 = math.sqrt(2.0 / math.pi)
GELU_C2 = 0.044715 * GELU_C1
K_CHUNK = 512


def _peer_dense_kernel(h2t0_ref, h2tn_ref, eu0_ref, eun_ref, evt_ref, ni_ref, ai_ref, rj_ref, bj_ref,
                       x_ref, m_ref, g_ref, b_ref, y_ref, acc_sc, act0_sc, act1_sc, wt_sc, *, n_e):
    e = pl.program_id(1)

    @pl.when((pl.program_id(0) == 0) & (e == 0))
    def _():
        act0_sc[...] = _mm(eu0_ref[...], h2t0_ref[...]).astype(BF16)

    @pl.when(e == 0)
    def _():
        acc_sc[...] = jnp.zeros_like(acc_sc)

    te, tm = wt_sc.shape
    zero = jnp.zeros((), BF16)

    def after(words, product):
        bits = lax.bitcast_convert_type(product[0:1, :], jnp.int32)
        return words + lax.shift_right_logical(lax.shift_right_logical(bits, 16), 16)

    def step(act_ref, next_ref):
        part = None
        nxt_prev = None
        for kc in range(te // K_CHUNK):
            rows = slice(kc * K_CHUNK, (kc + 1) * K_CHUNK)
            nxt = _mm(eun_ref[rows, :], h2tn_ref[...])
            next_ref[rows, :] = nxt.astype(BF16)
            for il in range(kc * K_CHUNK // N_KEYS, (kc + 1) * K_CHUNK // N_KEYS):
                rs = slice(il * N_KEYS, (il + 1) * N_KEYS)
                gate = None
                for h in range(P_HEADS):
                    cnt_words = ni_ref[h, il:il + 1, :]
                    if nxt_prev is not None and h == 0 and il * N_KEYS == kc * K_CHUNK:
                        cnt_words = after(cnt_words, nxt_prev)
                    cnt = _row_tile(cnt_words)
                    half_a = _row_tile(ai_ref[h, il:il + 1, :])
                    gh = jnp.where(rj_ref[h] < cnt, bj_ref[h], zero) * half_a
                    gate = gh if gate is None else gate + gh
                a = act_ref[rs, :]
                t = jnp.tanh(a * (GELU_C1 + GELU_C2 * (a * a)))
                wt_sc[rs, :] = gate * (a + a * t)
            p = _mm(evt_ref[:, rows], wt_sc[rows, :])
            part = p if part is None else part + p
            nxt_prev = nxt
        acc_sc[...] += part

    @pl.when(e % 2 == 0)
    def _():
        step(act0_sc, act1_sc)

    @pl.when(e % 2 == 1)
    def _():
        step(act1_sc, act0_sc)

    @pl.when(e == n_e - 1)
    def _():
        f = acc_sc[...].T
        z = DN_ALPHA * x_ref[...] + m_ref[0, 5:6, :] * f
        y_ref[...] = _layer_norm(z, g_ref[...], b_ref[...])


def _peer_dense(h2t, e_u, e_vt, layer, ni, ai, rj, bj, x, mods_l, ln_g, ln_b, *, t_p, s_len):
    t, d = x.shape
    tm = TM_PEER
    te = TE_PEER
    n_p = t_p // tm
    tpb = s_len // tm
    n_exp = e_u.shape[1]
    n_e = n_exp // te
    n_i = te // N_KEYS
    n_t = t // tm
    assert n_e % 2 == 0
    nxt_tok = lambda i, e: jnp.minimum(i + (e + 1) // n_e, n_t - 1)
    return pl.pallas_call(
        functools.partial(_peer_dense_kernel, n_e=n_e),
        grid=(n_t, n_e),
        in_specs=[
            pl.BlockSpec((d, tm), lambda i, e: (0, 0)),
            pl.BlockSpec((d, tm), lambda i, e: (0, nxt_tok(i, e))),
            pl.BlockSpec((None, te, d), lambda i, e: (layer, 0, 0)),
            pl.BlockSpec((None, te, d), lambda i, e: (layer, (e + 1) % n_e, 0)),
            pl.BlockSpec((None, d, te), lambda i, e: (layer, 0, e)),
            pl.BlockSpec((P_HEADS, n_i, tm), lambda i, e: (0, e, i)),
            pl.BlockSpec((P_HEADS, n_i, tm), lambda i, e: (0, e, i)),
            pl.BlockSpec((P_HEADS, N_KEYS, tm), lambda i, e: (0, 0, i)),
            pl.BlockSpec((P_HEADS, N_KEYS, tm), lambda i, e: (0, 0, i)),
            pl.BlockSpec((tm, d), lambda i, e: (i, 0)),
            pl.BlockSpec((1, 6, d), lambda i, e: (_tile_group(i, n_p, tpb), 0, 0)),
            pl.BlockSpec((1, d), lambda i, e: (0, 0)),
            pl.BlockSpec((1, d), lambda i, e: (0, 0)),
        ],
        out_specs=pl.BlockSpec((tm, d), lambda i, e: (i, 0)),
        out_shape=jax.ShapeDtypeStruct((t, d), F32),
        scratch_shapes=[pltpu.VMEM((d, tm), F32), pltpu.VMEM((te, tm), BF16), pltpu.VMEM((te, tm), BF16),
                        pltpu.VMEM((te, tm), BF16)],
        compiler_params=_params(("arbitrary", "arbitrary")),
    )(h2t, h2t, e_u, e_u, e_vt, ni, ai, rj, bj, x, mods_l, ln_g.reshape(1, d), ln_b.reshape(1, d))


def _rope_tables(n_tok):
    rows = n_tok // GRID_W
    row = jnp.repeat(jnp.arange(rows), GRID_W).astype(F32)
    col = jnp.tile(jnp.arange(GRID_W), rows).astype(F32)
    n_freq = A_HD // 4
    inv = ROPE_THETA ** (-jnp.arange(n_freq, dtype=F32) / n_freq)
    ang = jnp.concatenate([row[:, None] * inv, col[:, None] * inv], -1)
    cos = jnp.repeat(jnp.cos(ang), 2, axis=-1)
    sin = jnp.repeat(jnp.sin(ang), 2, axis=-1)
    sign = jnp.tile(jnp.asarray([-1.0, 1.0], F32), A_HD // 2)
    reps = LANES // A_HD
    return jnp.tile(cos, (1, reps)), jnp.tile(sin * sign, (1, reps))


def _gqa_weights(w_qkv, q_norm, k_norm, w_o):
    d = w_qkv.shape[0]
    nq = A_HEADS * A_HD
    half = (np.arange(A_HEADS) // A_REP) % 2
    sel = np.stack([half == 0, half == 1], axis=1).astype(np.float32)
    wq = w_qkv[:, :nq].reshape(d, A_HEADS, 1, A_HD) * sel[None, :, :, None]
    w_exp = jnp.concatenate([wq.reshape(d, A_HEADS * LANES), w_qkv[:, nq:]], axis=1).astype(BF16)
    gq = jnp.tile(q_norm, 2 * A_HEADS).reshape(1, -1)
    gk = jnp.tile(k_norm, A_KV).reshape(1, -1)
    wo = w_o.reshape(A_HEADS, 1, A_HD, d) * sel[:, :, None, None]
    return w_exp, gq, gk, wo.reshape(A_HEADS * LANES, d).astype(BF16)


def _diff_weights(w_qkv):
    d = w_qkv.shape[0]
    eye = np.eye(2, dtype=np.float32)
    wq = w_qkv[:, :D_W].reshape(d, D_HEADS, 1, 2, D_HD) * eye[None, None, :, :, None]
    return jnp.concatenate([wq.reshape(d, D_HEADS * 2 * LANES), w_qkv[:, D_W:]], axis=1).astype(BF16)


def _split_f32(w):
    hi = w.astype(BF16)
    return hi, (w - hi.astype(F32)).astype(BF16)


def kernel(x_prompt, x_sample, cache_a_k, cache_a_v, cache_d_k, cache_d_v, c, c_ctx, w_mod, b_mod, ln_g, ln_b, a_w_qkv, a_q_norm, a_k_norm, a_w_o, d_w_qkv, d_lambda_q1, d_lambda_k1, d_lambda_q2, d_lambda_k2, d_sub_norm, d_w_o, g_w_in, g_b_in, g_v_norm_g, g_v_norm_b, g_w_s, g_b_s, g_w_out, p_w_q, p_sub_keys, p_expert_u, p_expert_v):
    batch, seq, d = x_prompt.shape
    dec_batch, dec_seq, _ = x_sample.shape
    past = cache_a_k.shape[2]
    t_p = batch * seq
    t_s = dec_batch * dec_seq
    dims = dict(t_p=t_p, s_len=dec_seq)

    x = (x_prompt.reshape(t_p, d), x_sample.reshape(t_s, d))
    e_u_all = p_expert_u.astype(BF16)
    e_vt_all = jnp.swapaxes(p_expert_v, 1, 2).astype(BF16)
    n_cond = -(-(1 + dec_batch) // 8) * 8
    cond = jnp.zeros((n_cond, d), F32).at[0].set(c_ctx).at[1:1 + dec_batch].set(c)
    mods = _modulation(cond, w_mod, b_mod).reshape(DEPTH, n_cond, 6, d)
    rope_c, rope_s = _rope_tables(dec_seq)

    ak, av, dk, dv = [], [], [], []
    for i in range(DEPTH):
        kind, j = i % N_MIXERS, i // N_MIXERS
        mods_l = mods[i]
        if kind == 0:
            w_exp, gq, gk, wo = _gqa_weights(a_w_qkv[j], a_q_norm[j], a_k_norm[j], a_w_o[j])
            q, k_att, v_att, k_f, v_f = _qkv_project(
                x, mods_l, w_exp, gq, gk, rope_c, rope_s, nq=A_HEADS, nk=A_KV * A_HD // LANES,
                nv=A_KV * A_HD // LANES, use_norm=True, q_scale=A_HD ** -0.5, **dims)
            ctx_k = cache_a_k[:, j].reshape(dec_batch, past, A_KV * A_HD).astype(BF16)
            ctx_v = cache_a_v[:, j].reshape(dec_batch, past, A_KV * A_HD).astype(BF16)
            o = _attention(_gqa_attn_kernel, (), q, k_att, v_att, ctx_k, ctx_v, n_out=A_HEADS * LANES,
                           t_p=t_p, seq_p=seq, batch_s=dec_batch, seq_s=dec_seq)
            ak.append(k_f[:t_p].reshape(batch, seq, A_KV, A_HD))
            av.append(v_f[:t_p].reshape(batch, seq, A_KV, A_HD))
        elif kind == 1:
            lam_init = 0.8 - 0.6 * math.exp(-0.3 * i)
            w_exp = _diff_weights(d_w_qkv[j])
            q, k_att, v_att, k_f, v_f = _qkv_project(
                x, mods_l, w_exp, jnp.ones((1, 2 * D_W), F32), jnp.ones((1, D_W), F32),
                rope_c, rope_s, nq=2 * D_HEADS, nk=D_HEADS, nv=D_HEADS,
                use_norm=False, q_scale=D_HD ** -0.5, **dims)
            ctx_k = cache_d_k[:, j].reshape(dec_batch, past, D_W).astype(BF16)
            ctx_v = cache_d_v[:, j].reshape(dec_batch, past, D_HEADS * D_VD).astype(BF16)
            lvec = jnp.stack([d_lambda_q1[j], d_lambda_k1[j], d_lambda_q2[j], d_lambda_k2[j]])
            kern = functools.partial(_diff_attn_kernel, lam_init=lam_init)
            o = _attention(kern, (lvec, d_sub_norm[j].reshape(1, D_VD)), q, k_att, v_att, ctx_k, ctx_v,
                           n_out=D_HEADS * D_VD, t_p=t_p, seq_p=seq, batch_s=dec_batch, seq_s=dec_seq)
            wo = d_w_o[j].astype(BF16)
            dk.append(k_f[:t_p].reshape(batch, seq, D_HEADS, 2, D_HD))
            dv.append(v_f[:t_p].reshape(batch, seq, D_HEADS, D_VD))
        if kind == 2:
            if isinstance(x, tuple):
                x = jnp.concatenate(x, axis=0)
            x = _gmlp(x, mods_l, g_w_in[j], g_b_in[j], g_v_norm_g[j], g_v_norm_b[j], g_w_s[j], g_b_s[j],
                      g_w_out[j], ln_g[i, 0], ln_b[i, 0], **dims)
        else:
            x = _proj_ln(*o, wo, x, mods_l, ln_g[i, 0], ln_b[i, 0], gate_idx=2, **dims)

        wq_hi, wq_lo = _split_f32(p_w_q[i].T)
        keys = p_sub_keys[i].reshape(P_HEADS * 2, N_KEYS, P_HALF)
        k_hi, k_lo = _split_f32(keys)
        h2t, ni, ai, rj, bj = _peer_topk(x, mods_l, wq_hi, wq_lo, k_hi, k_lo, **dims)
        x = _peer_dense(h2t, e_u_all, e_vt_all, i, ni, ai, rj, bj, x, mods_l, ln_g[i, 1], ln_b[i, 1], **dims)

    y_prompt = x[:t_p].reshape(batch, seq, d)
    y_sample = x[t_p:].reshape(dec_batch, dec_seq, d)
    return (y_prompt, y_sample, jnp.stack(ak, axis=1), jnp.stack(av, axis=1),
            jnp.stack(dk, axis=1), jnp.stack(dv, axis=1))
```

```python
import functools
import math

import numpy as np
import jax
import jax.numpy as jnp
from jax import lax
from jax.experimental import pallas as pl
from jax.experimental.pallas import tpu as pltpu

F32 = jnp.float32
BF16 = jnp.bfloat16

D_MODEL = 1024
DEPTH = 4
GRID_W = 64
N_MIXERS = 3
A_HEADS = 16
A_KV = 4
A_REP = A_HEADS // A_KV
A_HD = 64
D_HEADS = 8
D_HD = 64
D_VD = 2 * D_HD
D_W = D_HEADS * 2 * D_HD
G_DIM = 1024
G_GROUPS = 8
G_CHUNK = 128
P_HEADS = 8
N_KEYS = 128
P_HALF = 64
P_TOPK = 16
ROPE_THETA = 10000.0
DN_ALPHA = (2 * DEPTH) ** 0.25
EPS = 1e-6

LANES = 128
NEG_INF = float("-inf")
VMEM_LIMIT = 56 * 1024 * 1024

TM_QKV = 512
TM_PROJ = 512
TQ_ATTN = 256
TM_GMLP = 256
TM_TOPK = 1024
REDO_GROUP = 4
TM_PEER = 512
TE_PEER = 2048


def _mm(a, b):
    return jnp.dot(a, b, preferred_element_type=F32)


def _mm_nt(a, b):
    return lax.dot_general(a, b, (((1,), (1,)), ((), ())), preferred_element_type=F32)


def _split(x):
    hi = x.astype(BF16)
    lo = (x - hi.astype(F32)).astype(BF16)
    return hi, lo


def _mm3(a_hi, a_lo, b_hi, b_lo):
    return _mm(a_hi, b_hi) + _mm(a_hi, b_lo) + _mm(a_lo, b_hi)


def _layer_norm(z, g, b):
    mu = jnp.mean(z, axis=-1, keepdims=True)
    zc = z - mu
    var = jnp.mean(zc * zc, axis=-1, keepdims=True)
    return zc * lax.rsqrt(var + EPS) * g + b


def _params(sem, flags=None):
    return pltpu.CompilerParams(dimension_semantics=sem, vmem_limit_bytes=VMEM_LIMIT, flags=flags)


def _tile_group(i, n_p_tiles, tiles_per_batch):
    return jnp.where(i < n_p_tiles, 0, 1 + (i - n_p_tiles) // tiles_per_batch)


def _tile_pos_block(i, n_p_tiles, tiles_per_batch):
    return jnp.where(i < n_p_tiles, 0, (i - n_p_tiles) % tiles_per_batch)


def _mods_kernel(c_ref, w_ref, b_ref, o_ref):
    c = c_ref[...]
    a = c * jax.nn.sigmoid(c)
    a_hi, a_lo = _split(a)
    w_hi, w_lo = _split(w_ref[0])
    o_ref[0] = _mm3(a_hi, a_lo, w_hi, w_lo) + b_ref[0]


def _modulation(cond, w_mod, b_mod):
    depth, d, n = w_mod.shape
    rows = cond.shape[0]
    tn = 1536
    return pl.pallas_call(
        _mods_kernel,
        grid=(depth, n // tn),
        in_specs=[
            pl.BlockSpec((rows, d), lambda l, j: (0, 0)),
            pl.BlockSpec((1, d, tn), lambda l, j: (l, 0, j)),
            pl.BlockSpec((1, 1, tn), lambda l, j: (l, 0, j)),
        ],
        out_specs=pl.BlockSpec((1, rows, tn), lambda l, j: (l, 0, j)),
        out_shape=jax.ShapeDtypeStruct((depth, rows, n), F32),
        compiler_params=_params(("parallel", "parallel")),
    )(cond, w_mod, b_mod.reshape(depth, 1, n))


def _swap_pairs(x):
    lane = lax.broadcasted_iota(jnp.int32, x.shape, 1)
    nxt = pltpu.roll(x, LANES - 1, 1)
    prv = pltpu.roll(x, 1, 1)
    return jnp.where((lane & 1) == 0, nxt, prv)


def _x_operands(x, tm, n_p):
    if isinstance(x, tuple):
        xp, xs = x
        shift, lo, t = n_p, 0, xp.shape[0] + xs.shape[0]
    else:
        xp = xs = x
        shift, lo, t = 0, n_p, x.shape[0]
    d = xp.shape[1]
    specs = [pl.BlockSpec((tm, d), lambda i: (jnp.minimum(i, n_p - 1), 0)),
             pl.BlockSpec((tm, d), lambda i: (jnp.maximum(i - shift, lo), 0))]
    return (xp, xs), specs, t


def _select_x(xp_ref, xs_ref, n_p_tiles):
    return jnp.where(pl.program_id(0) < n_p_tiles, xp_ref[...], xs_ref[...])


def _qkv_kernel(xp_ref, xs_ref, m_ref, w_ref, bd_ref, gq_ref, gk_ref, cs_ref, sn_ref,
                q_ref, ka_ref, va_ref, kf_ref, vf_ref,
                *, nq, nk, nv, use_norm, n_p_tiles, q_scale):
    x = _select_x(xp_ref, xs_ref, n_p_tiles)
    h = x * (1.0 + m_ref[0, 1:2, :]) + m_ref[0, 0:1, :]
    y = _mm(h.astype(BF16), w_ref[...])
    is_lat = pl.program_id(0) >= n_p_tiles
    cs = jnp.where(is_lat, cs_ref[...], 1.0)
    sn = jnp.where(is_lat, sn_ref[...], 0.0)
    bd = bd_ref[...]

    def head_norm(yg, gain):
        s_hi, s_lo = _split(yg * yg)
        ms = _mm(s_hi, bd) + _mm(s_lo, bd)
        return yg * lax.rsqrt(ms + EPS) * gain

    def rope(yg):
        return yg * cs + _swap_pairs(yg) * sn

    for g in range(nq):
        sl = slice(g * LANES, (g + 1) * LANES)
        yg = y[:, sl]
        if use_norm:
            yg = head_norm(yg, gq_ref[:, sl])
        q_ref[:, sl] = (rope(yg) * q_scale).astype(BF16)
    for g in range(nk):
        sl = slice(g * LANES, (g + 1) * LANES)
        yg = y[:, (nq + g) * LANES:(nq + g + 1) * LANES]
        if use_norm:
            yg = head_norm(yg, gk_ref[:, sl])
        kf_ref[:, sl] = yg
        ka_ref[:, sl] = rope(yg).astype(BF16)
    for g in range(nv):
        sl = slice(g * LANES, (g + 1) * LANES)
        yg = y[:, (nq + nk + g) * LANES:(nq + nk + g + 1) * LANES]
        vf_ref[:, sl] = yg
        va_ref[:, sl] = yg.astype(BF16)


def _qkv_project(x, mods_l, w_exp, gq, gk, rope_c, rope_s, *, nq, nk, nv, use_norm, t_p, s_len, q_scale):
    tm = TM_QKV
    n_p = t_p // tm
    tpb = s_len // tm
    x_arrays, x_specs, t = _x_operands(x, tm, n_p)
    d = x_arrays[0].shape[1]
    ncol = (nq + nk + nv) * LANES
    bd = np.kron(np.eye(2, dtype=np.float32), np.full((64, 64), 1.0 / 64.0, np.float32))
    bd = jnp.asarray(bd, BF16)
    grp = lambda i: (_tile_group(i, n_p, tpb), 0, 0)
    pos = lambda i: (_tile_pos_block(i, n_p, tpb), 0)
    row = lambda i: (i, 0)
    fixed = lambda i: (0, 0)
    kern = functools.partial(_qkv_kernel, nq=nq, nk=nk, nv=nv, use_norm=use_norm,
                             n_p_tiles=n_p, q_scale=q_scale)
    return pl.pallas_call(
        kern,
        grid=(t // tm,),
        in_specs=x_specs + [
            pl.BlockSpec((1, 6, d), grp),
            pl.BlockSpec((d, ncol), fixed),
            pl.BlockSpec((LANES, LANES), fixed),
            pl.BlockSpec((1, nq * LANES), fixed),
            pl.BlockSpec((1, nk * LANES), fixed),
            pl.BlockSpec((tm, LANES), pos),
            pl.BlockSpec((tm, LANES), pos),
        ],
        out_specs=[
            pl.BlockSpec((tm, nq * LANES), row),
            pl.BlockSpec((tm, nk * LANES), row),
            pl.BlockSpec((tm, nv * LANES), row),
            pl.BlockSpec((tm, nk * LANES), row),
            pl.BlockSpec((tm, nv * LANES), row),
        ],
        out_shape=[
            jax.ShapeDtypeStruct((t, nq * LANES), BF16),
            jax.ShapeDtypeStruct((t, nk * LANES), BF16),
            jax.ShapeDtypeStruct((t, nv * LANES), BF16),
            jax.ShapeDtypeStruct((t, nk * LANES), F32),
            jax.ShapeDtypeStruct((t, nv * LANES), F32),
        ],
        compiler_params=_params(("parallel",)),
    )(*x_arrays, mods_l, w_exp, bd, gq, gk, rope_c, rope_s)


def _softmax_pv(qh, k_all, v_all):
    scores = [_mm_nt(qh, k) for k in k_all]
    m = scores[0].max(axis=-1, keepdims=True)
    for s in scores[1:]:
        m = jnp.maximum(m, s.max(axis=-1, keepdims=True))
    o = None
    l = None
    for s, v in zip(scores, v_all):
        p = jnp.exp(s - m)
        ls = p.sum(axis=-1, keepdims=True)
        os_ = _mm(p.astype(BF16), v)
        o = os_ if o is None else o + os_
        l = ls if l is None else l + ls
    return o, l


def _gqa_attn_kernel(*refs, has_ctx):
    if has_ctx:
        q_ref, k_ref, v_ref, kc_ref, vc_ref, o_ref = refs
    else:
        q_ref, k_ref, v_ref, o_ref = refs
    tq = q_ref.shape[0]
    for g in range(A_KV):
        ks = slice((g // 2) * LANES, (g // 2 + 1) * LANES)
        heads = [slice((g * A_REP + r) * LANES, (g * A_REP + r + 1) * LANES) for r in range(A_REP)]
        k_all = [k_ref[:, ks]]
        v_all = [v_ref[:, ks]]
        if has_ctx:
            k_all.append(kc_ref[0, :, ks])
            v_all.append(vc_ref[0, :, ks])
        o, l = _softmax_pv(jnp.concatenate([q_ref[:, sl] for sl in heads], axis=0), k_all, v_all)
        o = (o / l).astype(BF16)
        for r, sl in enumerate(heads):
            o_ref[:, sl] = o[r * tq:(r + 1) * tq, :]


def _diff_attn_kernel(*refs, has_ctx, lam_init):
    if has_ctx:
        lv_ref, sg_ref, q_ref, k_ref, v_ref, kc_ref, vc_ref, o_ref = refs
    else:
        lv_ref, sg_ref, q_ref, k_ref, v_ref, o_ref = refs
    lv = lv_ref[...]
    lam = (jnp.exp(jnp.sum(lv[0:1] * lv[1:2], axis=-1, keepdims=True))
           - jnp.exp(jnp.sum(lv[2:3] * lv[3:4], axis=-1, keepdims=True)) + lam_init)
    for h in range(D_HEADS):
        ks = slice(h * LANES, (h + 1) * LANES)
        k_all = [k_ref[:, ks]]
        v_all = [v_ref[:, ks]]
        if has_ctx:
            k_all.append(kc_ref[0, :, ks])
            v_all.append(vc_ref[0, :, ks])
        tq = q_ref.shape[0]
        q2 = jnp.concatenate([q_ref[:, (2 * h + j) * LANES:(2 * h + j + 1) * LANES] for j in range(2)], axis=0)
        o, l = _softmax_pv(q2, k_all, v_all)
        o = o / l
        o = o[:tq] - lam * o[tq:]
        ms = jnp.mean(o * o, axis=-1, keepdims=True)
        o = o * lax.rsqrt(ms + EPS) * sg_ref[...] * (1.0 - lam_init)
        o_ref[:, ks] = o.astype(BF16)


def _attention(kern, extra, q, k, v, ctx_k, ctx_v, *, n_out, t_p, seq_p, batch_s, seq_s):
    nqc = q.shape[1]
    nkc = k.shape[1]
    extra_specs = [pl.BlockSpec(a.shape, lambda *_: (0, 0)) for a in extra]
    n_seq_p = t_p // seq_p
    o_p = pl.pallas_call(
        functools.partial(kern, has_ctx=False),
        grid=(n_seq_p,),
        in_specs=extra_specs + [
            pl.BlockSpec((seq_p, nqc), lambda b: (b, 0)),
            pl.BlockSpec((seq_p, nkc), lambda b: (b, 0)),
            pl.BlockSpec((seq_p, nkc), lambda b: (b, 0)),
        ],
        out_specs=pl.BlockSpec((seq_p, n_out), lambda b: (b, 0)),
        out_shape=jax.ShapeDtypeStruct((t_p, n_out), BF16),
        compiler_params=_params(("parallel",)),
    )(*extra, q, k, v)
    tq = TQ_ATTN
    nqt = seq_s // tq
    assert t_p % seq_s == 0 and t_p % tq == 0
    q_off = t_p // tq
    k_off = t_p // seq_s
    past = ctx_k.shape[1]
    o_s = pl.pallas_call(
        functools.partial(kern, has_ctx=True),
        grid=(batch_s, nqt),
        in_specs=extra_specs + [
            pl.BlockSpec((tq, nqc), lambda b, i: (q_off + b * nqt + i, 0)),
            pl.BlockSpec((seq_s, nkc), lambda b, i: (k_off + b, 0)),
            pl.BlockSpec((seq_s, nkc), lambda b, i: (k_off + b, 0)),
            pl.BlockSpec((1, past, nkc), lambda b, i: (b, 0, 0)),
            pl.BlockSpec((1, past, nkc), lambda b, i: (b, 0, 0)),
        ],
        out_specs=pl.BlockSpec((tq, n_out), lambda b, i: (b * nqt + i, 0)),
        out_shape=jax.ShapeDtypeStruct((batch_s * seq_s, n_out), BF16),
        compiler_params=_params(("parallel", "arbitrary")),
    )(*extra, q, k, v, ctx_k, ctx_v)
    return o_p, o_s


def _proj_ln_kernel(op_ref, os_ref, w_ref, xp_ref, xs_ref, m_ref, g_ref, b_ref, y_ref, *, gate_idx, n_p_tiles):
    o = _select_x(op_ref, os_ref, n_p_tiles)
    y = _mm(o, w_ref[...])
    z = DN_ALPHA * _select_x(xp_ref, xs_ref, n_p_tiles) + m_ref[0, gate_idx:gate_idx + 1, :] * y
    y_ref[...] = _layer_norm(z, g_ref[...], b_ref[...])


def _proj_ln(o_p, o_s, w, x, mods_l, ln_g, ln_b, *, gate_idx, t_p, s_len):
    tm = TM_PROJ
    n_p = t_p // tm
    tpb = s_len // tm
    x_arrays, x_specs, t = _x_operands(x, tm, n_p)
    d = x_arrays[0].shape[1]
    kin = o_p.shape[1]
    row = lambda i: (i, 0)
    fixed = lambda i: (0, 0)
    return pl.pallas_call(
        functools.partial(_proj_ln_kernel, gate_idx=gate_idx, n_p_tiles=n_p),
        grid=(t // tm,),
        in_specs=[
            pl.BlockSpec((tm, kin), lambda i: (jnp.minimum(i, n_p - 1), 0)),
            pl.BlockSpec((tm, kin), lambda i: (jnp.maximum(i - n_p, 0), 0)),
            pl.BlockSpec((kin, d), fixed),
        ] + x_specs + [
            pl.BlockSpec((1, 6, d), lambda i: (_tile_group(i, n_p, tpb), 0, 0)),
            pl.BlockSpec((1, d), fixed),
            pl.BlockSpec((1, d), fixed),
        ],
        out_specs=pl.BlockSpec((tm, d), row),
        out_shape=jax.ShapeDtypeStruct((t, d), F32),
        compiler_params=_params(("parallel",)),
    )(o_p, o_s, w, *x_arrays, mods_l, ln_g.reshape(1, d), ln_b.reshape(1, d))


def _gmlp_kernel(x_ref, m_ref, win_ref, bin_ref, vg_ref, vb_ref, ws_ref, bs_ref, wout_ref,
                 g_ref, b_ref, y_ref, uv_sc):
    x = x_ref[...]
    h = x * (1.0 + m_ref[0, 1:2, :]) + m_ref[0, 0:1, :]
    z = jax.nn.gelu(_mm(h.astype(BF16), win_ref[...]) + bin_ref[...])
    u = z[:, :G_DIM]
    v = _layer_norm(z[:, G_DIM:], vg_ref[...], vb_ref[...])
    tm = x.shape[0]
    for c in range(tm // G_CHUNK):
        rs = slice(c * G_CHUNK, (c + 1) * G_CHUNK)
        for g in range(G_GROUPS):
            cs = slice(g * LANES, (g + 1) * LANES)
            vm = _mm(ws_ref[g], v[rs, cs].astype(BF16)) + bs_ref[g]
            uv_sc[rs, cs] = (u[rs, cs] * vm).astype(BF16)
    y = _mm(uv_sc[...], wout_ref[...])
    zz = DN_ALPHA * x + m_ref[0, 2:3, :] * y
    y_ref[...] = _layer_norm(zz, g_ref[...], b_ref[...])


def _gmlp(x, mods_l, w_in, b_in, vg, vb, w_s, b_s, w_out, ln_g, ln_b, *, t_p, s_len):
    t, d = x.shape
    tm = TM_GMLP
    n_p = t_p // tm
    tpb = s_len // tm
    row = lambda i: (i, 0)
    fixed = lambda i: (0, 0)
    fixed3 = lambda i: (0, 0, 0)
    b_s_b = jnp.broadcast_to(b_s[:, :, None], (G_GROUPS, G_CHUNK, LANES))
    return pl.pallas_call(
        _gmlp_kernel,
        grid=(t // tm,),
        in_specs=[
            pl.BlockSpec((tm, d), row),
            pl.BlockSpec((1, 6, d), lambda i: (_tile_group(i, n_p, tpb), 0, 0)),
            pl.BlockSpec((d, 2 * G_DIM), fixed),
            pl.BlockSpec((1, 2 * G_DIM), fixed),
            pl.BlockSpec((1, G_DIM), fixed),
            pl.BlockSpec((1, G_DIM), fixed),
            pl.BlockSpec((G_GROUPS, G_CHUNK, G_CHUNK), fixed3),
            pl.BlockSpec((G_GROUPS, G_CHUNK, LANES), fixed3),
            pl.BlockSpec((G_DIM, d), fixed),
            pl.BlockSpec((1, d), fixed),
            pl.BlockSpec((1, d), fixed),
        ],
        out_specs=pl.BlockSpec((tm, d), row),
        out_shape=jax.ShapeDtypeStruct((t, d), F32),
        scratch_shapes=[pltpu.VMEM((tm, G_DIM), BF16)],
        compiler_params=_params(("parallel",)),
    )(x, mods_l, w_in.astype(BF16), b_in.reshape(1, -1), vg.reshape(1, -1), vb.reshape(1, -1),
      w_s.astype(BF16), b_s_b, w_out.astype(BF16), ln_g.reshape(1, d), ln_b.reshape(1, d))


def _top16(sc):
    n, L = sc.shape
    pos = lax.broadcasted_iota(jnp.int32, (n, L), 0).astype(F32)
    slot = lax.broadcasted_iota(jnp.int32, (P_TOPK, L), 0)

    def body(k, carry):
        cur, rank, vals = carry
        m = jnp.max(cur, axis=0, keepdims=True)
        idx = jnp.min(jnp.where(cur == m, pos, float(n)), axis=0, keepdims=True)
        hit = pos == idx
        cur = jnp.where(hit, NEG_INF, cur)
        rank = jnp.where(hit, lax.convert_element_type(k, F32), rank)
        vals = jnp.where(slot == k, m, vals)
        return cur, rank, vals

    init = (sc, jnp.full((n, L), 64.0, F32), jnp.zeros((P_TOPK, L), F32))
    _, rank, vals = lax.fori_loop(0, P_TOPK, body, init)
    return vals, rank


_CAND_BLOCKS = ((0, 0, 8), (0, 8, 8), (1, 0, 8), (2, 0, 5), (3, 0, 4), (4, 0, 3), (5, 0, 2), (6, 0, 2), (7, 0, 2))


def _pair_select(v1, v2):
    L = v1.shape[1]
    r8 = lax.broadcasted_iota(jnp.int32, (8, L), 0).astype(F32)
    blocks = []
    codes = []
    for a, b0, nb in _CAND_BLOCKS:
        c = v1[a:a + 1] + v2[b0:b0 + 8]
        if nb < 8:
            c = jnp.where(r8 < float(nb), c, NEG_INF)
        blocks.append(c)
        codes.append(r8 + float(a * P_TOPK + b0))
    blocks.append(v1[8:16] + v2[0:1])
    codes.append((r8 + 8.0) * float(P_TOPK))
    cand0 = jnp.concatenate(blocks, axis=0)
    code = jnp.concatenate(codes, axis=0)

    def body(k, carry):
        cur, sel = carry
        m = jnp.max(cur, axis=0, keepdims=True)
        idx = jnp.min(jnp.where(cur == m, code, 1e9), axis=0, keepdims=True)
        hit = code == idx
        return jnp.where(hit, NEG_INF, cur), jnp.where(hit, 1.0, sel)

    _, sel = lax.fori_loop(0, P_TOPK, body, (cand0, jnp.zeros_like(cand0)))
    top = v1[0:1] + v2[0:1]
    z = jnp.sum(jnp.where(sel > 0.0, jnp.exp(cand0 - top), 0.0), axis=0, keepdims=True)
    cnt_lo = jnp.zeros((8, L), F32)
    starts = (0, 16, 24, 32, 40, 48, 56, 64, 72)
    for a in range(8):
        n_a = jnp.sum(sel[starts[a]:starts[a + 1]], axis=0, keepdims=True)
        cnt_lo = jnp.where(r8 == float(a), n_a, cnt_lo)
    count = jnp.concatenate([cnt_lo, sel[72:80]], axis=0)
    return count, z


def _retrieve_exact(s1, s2):
    v1, r1 = _top16(s1)
    v2, r2 = _top16(s2)
    count, z = _pair_select(v1, v2)
    npos = jnp.zeros_like(r1)
    for a in range(P_TOPK):
        npos = jnp.where(r1 == float(a), count[a:a + 1], npos)
    return npos, jnp.exp(s1 - v1[0:1]) / z, r2, jnp.exp(s2 - v2[0:1])


def _oddeven_merge_sort_pairs(n):
    pairs = []
    p = 1
    while p < n:
        k = p
        while k >= 1:
            for j in range(k % p, n - k, 2 * k):
                for i in range(min(k, n - j - k)):
                    if (i + j) // (2 * p) == (i + j + k) // (2 * p):
                        pairs.append((i + j, i + j + k))
            k //= 2
        p *= 2
    return tuple(pairs)


_SORT16 = _oddeven_merge_sort_pairs(P_TOPK)


def _sort16_desc(v):
    for i, j in _SORT16:
        v[i], v[j] = jnp.maximum(v[i], v[j]), jnp.minimum(v[i], v[j])


def _bitonic_merge16_desc(v):
    dist = P_TOPK // 2
    while dist >= 1:
        for i in range(P_TOPK):
            if not i & dist:
                v[i], v[i + dist] = jnp.maximum(v[i], v[i + dist]), jnp.minimum(v[i], v[i + dist])
        dist //= 2


def _merge_top16(a, b):
    v = [jnp.maximum(a[i], b[P_TOPK - 1 - i]) for i in range(P_TOPK)]
    _bitonic_merge16_desc(v)
    return v


def _sorted_top16(s):
    v = [s[8 * k:8 * k + 8, :] for k in range(N_KEYS // 8)]
    _sort16_desc(v)
    for shift in (4, 2, 1):
        v = [jnp.maximum(v[i], pltpu.roll(v[P_TOPK - 1 - i], shift, 0)) for i in range(P_TOPK)]
        _bitonic_merge16_desc(v)
    return v


_ROW_LEN = tuple(P_TOPK // (a + 1) for a in range(P_TOPK))


def _pair_counts(p1, p2):
    rows = [[p1[a] + p2[b] for b in range(_ROW_LEN[a])] for a in range(P_TOPK)]
    ninf = jnp.full_like(p1[0], NEG_INF)
    g0 = list(rows[0])
    g1 = rows[1] + [rows[a][0] for a in range(P_TOPK - 1, 7, -1)]
    _bitonic_merge16_desc(g1)
    g2 = rows[2] + rows[3] + rows[4] + rows[5] + rows[6]
    _sort16_desc(g2)
    g3 = rows[7] + [ninf] * (P_TOPK - len(rows[7]))
    tau = _merge_top16(_merge_top16(g0, g1), _merge_top16(g2, g3))[P_TOPK - 1]
    top = rows[0][0]
    counts = []
    z = jnp.zeros_like(top)
    total = jnp.zeros_like(top)
    for a in range(P_TOPK):
        n = jnp.zeros_like(top)
        for cand in rows[a]:
            ge = cand >= tau
            n = n + jnp.where(ge, 1.0, 0.0)
            z = z + jnp.where(ge, jnp.exp(cand - top), 0.0)
        counts.append(n)
        total = total + n
    return counts, z, jnp.where(total != float(P_TOPK), 1.0, 0.0)


def _retrieve_tie_free(s1_chunks, s2_chunks):
    n_chunks = len(s1_chunks)
    assert n_chunks <= 8
    sub = lax.broadcasted_iota(jnp.int32, (8, LANES), 0)
    blocks = [slice(8 * k, 8 * k + 8) for k in range(N_KEYS // 8)]
    add = lambda p, q: p + q
    rank_sum_distinct = float(sum(range(P_TOPK)) + P_TOPK * (N_KEYS - P_TOPK))

    tops = []
    rank2 = []
    ties = []
    p1 = p2 = None
    for c in range(n_chunks):
        s1, s2 = s1_chunks[c], s2_chunks[c]
        v1 = _sorted_top16(s1)
        v2 = _sorted_top16(s2)
        tops.append((v1, v2))
        tie = jnp.zeros((8, LANES), F32)
        ranks = []
        for blk in blocks:
            r = jnp.zeros((8, LANES), F32)
            for a in range(P_TOPK):
                r = jnp.where(v2[a] > s2[blk], float(a + 1), r)
            ranks.append(r)
        rank2.append(jnp.concatenate(ranks, axis=0))
        rank_sum = jnp.sum(functools.reduce(add, ranks), axis=0, keepdims=True)
        tie = jnp.where(rank_sum != rank_sum_distinct, 1.0, tie)
        for a in range(P_TOPK - 1):
            tie = jnp.where(v1[a] == v1[a + 1], 1.0, tie)
        n_ge = functools.reduce(add, [jnp.where(s1[blk] >= v1[P_TOPK - 1], 1.0, 0.0) for blk in blocks])
        ties.append(jnp.where(jnp.sum(n_ge, axis=0, keepdims=True) != float(P_TOPK), 1.0, tie))
        if c == 0:
            p1, p2 = list(v1), list(v2)
        else:
            p1 = [jnp.where(sub == c, v, p) for v, p in zip(v1, p1)]
            p2 = [jnp.where(sub == c, v, p) for v, p in zip(v2, p2)]

    counts, z, pair_tie = _pair_counts(p1, p2)
    ties = [jnp.maximum(t, pair_tie[c:c + 1, :]) for c, t in enumerate(ties)]

    tables = []
    for c in range(n_chunks):
        s1, s2 = s1_chunks[c], s2_chunks[c]
        v1, v2 = tops[c]
        cnt = [jnp.broadcast_to(n[c:c + 1, :], (8, LANES)) for n in counts] + [jnp.zeros((8, LANES), F32)]
        npos = []
        for blk in blocks:
            n = cnt[0]
            for a in range(P_TOPK):
                n = jnp.where(v1[a] > s1[blk], cnt[a + 1], n)
            npos.append(n)
        tables.append((jnp.concatenate(npos, axis=0), jnp.exp(s1 - v1[0][0:1]) / z[c:c + 1, :],
                       rank2[c], jnp.exp(s2 - v2[0][0:1])))
    return tables, ties


def _dup_bf16_word(x):
    u = lax.bitcast_convert_type(x.astype(BF16).astype(F32), jnp.uint32)
    return lax.bitcast_convert_type(u | (u >> 16), jnp.int32)


def _peer_topk_kernel(x_ref, m_ref, wh_ref, wl_ref, kh_ref, kl_ref,
                      h2t_ref, ni_ref, ai_ref, rj_ref, bj_ref, qt_sc):
    hd = pl.program_id(1)

    @pl.when(hd == 0)
    def _():
        x = x_ref[...]
        h2 = x * (1.0 + m_ref[0, 4:5, :]) + m_ref[0, 3:4, :]
        h_hi, h_lo = _split(h2.T)
        h2t_ref[...] = h_hi
        qt_sc[...] = _mm3(wh_ref[...], wl_ref[...], h_hi, h_lo)

    qh = qt_sc[pl.ds(pl.multiple_of(hd * (2 * P_HALF), 2 * P_HALF), 2 * P_HALF), :]
    q1_hi, q1_lo = _split(qh[:P_HALF])
    q2_hi, q2_lo = _split(qh[P_HALF:])
    sc1 = _mm3(kh_ref[0], kl_ref[0], q1_hi, q1_lo)
    sc2 = _mm3(kh_ref[1], kl_ref[1], q2_hi, q2_lo)
    n_chunks = sc1.shape[1] // LANES
    chunk = [slice(c * LANES, (c + 1) * LANES) for c in range(n_chunks)]

    def emit(c, tables):
        npos, a_i, r2, b_j = tables
        ni_ref[0, :, chunk[c]] = _dup_bf16_word(npos)
        ai_ref[0, :, chunk[c]] = _dup_bf16_word(0.5 * a_i)
        rj_ref[0, :, chunk[c]] = r2.astype(BF16)
        bj_ref[0, :, chunk[c]] = b_j.astype(BF16)

    tables, ties = _retrieve_tie_free([sc1[:, ls] for ls in chunk], [sc2[:, ls] for ls in chunk])
    for c in range(n_chunks):
        emit(c, tables[c])

    def redo(chunks):
        for c in chunks:
            emit(c, _retrieve_exact(sc1[:, chunk[c]], sc2[:, chunk[c]]))

    for g0 in range(0, n_chunks, REDO_GROUP):
        group = range(g0, min(g0 + REDO_GROUP, n_chunks))
        flag = functools.reduce(jnp.maximum, [ties[c] for c in group])
        pl.when(jnp.max(flag) > 0.0)(functools.partial(redo, group))


def _peer_topk(x, mods_l, wq_t_hi, wq_t_lo, keys_hi, keys_lo, *, t_p, s_len):
    t, d = x.shape
    tm = TM_TOPK
    n_p = t_p // tm
    tpb = s_len // tm
    row_tab = jax.ShapeDtypeStruct((P_HEADS, N_KEYS, t), jnp.int32)
    row_spec = pl.BlockSpec((1, N_KEYS, tm), lambda i, h: (h, 0, i))
    col_tab = jax.ShapeDtypeStruct((P_HEADS, N_KEYS, t), BF16)
    col_spec = row_spec
    return pl.pallas_call(
        _peer_topk_kernel,
        grid=(t // tm, P_HEADS),
        in_specs=[
            pl.BlockSpec((tm, d), lambda i, h: (i, 0)),
            pl.BlockSpec((1, 6, d), lambda i, h: (_tile_group(i, n_p, tpb), 0, 0)),
            pl.BlockSpec((d, d), lambda i, h: (0, 0)),
            pl.BlockSpec((d, d), lambda i, h: (0, 0)),
            pl.BlockSpec((2, N_KEYS, P_HALF), lambda i, h: (h, 0, 0)),
            pl.BlockSpec((2, N_KEYS, P_HALF), lambda i, h: (h, 0, 0)),
        ],
        out_specs=[pl.BlockSpec((d, tm), lambda i, h: (0, i)), row_spec, row_spec, col_spec, col_spec],
        out_shape=[jax.ShapeDtypeStruct((d, t), BF16), row_tab, row_tab, col_tab, col_tab],
        scratch_shapes=[pltpu.VMEM((d, tm), F32)],
        compiler_params=_params(("parallel", "arbitrary")),
    )(x, mods_l, wq_t_hi, wq_t_lo, keys_hi, keys_lo)


def _row_tile(row):
    tile = pltpu.bitcast(jnp.broadcast_to(row, (8, row.shape[1])), BF16)
    return jnp.concatenate([tile] * (N_KEYS // tile.shape[0]), axis=0)


GELU_C1 = math.sqrt(2.0 / math.pi)
GELU_C2 = 0.044715 * GELU_C1
K_CHUNK = 512


def _peer_dense_kernel(h2t0_ref, h2tn_ref, eu0_ref, eun_ref, evt_ref, ni_ref, ai_ref, rj_ref, bj_ref,
                       x_ref, m_ref, g_ref, b_ref, y_ref, acc_sc, act0_sc, act1_sc, wt_sc, *, n_e):
    e = pl.program_id(1)

    @pl.when((pl.program_id(0) == 0) & (e == 0))
    def _():
        act0_sc[...] = _mm(eu0_ref[...], h2t0_ref[...]).astype(BF16)

    @pl.when(e == 0)
    def _():
        acc_sc[...] = jnp.zeros_like(acc_sc)

    te, tm = wt_sc.shape
    zero = jnp.zeros((), BF16)

    def after(words, product):
        bits = lax.bitcast_convert_type(product[0:1, :], jnp.int32)
        return words + lax.shift_right_logical(lax.shift_right_logical(bits, 16), 16)

    def step(act_ref, next_ref):
        part = None
        nxt_prev = None
        for kc in range(te // K_CHUNK):
            rows = slice(kc * K_CHUNK, (kc + 1) * K_CHUNK)
            nxt = _mm(eun_ref[rows, :], h2tn_ref[...])
            next_ref[rows, :] = nxt.astype(BF16)
            for il in range(kc * K_CHUNK // N_KEYS, (kc + 1) * K_CHUNK // N_KEYS):
                rs = slice(il * N_KEYS, (il + 1) * N_KEYS)
                gate = None
                for h in range(P_HEADS):
                    cnt_words = ni_ref[h, il:il + 1, :]
                    if nxt_prev is not None and h == 0 and il * N_KEYS == kc * K_CHUNK:
                        cnt_words = after(cnt_words, nxt_prev)
                    cnt = _row_tile(cnt_words)
                    half_a = _row_tile(ai_ref[h, il:il + 1, :])
                    gh = jnp.where(rj_ref[h] < cnt, bj_ref[h], zero) * half_a
                    gate = gh if gate is None else gate + gh
                a = act_ref[rs, :]
                t = jnp.tanh(a * (GELU_C1 + GELU_C2 * (a * a)))
                wt_sc[rs, :] = gate * (a + a * t)
            p = _mm(evt_ref[:, rows], wt_sc[rows, :])
            part = p if part is None else part + p
            nxt_prev = nxt
        acc_sc[...] += part

    @pl.when(e % 2 == 0)
    def _():
        step(act0_sc, act1_sc)

    @pl.when(e % 2 == 1)
    def _():
        step(act1_sc, act0_sc)

    @pl.when(e == n_e - 1)
    def _():
        f = acc_sc[...].T
        z = DN_ALPHA * x_ref[...] + m_ref[0, 5:6, :] * f
        y_ref[...] = _layer_norm(z, g_ref[...], b_ref[...])


def _peer_dense(h2t, e_u, e_vt, layer, ni, ai, rj, bj, x, mods_l, ln_g, ln_b, *, t_p, s_len):
    t, d = x.shape
    tm = TM_PEER
    te = TE_PEER
    n_p = t_p // tm
    tpb = s_len // tm
    n_exp = e_u.shape[1]
    n_e = n_exp // te
    n_i = te // N_KEYS
    n_t = t // tm
    assert n_e % 2 == 0
    nxt_tok = lambda i, e: jnp.minimum(i + (e + 1) // n_e, n_t - 1)
    return pl.pallas_call(
        functools.partial(_peer_dense_kernel, n_e=n_e),
        grid=(n_t, n_e),
        in_specs=[
            pl.BlockSpec((d, tm), lambda i, e: (0, 0)),
            pl.BlockSpec((d, tm), lambda i, e: (0, nxt_tok(i, e))),
            pl.BlockSpec((None, te, d), lambda i, e: (layer, 0, 0)),
            pl.BlockSpec((None, te, d), lambda i, e: (layer, (e + 1) % n_e, 0)),
            pl.BlockSpec((None, d, te), lambda i, e: (layer, 0, e)),
            pl.BlockSpec((P_HEADS, n_i, tm), lambda i, e: (0, e, i)),
            pl.BlockSpec((P_HEADS, n_i, tm), lambda i, e: (0, e, i)),
            pl.BlockSpec((P_HEADS, N_KEYS, tm), lambda i, e: (0, 0, i)),
            pl.BlockSpec((P_HEADS, N_KEYS, tm), lambda i, e: (0, 0, i)),
            pl.BlockSpec((tm, d), lambda i, e: (i, 0)),
            pl.BlockSpec((1, 6, d), lambda i, e: (_tile_group(i, n_p, tpb), 0, 0)),
            pl.BlockSpec((1, d), lambda i, e: (0, 0)),
            pl.BlockSpec((1, d), lambda i, e: (0, 0)),
        ],
        out_specs=pl.BlockSpec((tm, d), lambda i, e: (i, 0)),
        out_shape=jax.ShapeDtypeStruct((t, d), F32),
        scratch_shapes=[pltpu.VMEM((d, tm), F32), pltpu.VMEM((te, tm), BF16), pltpu.VMEM((te, tm), BF16),
                        pltpu.VMEM((te, tm), BF16)],
        compiler_params=_params(("arbitrary", "arbitrary")),
    )(h2t, h2t, e_u, e_u, e_vt, ni, ai, rj, bj, x, mods_l, ln_g.reshape(1, d), ln_b.reshape(1, d))


def _rope_tables(n_tok):
    rows = n_tok // GRID_W
    row = jnp.repeat(jnp.arange(rows), GRID_W).astype(F32)
    col = jnp.tile(jnp.arange(GRID_W), rows).astype(F32)
    n_freq = A_HD // 4
    inv = ROPE_THETA ** (-jnp.arange(n_freq, dtype=F32) / n_freq)
    ang = jnp.concatenate([row[:, None] * inv, col[:, None] * inv], -1)
    cos = jnp.repeat(jnp.cos(ang), 2, axis=-1)
    sin = jnp.repeat(jnp.sin(ang), 2, axis=-1)
    sign = jnp.tile(jnp.asarray([-1.0, 1.0], F32), A_HD // 2)
    reps = LANES // A_HD
    return jnp.tile(cos, (1, reps)), jnp.tile(sin * sign, (1, reps))


def _gqa_weights(w_qkv, q_norm, k_norm, w_o):
    d = w_qkv.shape[0]
    nq = A_HEADS * A_HD
    half = (np.arange(A_HEADS) // A_REP) % 2
    sel = np.stack([half == 0, half == 1], axis=1).astype(np.float32)
    wq = w_qkv[:, :nq].reshape(d, A_HEADS, 1, A_HD) * sel[None, :, :, None]
    w_exp = jnp.concatenate([wq.reshape(d, A_HEADS * LANES), w_qkv[:, nq:]], axis=1).astype(BF16)
    gq = jnp.tile(q_norm, 2 * A_HEADS).reshape(1, -1)
    gk = jnp.tile(k_norm, A_KV).reshape(1, -1)
    wo = w_o.reshape(A_HEADS, 1, A_HD, d) * sel[:, :, None, None]
    return w_exp, gq, gk, wo.reshape(A_HEADS * LANES, d).astype(BF16)


def _diff_weights(w_qkv):
    d = w_qkv.shape[0]
    eye = np.eye(2, dtype=np.float32)
    wq = w_qkv[:, :D_W].reshape(d, D_HEADS, 1, 2, D_HD) * eye[None, None, :, :, None]
    return jnp.concatenate([wq.reshape(d, D_HEADS * 2 * LANES), w_qkv[:, D_W:]], axis=1).astype(BF16)


def _split_f32(w):
    hi = w.astype(BF16)
    return hi, (w - hi.astype(F32)).astype(BF16)


def kernel(x_prompt, x_sample, cache_a_k, cache_a_v, cache_d_k, cache_d_v, c, c_ctx, w_mod, b_mod, ln_g, ln_b, a_w_qkv, a_q_norm, a_k_norm, a_w_o, d_w_qkv, d_lambda_q1, d_lambda_k1, d_lambda_q2, d_lambda_k2, d_sub_norm, d_w_o, g_w_in, g_b_in, g_v_norm_g, g_v_norm_b, g_w_s, g_b_s, g_w_out, p_w_q, p_sub_keys, p_expert_u, p_expert_v):
    batch, seq, d = x_prompt.shape
    dec_batch, dec_seq, _ = x_sample.shape
    past = cache_a_k.shape[2]
    t_p = batch * seq
    t_s = dec_batch * dec_seq
    dims = dict(t_p=t_p, s_len=dec_seq)

    x = (x_prompt.reshape(t_p, d), x_sample.reshape(t_s, d))
    e_u_all = p_expert_u.astype(BF16)
    e_vt_all = jnp.swapaxes(p_expert_v, 1, 2).astype(BF16)
    n_cond = -(-(1 + dec_batch) // 8) * 8
    cond = jnp.zeros((n_cond, d), F32).at[0].set(c_ctx).at[1:1 + dec_batch].set(c)
    mods = _modulation(cond, w_mod, b_mod).reshape(DEPTH, n_cond, 6, d)
    rope_c, rope_s = _rope_tables(dec_seq)

    ak, av, dk, dv = [], [], [], []
    for i in range(DEPTH):
        kind, j = i % N_MIXERS, i // N_MIXERS
        mods_l = mods[i]
        if kind == 0:
            w_exp, gq, gk, wo = _gqa_weights(a_w_qkv[j], a_q_norm[j], a_k_norm[j], a_w_o[j])
            q, k_att, v_att, k_f, v_f = _qkv_project(
                x, mods_l, w_exp, gq, gk, rope_c, rope_s, nq=A_HEADS, nk=A_KV * A_HD // LANES,
                nv=A_KV * A_HD // LANES, use_norm=True, q_scale=A_HD ** -0.5, **dims)
            ctx_k = cache_a_k[:, j].reshape(dec_batch, past, A_KV * A_HD).astype(BF16)
            ctx_v = cache_a_v[:, j].reshape(dec_batch, past, A_KV * A_HD).astype(BF16)
            o = _attention(_gqa_attn_kernel, (), q, k_att, v_att, ctx_k, ctx_v, n_out=A_HEADS * LANES,
                           t_p=t_p, seq_p=seq, batch_s=dec_batch, seq_s=dec_seq)
            ak.append(k_f[:t_p].reshape(batch, seq, A_KV, A_HD))
            av.append(v_f[:t_p].reshape(batch, seq, A_KV, A_HD))
        elif kind == 1:
            lam_init = 0.8 - 0.6 * math.exp(-0.3 * i)
            w_exp = _diff_weights(d_w_qkv[j])
            q, k_att, v_att, k_f, v_f = _qkv_project(
                x, mods_l, w_exp, jnp.ones((1, 2 * D_W), F32), jnp.ones((1, D_W), F32),
                rope_c, rope_s, nq=2 * D_HEADS, nk=D_HEADS, nv=D_HEADS,
                use_norm=False, q_scale=D_HD ** -0.5, **dims)
            ctx_k = cache_d_k[:, j].reshape(dec_batch, past, D_W).astype(BF16)
            ctx_v = cache_d_v[:, j].reshape(dec_batch, past, D_HEADS * D_VD).astype(BF16)
            lvec = jnp.stack([d_lambda_q1[j], d_lambda_k1[j], d_lambda_q2[j], d_lambda_k2[j]])
            kern = functools.partial(_diff_attn_kernel, lam_init=lam_init)
            o = _attention(kern, (lvec, d_sub_norm[j].reshape(1, D_VD)), q, k_att, v_att, ctx_k, ctx_v,
                           n_out=D_HEADS * D_VD, t_p=t_p, seq_p=seq, batch_s=dec_batch, seq_s=dec_seq)
            wo = d_w_o[j].astype(BF16)
            dk.append(k_f[:t_p].reshape(batch, seq, D_HEADS, 2, D_HD))
            dv.append(v_f[:t_p].reshape(batch, seq, D_HEADS, D_VD))
        if kind == 2:
            if isinstance(x, tuple):
                x = jnp.concatenate(x, axis=0)
            x = _gmlp(x, mods_l, g_w_in[j], g_b_in[j], g_v_norm_g[j], g_v_norm_b[j], g_w_s[j], g_b_s[j],
                      g_w_out[j], ln_g[i, 0], ln_b[i, 0], **dims)
        else:
            x = _proj_ln(*o, wo, x, mods_l, ln_g[i, 0], ln_b[i, 0], gate_idx=2, **dims)

        wq_hi, wq_lo = _split_f32(p_w_q[i].T)
        keys = p_sub_keys[i].reshape(P_HEADS * 2, N_KEYS, P_HALF)
        k_hi, k_lo = _split_f32(keys)
        h2t, ni, ai, rj, bj = _peer_topk(x, mods_l, wq_hi, wq_lo, k_hi, k_lo, **dims)
        x = _peer_dense(h2t, e_u_all, e_vt_all, i, ni, ai, rj, bj, x, mods_l, ln_g[i, 1], ln_b[i, 1], **dims)

    y_prompt = x[:t_p].reshape(batch, seq, d)
    y_sample = x[t_p:].reshape(dec_batch, dec_seq, d)
    return (y_prompt, y_sample, jnp.stack(ak, axis=1), jnp.stack(av, axis=1),
            jnp.stack(dk, axis=1), jnp.stack(dv, axis=1))
```

```python
import functools
import math

import numpy as np
import jax
import jax.numpy as jnp
from jax import lax
from jax.experimental import pallas as pl
from jax.experimental.pallas import tpu as pltpu

F32 = jnp.float32
BF16 = jnp.bfloat16

D_MODEL = 1024
DEPTH = 4
GRID_W = 64
N_MIXERS = 3
A_HEADS = 16
A_KV = 4
A_REP = A_HEADS // A_KV
A_HD = 64
D_HEADS = 8
D_HD = 64
D_VD = 2 * D_HD
D_W = D_HEADS * 2 * D_HD
G_DIM = 1024
G_GROUPS = 8
G_CHUNK = 128
P_HEADS = 8
N_KEYS = 128
P_HALF = 64
P_TOPK = 16
ROPE_THETA = 10000.0
DN_ALPHA = (2 * DEPTH) ** 0.25
EPS = 1e-6

LANES = 128
NEG_INF = float("-inf")
VMEM_LIMIT = 56 * 1024 * 1024

TM_QKV = 512
TM_PROJ = 512
TQ_ATTN = 256
TM_GMLP = 512
TM_TOPK = 1024
REDO_GROUP = 4
TM_PEER = 512
TE_PEER = 2048


def _mm(a, b):
    return jnp.dot(a, b, preferred_element_type=F32)


def _mm_nt(a, b):
    return lax.dot_general(a, b, (((1,), (1,)), ((), ())), preferred_element_type=F32)


def _split(x):
    hi = x.astype(BF16)
    lo = (x - hi.astype(F32)).astype(BF16)
    return hi, lo


def _mm3(a_hi, a_lo, b_hi, b_lo):
    return _mm(a_hi, b_hi) + _mm(a_hi, b_lo) + _mm(a_lo, b_hi)


def _layer_norm(z, g, b):
    mu = jnp.mean(z, axis=-1, keepdims=True)
    zc = z - mu
    var = jnp.mean(zc * zc, axis=-1, keepdims=True)
    return zc * lax.rsqrt(var + EPS) * g + b


def _params(sem, flags=None):
    return pltpu.CompilerParams(dimension_semantics=sem, vmem_limit_bytes=VMEM_LIMIT, flags=flags)


def _tile_group(i, n_p_tiles, tiles_per_batch):
    return jnp.where(i < n_p_tiles, 0, 1 + (i - n_p_tiles) // tiles_per_batch)


def _tile_pos_block(i, n_p_tiles, tiles_per_batch):
    return jnp.where(i < n_p_tiles, 0, (i - n_p_tiles) % tiles_per_batch)


def _mods_kernel(c_ref, w_ref, b_ref, o_ref):
    c = c_ref[...]
    a = c * jax.nn.sigmoid(c)
    a_hi, a_lo = _split(a)
    w_hi, w_lo = _split(w_ref[0])
    o_ref[0] = _mm3(a_hi, a_lo, w_hi, w_lo) + b_ref[0]


def _modulation(cond, w_mod, b_mod):
    depth, d, n = w_mod.shape
    rows = cond.shape[0]
    tn = 1536
    return pl.pallas_call(
        _mods_kernel,
        grid=(depth, n // tn),
        in_specs=[
            pl.BlockSpec((rows, d), lambda l, j: (0, 0)),
            pl.BlockSpec((1, d, tn), lambda l, j: (l, 0, j)),
            pl.BlockSpec((1, 1, tn), lambda l, j: (l, 0, j)),
        ],
        out_specs=pl.BlockSpec((1, rows, tn), lambda l, j: (l, 0, j)),
        out_shape=jax.ShapeDtypeStruct((depth, rows, n), F32),
        compiler_params=_params(("parallel", "parallel")),
    )(cond, w_mod, b_mod.reshape(depth, 1, n))


def _swap_pairs(x):
    lane = lax.broadcasted_iota(jnp.int32, x.shape, 1)
    nxt = pltpu.roll(x, LANES - 1, 1)
    prv = pltpu.roll(x, 1, 1)
    return jnp.where((lane & 1) == 0, nxt, prv)


def _x_operands(x, tm, n_p):
    if isinstance(x, tuple):
        xp, xs = x
        shift, lo, t = n_p, 0, xp.shape[0] + xs.shape[0]
    else:
        xp = xs = x
        shift, lo, t = 0, n_p, x.shape[0]
    d = xp.shape[1]
    specs = [pl.BlockSpec((tm, d), lambda i: (jnp.minimum(i, n_p - 1), 0)),
             pl.BlockSpec((tm, d), lambda i: (jnp.maximum(i - shift, lo), 0))]
    return (xp, xs), specs, t


def _select_x(xp_ref, xs_ref, n_p_tiles):
    return jnp.where(pl.program_id(0) < n_p_tiles, xp_ref[...], xs_ref[...])


def _qkv_kernel(xp_ref, xs_ref, m_ref, w_ref, bd_ref, gq_ref, gk_ref, cs_ref, sn_ref,
                q_ref, ka_ref, va_ref, kf_ref, vf_ref,
                *, nq, nk, nv, use_norm, n_p_tiles, q_scale):
    x = _select_x(xp_ref, xs_ref, n_p_tiles)
    h = x * (1.0 + m_ref[0, 1:2, :]) + m_ref[0, 0:1, :]
    y = _mm(h.astype(BF16), w_ref[...])
    is_lat = pl.program_id(0) >= n_p_tiles
    cs = jnp.where(is_lat, cs_ref[...], 1.0)
    sn = jnp.where(is_lat, sn_ref[...], 0.0)
    bd = bd_ref[...]

    def head_norm(yg, gain):
        s_hi, s_lo = _split(yg * yg)
        ms = _mm(s_hi, bd) + _mm(s_lo, bd)
        return yg * lax.rsqrt(ms + EPS) * gain

    def rope(yg):
        return yg * cs + _swap_pairs(yg) * sn

    for g in range(nq):
        sl = slice(g * LANES, (g + 1) * LANES)
        yg = y[:, sl]
        if use_norm:
            yg = head_norm(yg, gq_ref[:, sl])
        q_ref[:, sl] = (rope(yg) * q_scale).astype(BF16)
    for g in range(nk):
        sl = slice(g * LANES, (g + 1) * LANES)
        yg = y[:, (nq + g) * LANES:(nq + g + 1) * LANES]
        if use_norm:
            yg = head_norm(yg, gk_ref[:, sl])
        kf_ref[:, sl] = yg
        ka_ref[:, sl] = rope(yg).astype(BF16)
    for g in range(nv):
        sl = slice(g * LANES, (g + 1) * LANES)
        yg = y[:, (nq + nk + g) * LANES:(nq + nk + g + 1) * LANES]
        vf_ref[:, sl] = yg
        va_ref[:, sl] = yg.astype(BF16)


def _qkv_project(x, mods_l, w_exp, gq, gk, rope_c, rope_s, *, nq, nk, nv, use_norm, t_p, s_len, q_scale):
    tm = TM_QKV
    n_p = t_p // tm
    tpb = s_len // tm
    x_arrays, x_specs, t = _x_operands(x, tm, n_p)
    d = x_arrays[0].shape[1]
    ncol = (nq + nk + nv) * LANES
    bd = np.kron(np.eye(2, dtype=np.float32), np.full((64, 64), 1.0 / 64.0, np.float32))
    bd = jnp.asarray(bd, BF16)
    grp = lambda i: (_tile_group(i, n_p, tpb), 0, 0)
    pos = lambda i: (_tile_pos_block(i, n_p, tpb), 0)
    row = lambda i: (i, 0)
    fixed = lambda i: (0, 0)
    kern = functools.partial(_qkv_kernel, nq=nq, nk=nk, nv=nv, use_norm=use_norm,
                             n_p_tiles=n_p, q_scale=q_scale)
    return pl.pallas_call(
        kern,
        grid=(t // tm,),
        in_specs=x_specs + [
            pl.BlockSpec((1, 6, d), grp),
            pl.BlockSpec((d, ncol), fixed),
            pl.BlockSpec((LANES, LANES), fixed),
            pl.BlockSpec((1, nq * LANES), fixed),
            pl.BlockSpec((1, nk * LANES), fixed),
            pl.BlockSpec((tm, LANES), pos),
            pl.BlockSpec((tm, LANES), pos),
        ],
        out_specs=[
            pl.BlockSpec((tm, nq * LANES), row),
            pl.BlockSpec((tm, nk * LANES), row),
            pl.BlockSpec((tm, nv * LANES), row),
            pl.BlockSpec((tm, nk * LANES), row),
            pl.BlockSpec((tm, nv * LANES), row),
        ],
        out_shape=[
            jax.ShapeDtypeStruct((t, nq * LANES), BF16),
            jax.ShapeDtypeStruct((t, nk * LANES), BF16),
            jax.ShapeDtypeStruct((t, nv * LANES), BF16),
            jax.ShapeDtypeStruct((t, nk * LANES), F32),
            jax.ShapeDtypeStruct((t, nv * LANES), F32),
        ],
        compiler_params=_params(("parallel",)),
    )(*x_arrays, mods_l, w_exp, bd, gq, gk, rope_c, rope_s)


def _softmax_pv(qh, k_all, v_all):
    scores = [_mm_nt(qh, k) for k in k_all]
    m = scores[0].max(axis=-1, keepdims=True)
    for s in scores[1:]:
        m = jnp.maximum(m, s.max(axis=-1, keepdims=True))
    o = None
    l = None
    for s, v in zip(scores, v_all):
        p = jnp.exp(s - m)
        ls = p.sum(axis=-1, keepdims=True)
        os_ = _mm(p.astype(BF16), v)
        o = os_ if o is None else o + os_
        l = ls if l is None else l + ls
    return o, l


def _gqa_attn_kernel(*refs, has_ctx):
    if has_ctx:
        q_ref, k_ref, v_ref, kc_ref, vc_ref, o_ref = refs
    else:
        q_ref, k_ref, v_ref, o_ref = refs
    tq = q_ref.shape[0]
    for g in range(A_KV):
        ks = slice((g // 2) * LANES, (g // 2 + 1) * LANES)
        heads = [slice((g * A_REP + r) * LANES, (g * A_REP + r + 1) * LANES) for r in range(A_REP)]
        k_all = [k_ref[:, ks]]
        v_all = [v_ref[:, ks]]
        if has_ctx:
            k_all.append(kc_ref[0, :, ks])
            v_all.append(vc_ref[0, :, ks])
        o, l = _softmax_pv(jnp.concatenate([q_ref[:, sl] for sl in heads], axis=0), k_all, v_all)
        o = (o / l).astype(BF16)
        for r, sl in enumerate(heads):
            o_ref[:, sl] = o[r * tq:(r + 1) * tq, :]


def _diff_attn_kernel(*refs, has_ctx, lam_init):
    if has_ctx:
        lv_ref, sg_ref, q_ref, k_ref, v_ref, kc_ref, vc_ref, o_ref = refs
    else:
        lv_ref, sg_ref, q_ref, k_ref, v_ref, o_ref = refs
    lv = lv_ref[...]
    lam = (jnp.exp(jnp.sum(lv[0:1] * lv[1:2], axis=-1, keepdims=True))
           - jnp.exp(jnp.sum(lv[2:3] * lv[3:4], axis=-1, keepdims=True)) + lam_init)
    for h in range(D_HEADS):
        ks = slice(h * LANES, (h + 1) * LANES)
        k_all = [k_ref[:, ks]]
        v_all = [v_ref[:, ks]]
        if has_ctx:
            k_all.append(kc_ref[0, :, ks])
            v_all.append(vc_ref[0, :, ks])
        tq = q_ref.shape[0]
        q2 = jnp.concatenate([q_ref[:, (2 * h + j) * LANES:(2 * h + j + 1) * LANES] for j in range(2)], axis=0)
        o, l = _softmax_pv(q2, k_all, v_all)
        o = o / l
        o = o[:tq] - lam * o[tq:]
        ms = jnp.mean(o * o, axis=-1, keepdims=True)
        o = o * lax.rsqrt(ms + EPS) * sg_ref[...] * (1.0 - lam_init)
        o_ref[:, ks] = o.astype(BF16)


def _attention(kern, extra, q, k, v, ctx_k, ctx_v, *, n_out, t_p, seq_p, batch_s, seq_s):
    nqc = q.shape[1]
    nkc = k.shape[1]
    extra_specs = [pl.BlockSpec(a.shape, lambda *_: (0, 0)) for a in extra]
    n_seq_p = t_p // seq_p
    o_p = pl.pallas_call(
        functools.partial(kern, has_ctx=False),
        grid=(n_seq_p,),
        in_specs=extra_specs + [
            pl.BlockSpec((seq_p, nqc), lambda b: (b, 0)),
            pl.BlockSpec((seq_p, nkc), lambda b: (b, 0)),
            pl.BlockSpec((seq_p, nkc), lambda b: (b, 0)),
        ],
        out_specs=pl.BlockSpec((seq_p, n_out), lambda b: (b, 0)),
        out_shape=jax.ShapeDtypeStruct((t_p, n_out), BF16),
        compiler_params=_params(("parallel",)),
    )(*extra, q, k, v)
    tq = TQ_ATTN
    nqt = seq_s // tq
    assert t_p % seq_s == 0 and t_p % tq == 0
    q_off = t_p // tq
    k_off = t_p // seq_s
    past = ctx_k.shape[1]
    o_s = pl.pallas_call(
        functools.partial(kern, has_ctx=True),
        grid=(batch_s, nqt),
        in_specs=extra_specs + [
            pl.BlockSpec((tq, nqc), lambda b, i: (q_off + b * nqt + i, 0)),
            pl.BlockSpec((seq_s, nkc), lambda b, i: (k_off + b, 0)),
            pl.BlockSpec((seq_s, nkc), lambda b, i: (k_off + b, 0)),
            pl.BlockSpec((1, past, nkc), lambda b, i: (b, 0, 0)),
            pl.BlockSpec((1, past, nkc), lambda b, i: (b, 0, 0)),
        ],
        out_specs=pl.BlockSpec((tq, n_out), lambda b, i: (b * nqt + i, 0)),
        out_shape=jax.ShapeDtypeStruct((batch_s * seq_s, n_out), BF16),
        compiler_params=_params(("parallel", "arbitrary")),
    )(*extra, q, k, v, ctx_k, ctx_v)
    return o_p, o_s


def _proj_ln_kernel(op_ref, os_ref, w_ref, xp_ref, xs_ref, m_ref, g_ref, b_ref, y_ref, *, gate_idx, n_p_tiles):
    o = _select_x(op_ref, os_ref, n_p_tiles)
    y = _mm(o, w_ref[...])
    z = DN_ALPHA * _select_x(xp_ref, xs_ref, n_p_tiles) + m_ref[0, gate_idx:gate_idx + 1, :] * y
    y_ref[...] = _layer_norm(z, g_ref[...], b_ref[...])


def _proj_ln(o_p, o_s, w, x, mods_l, ln_g, ln_b, *, gate_idx, t_p, s_len):
    tm = TM_PROJ
    n_p = t_p // tm
    tpb = s_len // tm
    x_arrays, x_specs, t = _x_operands(x, tm, n_p)
    d = x_arrays[0].shape[1]
    kin = o_p.shape[1]
    row = lambda i: (i, 0)
    fixed = lambda i: (0, 0)
    return pl.pallas_call(
        functools.partial(_proj_ln_kernel, gate_idx=gate_idx, n_p_tiles=n_p),
        grid=(t // tm,),
        in_specs=[
            pl.BlockSpec((tm, kin), lambda i: (jnp.minimum(i, n_p - 1), 0)),
            pl.BlockSpec((tm, kin), lambda i: (jnp.maximum(i - n_p, 0), 0)),
            pl.BlockSpec((kin, d), fixed),
        ] + x_specs + [
            pl.BlockSpec((1, 6, d), lambda i: (_tile_group(i, n_p, tpb), 0, 0)),
            pl.BlockSpec((1, d), fixed),
            pl.BlockSpec((1, d), fixed),
        ],
        out_specs=pl.BlockSpec((tm, d), row),
        out_shape=jax.ShapeDtypeStruct((t, d), F32),
        compiler_params=_params(("parallel",)),
    )(o_p, o_s, w, *x_arrays, mods_l, ln_g.reshape(1, d), ln_b.reshape(1, d))


def _gmlp_kernel(x_ref, m_ref, win_ref, bin_ref, vg_ref, vb_ref, ws_ref, bs_ref, wout_ref,
                 g_ref, b_ref, y_ref, uv_sc):
    x = x_ref[...]
    h = x * (1.0 + m_ref[0, 1:2, :]) + m_ref[0, 0:1, :]
    z = jax.nn.gelu(_mm(h.astype(BF16), win_ref[...]) + bin_ref[...])
    u = z[:, :G_DIM]
    v = _layer_norm(z[:, G_DIM:], vg_ref[...], vb_ref[...])
    tm = x.shape[0]
    for c in range(tm // G_CHUNK):
        rs = slice(c * G_CHUNK, (c + 1) * G_CHUNK)
        for g in range(G_GROUPS):
            cs = slice(g * LANES, (g + 1) * LANES)
            vm = _mm(ws_ref[g], v[rs, cs].astype(BF16)) + bs_ref[g]
            uv_sc[rs, cs] = (u[rs, cs] * vm).astype(BF16)
    y = _mm(uv_sc[...], wout_ref[...])
    zz = DN_ALPHA * x + m_ref[0, 2:3, :] * y
    y_ref[...] = _layer_norm(zz, g_ref[...], b_ref[...])


def _gmlp(x, mods_l, w_in, b_in, vg, vb, w_s, b_s, w_out, ln_g, ln_b, *, t_p, s_len):
    t, d = x.shape
    tm = TM_GMLP
    n_p = t_p // tm
    tpb = s_len // tm
    row = lambda i: (i, 0)
    fixed = lambda i: (0, 0)
    fixed3 = lambda i: (0, 0, 0)
    b_s_b = jnp.broadcast_to(b_s[:, :, None], (G_GROUPS, G_CHUNK, LANES))
    return pl.pallas_call(
        _gmlp_kernel,
        grid=(t // tm,),
        in_specs=[
            pl.BlockSpec((tm, d), row),
            pl.BlockSpec((1, 6, d), lambda i: (_tile_group(i, n_p, tpb), 0, 0)),
            pl.BlockSpec((d, 2 * G_DIM), fixed),
            pl.BlockSpec((1, 2 * G_DIM), fixed),
            pl.BlockSpec((1, G_DIM), fixed),
            pl.BlockSpec((1, G_DIM), fixed),
            pl.BlockSpec((G_GROUPS, G_CHUNK, G_CHUNK), fixed3),
            pl.BlockSpec((G_GROUPS, G_CHUNK, LANES), fixed3),
            pl.BlockSpec((G_DIM, d), fixed),
            pl.BlockSpec((1, d), fixed),
            pl.BlockSpec((1, d), fixed),
        ],
        out_specs=pl.BlockSpec((tm, d), row),
        out_shape=jax.ShapeDtypeStruct((t, d), F32),
        scratch_shapes=[pltpu.VMEM((tm, G_DIM), BF16)],
        compiler_params=_params(("parallel",)),
    )(x, mods_l, w_in.astype(BF16), b_in.reshape(1, -1), vg.reshape(1, -1), vb.reshape(1, -1),
      w_s.astype(BF16), b_s_b, w_out.astype(BF16), ln_g.reshape(1, d), ln_b.reshape(1, d))


def _top16(sc):
    n, L = sc.shape
    pos = lax.broadcasted_iota(jnp.int32, (n, L), 0).astype(F32)
    slot = lax.broadcasted_iota(jnp.int32, (P_TOPK, L), 0)

    def body(k, carry):
        cur, rank, vals = carry
        m = jnp.max(cur, axis=0, keepdims=True)
        idx = jnp.min(jnp.where(cur == m, pos, float(n)), axis=0, keepdims=True)
        hit = pos == idx
        cur = jnp.where(hit, NEG_INF, cur)
        rank = jnp.where(hit, lax.convert_element_type(k, F32), rank)
        vals = jnp.where(slot == k, m, vals)
        return cur, rank, vals

    init = (sc, jnp.full((n, L), 64.0, F32), jnp.zeros((P_TOPK, L), F32))
    _, rank, vals = lax.fori_loop(0, P_TOPK, body, init)
    return vals, rank


_CAND_BLOCKS = ((0, 0, 8), (0, 8, 8), (1, 0, 8), (2, 0, 5), (3, 0, 4), (4, 0, 3), (5, 0, 2), (6, 0, 2), (7, 0, 2))


def _pair_select(v1, v2):
    L = v1.shape[1]
    r8 = lax.broadcasted_iota(jnp.int32, (8, L), 0).astype(F32)
    blocks = []
    codes = []
    for a, b0, nb in _CAND_BLOCKS:
        c = v1[a:a + 1] + v2[b0:b0 + 8]
        if nb < 8:
            c = jnp.where(r8 < float(nb), c, NEG_INF)
        blocks.append(c)
        codes.append(r8 + float(a * P_TOPK + b0))
    blocks.append(v1[8:16] + v2[0:1])
    codes.append((r8 + 8.0) * float(P_TOPK))
    cand0 = jnp.concatenate(blocks, axis=0)
    code = jnp.concatenate(codes, axis=0)

    def body(k, carry):
        cur, sel = carry
        m = jnp.max(cur, axis=0, keepdims=True)
        idx = jnp.min(jnp.where(cur == m, code, 1e9), axis=0, keepdims=True)
        hit = code == idx
        return jnp.where(hit, NEG_INF, cur), jnp.where(hit, 1.0, sel)

    _, sel = lax.fori_loop(0, P_TOPK, body, (cand0, jnp.zeros_like(cand0)))
    top = v1[0:1] + v2[0:1]
    z = jnp.sum(jnp.where(sel > 0.0, jnp.exp(cand0 - top), 0.0), axis=0, keepdims=True)
    cnt_lo = jnp.zeros((8, L), F32)
    starts = (0, 16, 24, 32, 40, 48, 56, 64, 72)
    for a in range(8):
        n_a = jnp.sum(sel[starts[a]:starts[a + 1]], axis=0, keepdims=True)
        cnt_lo = jnp.where(r8 == float(a), n_a, cnt_lo)
    count = jnp.concatenate([cnt_lo, sel[72:80]], axis=0)
    return count, z


def _retrieve_exact(s1, s2):
    v1, r1 = _top16(s1)
    v2, r2 = _top16(s2)
    count, z = _pair_select(v1, v2)
    npos = jnp.zeros_like(r1)
    for a in range(P_TOPK):
        npos = jnp.where(r1 == float(a), count[a:a + 1], npos)
    return npos, jnp.exp(s1 - v1[0:1]) / z, r2, jnp.exp(s2 - v2[0:1])


def _oddeven_merge_sort_pairs(n):
    pairs = []
    p = 1
    while p < n:
        k = p
        while k >= 1:
            for j in range(k % p, n - k, 2 * k):
                for i in range(min(k, n - j - k)):
                    if (i + j) // (2 * p) == (i + j + k) // (2 * p):
                        pairs.append((i + j, i + j + k))
            k //= 2
        p *= 2
    return tuple(pairs)


_SORT16 = _oddeven_merge_sort_pairs(P_TOPK)


def _sort16_desc(v):
    for i, j in _SORT16:
        v[i], v[j] = jnp.maximum(v[i], v[j]), jnp.minimum(v[i], v[j])


def _bitonic_merge16_desc(v):
    dist = P_TOPK // 2
    while dist >= 1:
        for i in range(P_TOPK):
            if not i & dist:
                v[i], v[i + dist] = jnp.maximum(v[i], v[i + dist]), jnp.minimum(v[i], v[i + dist])
        dist //= 2


def _merge_top16(a, b):
    v = [jnp.maximum(a[i], b[P_TOPK - 1 - i]) for i in range(P_TOPK)]
    _bitonic_merge16_desc(v)
    return v


def _sorted_top16(s):
    v = [s[8 * k:8 * k + 8, :] for k in range(N_KEYS // 8)]
    _sort16_desc(v)
    for shift in (4, 2, 1):
        v = [jnp.maximum(v[i], pltpu.roll(v[P_TOPK - 1 - i], shift, 0)) for i in range(P_TOPK)]
        _bitonic_merge16_desc(v)
    return v


_ROW_LEN = tuple(P_TOPK // (a + 1) for a in range(P_TOPK))


def _pair_counts(p1, p2):
    rows = [[p1[a] + p2[b] for b in range(_ROW_LEN[a])] for a in range(P_TOPK)]
    ninf = jnp.full_like(p1[0], NEG_INF)
    g0 = list(rows[0])
    g1 = rows[1] + [rows[a][0] for a in range(P_TOPK - 1, 7, -1)]
    _bitonic_merge16_desc(g1)
    g2 = rows[2] + rows[3] + rows[4] + rows[5] + rows[6]
    _sort16_desc(g2)
    g3 = rows[7] + [ninf] * (P_TOPK - len(rows[7]))
    tau = _merge_top16(_merge_top16(g0, g1), _merge_top16(g2, g3))[P_TOPK - 1]
    top = rows[0][0]
    counts = []
    z = jnp.zeros_like(top)
    total = jnp.zeros_like(top)
    for a in range(P_TOPK):
        n = jnp.zeros_like(top)
        for cand in rows[a]:
            ge = cand >= tau
            n = n + jnp.where(ge, 1.0, 0.0)
            z = z + jnp.where(ge, jnp.exp(cand - top), 0.0)
        counts.append(n)
        total = total + n
    return counts, z, jnp.where(total != float(P_TOPK), 1.0, 0.0)


def _retrieve_tie_free(s1_chunks, s2_chunks):
    n_chunks = len(s1_chunks)
    assert n_chunks <= 8
    sub = lax.broadcasted_iota(jnp.int32, (8, LANES), 0)
    blocks = [slice(8 * k, 8 * k + 8) for k in range(N_KEYS // 8)]
    add = lambda p, q: p + q
    rank_sum_distinct = float(sum(range(P_TOPK)) + P_TOPK * (N_KEYS - P_TOPK))

    tops = []
    rank2 = []
    ties = []
    p1 = p2 = None
    for c in range(n_chunks):
        s1, s2 = s1_chunks[c], s2_chunks[c]
        v1 = _sorted_top16(s1)
        v2 = _sorted_top16(s2)
        tops.append((v1, v2))
        tie = jnp.zeros((8, LANES), F32)
        ranks = []
        for blk in blocks:
            r = jnp.zeros((8, LANES), F32)
            for a in range(P_TOPK):
                r = jnp.where(v2[a] > s2[blk], float(a + 1), r)
            ranks.append(r)
        rank2.append(jnp.concatenate(ranks, axis=0))
        rank_sum = jnp.sum(functools.reduce(add, ranks), axis=0, keepdims=True)
        tie = jnp.where(rank_sum != rank_sum_distinct, 1.0, tie)
        for a in range(P_TOPK - 1):
            tie = jnp.where(v1[a] == v1[a + 1], 1.0, tie)
        n_ge = functools.reduce(add, [jnp.where(s1[blk] >= v1[P_TOPK - 1], 1.0, 0.0) for blk in blocks])
        ties.append(jnp.where(jnp.sum(n_ge, axis=0, keepdims=True) != float(P_TOPK), 1.0, tie))
        if c == 0:
            p1, p2 = list(v1), list(v2)
        else:
            p1 = [jnp.where(sub == c, v, p) for v, p in zip(v1, p1)]
            p2 = [jnp.where(sub == c, v, p) for v, p in zip(v2, p2)]

    counts, z, pair_tie = _pair_counts(p1, p2)
    ties = [jnp.maximum(t, pair_tie[c:c + 1, :]) for c, t in enumerate(ties)]

    tables = []
    for c in range(n_chunks):
        s1, s2 = s1_chunks[c], s2_chunks[c]
        v1, v2 = tops[c]
        cnt = [jnp.broadcast_to(n[c:c + 1, :], (8, LANES)) for n in counts] + [jnp.zeros((8, LANES), F32)]
        npos = []
        for blk in blocks:
            n = cnt[0]
            for a in range(P_TOPK):
                n = jnp.where(v1[a] > s1[blk], cnt[a + 1], n)
            npos.append(n)
        tables.append((jnp.concatenate(npos, axis=0), jnp.exp(s1 - v1[0][0:1]) / z[c:c + 1, :],
                       rank2[c], jnp.exp(s2 - v2[0][0:1])))
    return tables, ties


def _dup_bf16_word(x):
    u = lax.bitcast_convert_type(x.astype(BF16).astype(F32), jnp.uint32)
    return lax.bitcast_convert_type(u | (u >> 16), jnp.int32)


def _peer_topk_kernel(x_ref, m_ref, wh_ref, wl_ref, kh_ref, kl_ref,
                      h2t_ref, ni_ref, ai_ref, rj_ref, bj_ref, qt_sc):
    hd = pl.program_id(1)

    @pl.when(hd == 0)
    def _():
        x = x_ref[...]
        h2 = x * (1.0 + m_ref[0, 4:5, :]) + m_ref[0, 3:4, :]
        h_hi, h_lo = _split(h2.T)
        h2t_ref[...] = h_hi
        qt_sc[...] = _mm3(wh_ref[...], wl_ref[...], h_hi, h_lo)

    qh = qt_sc[pl.ds(pl.multiple_of(hd * (2 * P_HALF), 2 * P_HALF), 2 * P_HALF), :]
    q1_hi, q1_lo = _split(qh[:P_HALF])
    q2_hi, q2_lo = _split(qh[P_HALF:])
    sc1 = _mm3(kh_ref[0], kl_ref[0], q1_hi, q1_lo)
    sc2 = _mm3(kh_ref[1], kl_ref[1], q2_hi, q2_lo)
    n_chunks = sc1.shape[1] // LANES
    chunk = [slice(c * LANES, (c + 1) * LANES) for c in range(n_chunks)]

    def emit(c, tables):
        npos, a_i, r2, b_j = tables
        ni_ref[0, :, chunk[c]] = _dup_bf16_word(npos)
        ai_ref[0, :, chunk[c]] = _dup_bf16_word(0.5 * a_i)
        rj_ref[0, :, chunk[c]] = r2.astype(BF16)
        bj_ref[0, :, chunk[c]] = b_j.astype(BF16)

    tables, ties = _retrieve_tie_free([sc1[:, ls] for ls in chunk], [sc2[:, ls] for ls in chunk])
    for c in range(n_chunks):
        emit(c, tables[c])

    def redo(chunks):
        for c in chunks:
            emit(c, _retrieve_exact(sc1[:, chunk[c]], sc2[:, chunk[c]]))

    for g0 in range(0, n_chunks, REDO_GROUP):
        group = range(g0, min(g0 + REDO_GROUP, n_chunks))
        flag = functools.reduce(jnp.maximum, [ties[c] for c in group])
        pl.when(jnp.max(flag) > 0.0)(functools.partial(redo, group))


def _peer_topk(x, mods_l, wq_t_hi, wq_t_lo, keys_hi, keys_lo, *, t_p, s_len):
    t, d = x.shape
    tm = TM_TOPK
    n_p = t_p // tm
    tpb = s_len // tm
    row_tab = jax.ShapeDtypeStruct((P_HEADS, N_KEYS, t), jnp.int32)
    row_spec = pl.BlockSpec((1, N_KEYS, tm), lambda i, h: (h, 0, i))
    col_tab = jax.ShapeDtypeStruct((P_HEADS, N_KEYS, t), BF16)
    col_spec = row_spec
    return pl.pallas_call(
        _peer_topk_kernel,
        grid=(t // tm, P_HEADS),
        in_specs=[
            pl.BlockSpec((tm, d), lambda i, h: (i, 0)),
            pl.BlockSpec((1, 6, d), lambda i, h: (_tile_group(i, n_p, tpb), 0, 0)),
            pl.BlockSpec((d, d), lambda i, h: (0, 0)),
            pl.BlockSpec((d, d), lambda i, h: (0, 0)),
            pl.BlockSpec((2, N_KEYS, P_HALF), lambda i, h: (h, 0, 0)),
            pl.BlockSpec((2, N_KEYS, P_HALF), lambda i, h: (h, 0, 0)),
        ],
        out_specs=[pl.BlockSpec((d, tm), lambda i, h: (0, i)), row_spec, row_spec, col_spec, col_spec],
        out_shape=[jax.ShapeDtypeStruct((d, t), BF16), row_tab, row_tab, col_tab, col_tab],
        scratch_shapes=[pltpu.VMEM((d, tm), F32)],
        compiler_params=_params(("parallel", "arbitrary")),
    )(x, mods_l, wq_t_hi, wq_t_lo, keys_hi, keys_lo)


def _row_tile(row):
    tile = pltpu.bitcast(jnp.broadcast_to(row, (8, row.shape[1])), BF16)
    return jnp.concatenate([tile] * (N_KEYS // tile.shape[0]), axis=0)


GELU_C1 = math.sqrt(2.0 / math.pi)
GELU_C2 = 0.044715 * GELU_C1
K_CHUNK = 512


def _peer_dense_kernel(h2t0_ref, h2tn_ref, eu0_ref, eun_ref, evt_ref, ni_ref, ai_ref, rj_ref, bj_ref,
                       x_ref, m_ref, g_ref, b_ref, y_ref, acc_sc, act0_sc, act1_sc, wt_sc, *, n_e):
    e = pl.program_id(1)

    @pl.when((pl.program_id(0) == 0) & (e == 0))
    def _():
        act0_sc[...] = _mm(eu0_ref[...], h2t0_ref[...]).astype(BF16)

    @pl.when(e == 0)
    def _():
        acc_sc[...] = jnp.zeros_like(acc_sc)

    te, tm = wt_sc.shape
    zero = jnp.zeros((), BF16)

    def after(words, product):
        bits = lax.bitcast_convert_type(product[0:1, :], jnp.int32)
        return words + lax.shift_right_logical(lax.shift_right_logical(bits, 16), 16)

    def step(act_ref, next_ref):
        part = None
        nxt_prev = None
        for kc in range(te // K_CHUNK):
            rows = slice(kc * K_CHUNK, (kc + 1) * K_CHUNK)
            nxt = _mm(eun_ref[rows, :], h2tn_ref[...])
            next_ref[rows, :] = nxt.astype(BF16)
            for il in range(kc * K_CHUNK // N_KEYS, (kc + 1) * K_CHUNK // N_KEYS):
                rs = slice(il * N_KEYS, (il + 1) * N_KEYS)
                gate = None
                for h in range(P_HEADS):
                    cnt_words = ni_ref[h, il:il + 1, :]
                    if nxt_prev is not None and h == 0 and il * N_KEYS == kc * K_CHUNK:
                        cnt_words = after(cnt_words, nxt_prev)
                    cnt = _row_tile(cnt_words)
                    half_a = _row_tile(ai_ref[h, il:il + 1, :])
                    gh = jnp.where(rj_ref[h] < cnt, bj_ref[h], zero) * half_a
                    gate = gh if gate is None else gate + gh
                a = act_ref[rs, :]
                t = jnp.tanh(a * (GELU_C1 + GELU_C2 * (a * a)))
                wt_sc[rs, :] = gate * (a + a * t)
            p = _mm(evt_ref[:, rows], wt_sc[rows, :])
            part = p if part is None else part + p
            nxt_prev = nxt
        acc_sc[...] += part

    @pl.when(e % 2 == 0)
    def _():
        step(act0_sc, act1_sc)

    @pl.when(e % 2 == 1)
    def _():
        step(act1_sc, act0_sc)

    @pl.when(e == n_e - 1)
    def _():
        f = acc_sc[...].T
        z = DN_ALPHA * x_ref[...] + m_ref[0, 5:6, :] * f
        y_ref[...] = _layer_norm(z, g_ref[...], b_ref[...])


def _peer_dense(h2t, e_u, e_vt, layer, ni, ai, rj, bj, x, mods_l, ln_g, ln_b, *, t_p, s_len):
    t, d = x.shape
    tm = TM_PEER
    te = TE_PEER
    n_p = t_p // tm
    tpb = s_len // tm
    n_exp = e_u.shape[1]
    n_e = n_exp // te
    n_i = te // N_KEYS
    n_t = t // tm
    assert n_e % 2 == 0
    nxt_tok = lambda i, e: jnp.minimum(i + (e + 1) // n_e, n_t - 1)
    return pl.pallas_call(
        functools.partial(_peer_dense_kernel, n_e=n_e),
        grid=(n_t, n_e),
        in_specs=[
            pl.BlockSpec((d, tm), lambda i, e: (0, 0)),
            pl.BlockSpec((d, tm), lambda i, e: (0, nxt_tok(i, e))),
            pl.BlockSpec((None, te, d), lambda i, e: (layer, 0, 0)),
            pl.BlockSpec((None, te, d), lambda i, e: (layer, (e + 1) % n_e, 0)),
            pl.BlockSpec((None, d, te), lambda i, e: (layer, 0, e)),
            pl.BlockSpec((P_HEADS, n_i, tm), lambda i, e: (0, e, i)),
            pl.BlockSpec((P_HEADS, n_i, tm), lambda i, e: (0, e, i)),
            pl.BlockSpec((P_HEADS, N_KEYS, tm), lambda i, e: (0, 0, i)),
            pl.BlockSpec((P_HEADS, N_KEYS, tm), lambda i, e: (0, 0, i)),
            pl.BlockSpec((tm, d), lambda i, e: (i, 0)),
            pl.BlockSpec((1, 6, d), lambda i, e: (_tile_group(i, n_p, tpb), 0, 0)),
            pl.BlockSpec((1, d), lambda i, e: (0, 0)),
            pl.BlockSpec((1, d), lambda i, e: (0, 0)),
        ],
        out_specs=pl.BlockSpec((tm, d), lambda i, e: (i, 0)),
        out_shape=jax.ShapeDtypeStruct((t, d), F32),
        scratch_shapes=[pltpu.VMEM((d, tm), F32), pltpu.VMEM((te, tm), BF16), pltpu.VMEM((te, tm), BF16),
                        pltpu.VMEM((te, tm), BF16)],
        compiler_params=_params(("arbitrary", "arbitrary")),
    )(h2t, h2t, e_u, e_u, e_vt, ni, ai, rj, bj, x, mods_l, ln_g.reshape(1, d), ln_b.reshape(1, d))


def _rope_tables(n_tok):
    rows = n_tok // GRID_W
    row = jnp.repeat(jnp.arange(rows), GRID_W).astype(F32)
    col = jnp.tile(jnp.arange(GRID_W), rows).astype(F32)
    n_freq = A_HD // 4
    inv = ROPE_THETA ** (-jnp.arange(n_freq, dtype=F32) / n_freq)
    ang = jnp.concatenate([row[:, None] * inv, col[:, None] * inv], -1)
    cos = jnp.repeat(jnp.cos(ang), 2, axis=-1)
    sin = jnp.repeat(jnp.sin(ang), 2, axis=-1)
    sign = jnp.tile(jnp.asarray([-1.0, 1.0], F32), A_HD // 2)
    reps = LANES // A_HD
    return jnp.tile(cos, (1, reps)), jnp.tile(sin * sign, (1, reps))


def _gqa_weights(w_qkv, q_norm, k_norm, w_o):
    d = w_qkv.shape[0]
    nq = A_HEADS * A_HD
    half = (np.arange(A_HEADS) // A_REP) % 2
    sel = np.stack([half == 0, half == 1], axis=1).astype(np.float32)
    wq = w_qkv[:, :nq].reshape(d, A_HEADS, 1, A_HD) * sel[None, :, :, None]
    w_exp = jnp.concatenate([wq.reshape(d, A_HEADS * LANES), w_qkv[:, nq:]], axis=1).astype(BF16)
    gq = jnp.tile(q_norm, 2 * A_HEADS).reshape(1, -1)
    gk = jnp.tile(k_norm, A_KV).reshape(1, -1)
    wo = w_o.reshape(A_HEADS, 1, A_HD, d) * sel[:, :, None, None]
    return w_exp, gq, gk, wo.reshape(A_HEADS * LANES, d).astype(BF16)


def _diff_weights(w_qkv):
    d = w_qkv.shape[0]
    eye = np.eye(2, dtype=np.float32)
    wq = w_qkv[:, :D_W].reshape(d, D_HEADS, 1, 2, D_HD) * eye[None, None, :, :, None]
    return jnp.concatenate([wq.reshape(d, D_HEADS * 2 * LANES), w_qkv[:, D_W:]], axis=1).astype(BF16)


def _split_f32(w):
    hi = w.astype(BF16)
    return hi, (w - hi.astype(F32)).astype(BF16)


def kernel(x_prompt, x_sample, cache_a_k, cache_a_v, cache_d_k, cache_d_v, c, c_ctx, w_mod, b_mod, ln_g, ln_b, a_w_qkv, a_q_norm, a_k_norm, a_w_o, d_w_qkv, d_lambda_q1, d_lambda_k1, d_lambda_q2, d_lambda_k2, d_sub_norm, d_w_o, g_w_in, g_b_in, g_v_norm_g, g_v_norm_b, g_w_s, g_b_s, g_w_out, p_w_q, p_sub_keys, p_expert_u, p_expert_v):
    batch, seq, d = x_prompt.shape
    dec_batch, dec_seq, _ = x_sample.shape
    past = cache_a_k.shape[2]
    t_p = batch * seq
    t_s = dec_batch * dec_seq
    dims = dict(t_p=t_p, s_len=dec_seq)

    x = (x_prompt.reshape(t_p, d), x_sample.reshape(t_s, d))
    e_u_all = p_expert_u.astype(BF16)
    e_vt_all = jnp.swapaxes(p_expert_v, 1, 2).astype(BF16)
    n_cond = -(-(1 + dec_batch) // 8) * 8
    cond = jnp.zeros((n_cond, d), F32).at[0].set(c_ctx).at[1:1 + dec_batch].set(c)
    mods = _modulation(cond, w_mod, b_mod).reshape(DEPTH, n_cond, 6, d)
    rope_c, rope_s = _rope_tables(dec_seq)

    ak, av, dk, dv = [], [], [], []
    for i in range(DEPTH):
        kind, j = i % N_MIXERS, i // N_MIXERS
        mods_l = mods[i]
        if kind == 0:
            w_exp, gq, gk, wo = _gqa_weights(a_w_qkv[j], a_q_norm[j], a_k_norm[j], a_w_o[j])
            q, k_att, v_att, k_f, v_f = _qkv_project(
                x, mods_l, w_exp, gq, gk, rope_c, rope_s, nq=A_HEADS, nk=A_KV * A_HD // LANES,
                nv=A_KV * A_HD // LANES, use_norm=True, q_scale=A_HD ** -0.5, **dims)
            ctx_k = cache_a_k[:, j].reshape(dec_batch, past, A_KV * A_HD).astype(BF16)
            ctx_v = cache_a_v[:, j].reshape(dec_batch, past, A_KV * A_HD).astype(BF16)
            o = _attention(_gqa_attn_kernel, (), q, k_att, v_att, ctx_k, ctx_v, n_out=A_HEADS * LANES,
                           t_p=t_p, seq_p=seq, batch_s=dec_batch, seq_s=dec_seq)
            ak.append(k_f[:t_p].reshape(batch, seq, A_KV, A_HD))
            av.append(v_f[:t_p].reshape(batch, seq, A_KV, A_HD))
        elif kind == 1:
            lam_init = 0.8 - 0.6 * math.exp(-0.3 * i)
            w_exp = _diff_weights(d_w_qkv[j])
            q, k_att, v_att, k_f, v_f = _qkv_project(
                x, mods_l, w_exp, jnp.ones((1, 2 * D_W), F32), jnp.ones((1, D_W), F32),
                rope_c, rope_s, nq=2 * D_HEADS, nk=D_HEADS, nv=D_HEADS,
                use_norm=False, q_scale=D_HD ** -0.5, **dims)
            ctx_k = cache_d_k[:, j].reshape(dec_batch, past, D_W).astype(BF16)
            ctx_v = cache_d_v[:, j].reshape(dec_batch, past, D_HEADS * D_VD).astype(BF16)
            lvec = jnp.stack([d_lambda_q1[j], d_lambda_k1[j], d_lambda_q2[j], d_lambda_k2[j]])
            kern = functools.partial(_diff_attn_kernel, lam_init=lam_init)
            o = _attention(kern, (lvec, d_sub_norm[j].reshape(1, D_VD)), q, k_att, v_att, ctx_k, ctx_v,
                           n_out=D_HEADS * D_VD, t_p=t_p, seq_p=seq, batch_s=dec_batch, seq_s=dec_seq)
            wo = d_w_o[j].astype(BF16)
            dk.append(k_f[:t_p].reshape(batch, seq, D_HEADS, 2, D_HD))
            dv.append(v_f[:t_p].reshape(batch, seq, D_HEADS, D_VD))
        if kind == 2:
            if isinstance(x, tuple):
                x = jnp.concatenate(x, axis=0)
            x = _gmlp(x, mods_l, g_w_in[j], g_b_in[j], g_v_norm_g[j], g_v_norm_b[j], g_w_s[j], g_b_s[j],
                      g_w_out[j], ln_g[i, 0], ln_b[i, 0], **dims)
        else:
            x = _proj_ln(*o, wo, x, mods_l, ln_g[i, 0], ln_b[i, 0], gate_idx=2, **dims)

        wq_hi, wq_lo = _split_f32(p_w_q[i].T)
        keys = p_sub_keys[i].reshape(P_HEADS * 2, N_KEYS, P_HALF)
        k_hi, k_lo = _split_f32(keys)
        h2t, ni, ai, rj, bj = _peer_topk(x, mods_l, wq_hi, wq_lo, k_hi, k_lo, **dims)
        x = _peer_dense(h2t, e_u_all, e_vt_all, i, ni, ai, rj, bj, x, mods_l, ln_g[i, 1], ln_b[i, 1], **dims)

    y_prompt = x[:t_p].reshape(batch, seq, d)
    y_sample = x[t_p:].reshape(dec_batch, dec_seq, d)
    return (y_prompt, y_sample, jnp.stack(ak, axis=1), jnp.stack(av, axis=1),
            jnp.stack(dk, axis=1), jnp.stack(dv, axis=1))
```

```python
import functools
import math

import numpy as np
import jax
import jax.numpy as jnp
from jax import lax
from jax.experimental import pallas as pl
from jax.experimental.pallas import tpu as pltpu

F32 = jnp.float32
BF16 = jnp.bfloat16

D_MODEL = 1024
DEPTH = 4
GRID_W = 64
N_MIXERS = 3
A_HEADS = 16
A_KV = 4
A_REP = A_HEADS // A_KV
A_HD = 64
D_HEADS = 8
D_HD = 64
D_VD = 2 * D_HD
D_W = D_HEADS * 2 * D_HD
G_DIM = 1024
G_GROUPS = 8
G_CHUNK = 128
P_HEADS = 8
N_KEYS = 128
P_HALF = 64
P_TOPK = 16
ROPE_THETA = 10000.0
DN_ALPHA = (2 * DEPTH) ** 0.25
EPS = 1e-6

LANES = 128
NEG_INF = float("-inf")
VMEM_LIMIT = 56 * 1024 * 1024

TM_QKV = 512
TM_PROJ = 512
TQ_ATTN = 256
TM_GMLP = 256
TM_TOPK = 1024
REDO_GROUP = 4
TM_PEER = 512
TE_PEER = 2048


def _mm(a, b):
    return jnp.dot(a, b, preferred_element_type=F32)


def _mm_nt(a, b):
    return lax.dot_general(a, b, (((1,), (1,)), ((), ())), preferred_element_type=F32)


def _split(x):
    hi = x.astype(BF16)
    lo = (x - hi.astype(F32)).astype(BF16)
    return hi, lo


def _mm3(a_hi, a_lo, b_hi, b_lo):
    return _mm(a_hi, b_hi) + _mm(a_hi, b_lo) + _mm(a_lo, b_hi)


def _layer_norm(z, g, b):
    mu = jnp.mean(z, axis=-1, keepdims=True)
    zc = z - mu
    var = jnp.mean(zc * zc, axis=-1, keepdims=True)
    return zc * lax.rsqrt(var + EPS) * g + b


def _params(sem, flags=None):
    return pltpu.CompilerParams(dimension_semantics=sem, vmem_limit_bytes=VMEM_LIMIT, flags=flags)


def _tile_group(i, n_p_tiles, tiles_per_batch):
    return jnp.where(i < n_p_tiles, 0, 1 + (i - n_p_tiles) // tiles_per_batch)


def _tile_pos_block(i, n_p_tiles, tiles_per_batch):
    return jnp.where(i < n_p_tiles, 0, (i - n_p_tiles) % tiles_per_batch)


def _mods_kernel(c_ref, w_ref, b_ref, o_ref):
    c = c_ref[...]
    a = c * jax.nn.sigmoid(c)
    a_hi, a_lo = _split(a)
    w_hi, w_lo = _split(w_ref[0])
    o_ref[0] = _mm3(a_hi, a_lo, w_hi, w_lo) + b_ref[0]


def _modulation(cond, w_mod, b_mod):
    depth, d, n = w_mod.shape
    rows = cond.shape[0]
    tn = 1536
    return pl.pallas_call(
        _mods_kernel,
        grid=(depth, n // tn),
        in_specs=[
            pl.BlockSpec((rows, d), lambda l, j: (0, 0)),
            pl.BlockSpec((1, d, tn), lambda l, j: (l, 0, j)),
            pl.BlockSpec((1, 1, tn), lambda l, j: (l, 0, j)),
        ],
        out_specs=pl.BlockSpec((1, rows, tn), lambda l, j: (l, 0, j)),
        out_shape=jax.ShapeDtypeStruct((depth, rows, n), F32),
        compiler_params=_params(("parallel", "parallel")),
    )(cond, w_mod, b_mod.reshape(depth, 1, n))


def _swap_pairs(x):
    lane = lax.broadcasted_iota(jnp.int32, x.shape, 1)
    nxt = pltpu.roll(x, LANES - 1, 1)
    prv = pltpu.roll(x, 1, 1)
    return jnp.where((lane & 1) == 0, nxt, prv)


def _x_operands(x, tm, n_p):
    if isinstance(x, tuple):
        xp, xs = x
        shift, lo, t = n_p, 0, xp.shape[0] + xs.shape[0]
    else:
        xp = xs = x
        shift, lo, t = 0, n_p, x.shape[0]
    d = xp.shape[1]
    specs = [pl.BlockSpec((tm, d), lambda i: (jnp.minimum(i, n_p - 1), 0)),
             pl.BlockSpec((tm, d), lambda i: (jnp.maximum(i - shift, lo), 0))]
    return (xp, xs), specs, t


def _select_x(xp_ref, xs_ref, n_p_tiles):
    return jnp.where(pl.program_id(0) < n_p_tiles, xp_ref[...], xs_ref[...])


def _qkv_kernel(xp_ref, xs_ref, m_ref, w_ref, bd_ref, gq_ref, gk_ref, cs_ref, sn_ref,
                q_ref, ka_ref, va_ref, kf_ref, vf_ref,
                *, nq, nk, nv, use_norm, n_p_tiles, q_scale):
    x = _select_x(xp_ref, xs_ref, n_p_tiles)
    h = x * (1.0 + m_ref[0, 1:2, :]) + m_ref[0, 0:1, :]
    y = _mm(h.astype(BF16), w_ref[...])
    is_lat = pl.program_id(0) >= n_p_tiles
    cs = jnp.where(is_lat, cs_ref[...], 1.0)
    sn = jnp.where(is_lat, sn_ref[...], 0.0)
    bd = bd_ref[...]

    def head_norm(yg, gain):
        s_hi, s_lo = _split(yg * yg)
        ms = _mm(s_hi, bd) + _mm(s_lo, bd)
        return yg * lax.rsqrt(ms + EPS) * gain

    def rope(yg):
        return yg * cs + _swap_pairs(yg) * sn

    for g in range(nq):
        sl = slice(g * LANES, (g + 1) * LANES)
        yg = y[:, sl]
        if use_norm:
            yg = head_norm(yg, gq_ref[:, sl])
        q_ref[:, sl] = (rope(yg) * q_scale).astype(BF16)
    for g in range(nk):
        sl = slice(g * LANES, (g + 1) * LANES)
        yg = y[:, (nq + g) * LANES:(nq + g + 1) * LANES]
        if use_norm:
            yg = head_norm(yg, gk_ref[:, sl])
        kf_ref[:, sl] = yg
        ka_ref[:, sl] = rope(yg).astype(BF16)
    for g in range(nv):
        sl = slice(g * LANES, (g + 1) * LANES)
        yg = y[:, (nq + nk + g) * LANES:(nq + nk + g + 1) * LANES]
        vf_ref[:, sl] = yg
        va_ref[:, sl] = yg.astype(BF16)


def _qkv_project(x, mods_l, w_exp, gq, gk, rope_c, rope_s, *, nq, nk, nv, use_norm, t_p, s_len, q_scale):
    tm = TM_QKV
    n_p = t_p // tm
    tpb = s_len // tm
    x_arrays, x_specs, t = _x_operands(x, tm, n_p)
    d = x_arrays[0].shape[1]
    ncol = (nq + nk + nv) * LANES
    bd = np.kron(np.eye(2, dtype=np.float32), np.full((64, 64), 1.0 / 64.0, np.float32))
    bd = jnp.asarray(bd, BF16)
    grp = lambda i: (_tile_group(i, n_p, tpb), 0, 0)
    pos = lambda i: (_tile_pos_block(i, n_p, tpb), 0)
    row = lambda i: (i, 0)
    fixed = lambda i: (0, 0)
    kern = functools.partial(_qkv_kernel, nq=nq, nk=nk, nv=nv, use_norm=use_norm,
                             n_p_tiles=n_p, q_scale=q_scale)
    return pl.pallas_call(
        kern,
        grid=(t // tm,),
        in_specs=x_specs + [
            pl.BlockSpec((1, 6, d), grp),
            pl.BlockSpec((d, ncol), fixed),
            pl.BlockSpec((LANES, LANES), fixed),
            pl.BlockSpec((1, nq * LANES), fixed),
            pl.BlockSpec((1, nk * LANES), fixed),
            pl.BlockSpec((tm, LANES), pos),
            pl.BlockSpec((tm, LANES), pos),
        ],
        out_specs=[
            pl.BlockSpec((tm, nq * LANES), row),
            pl.BlockSpec((tm, nk * LANES), row),
            pl.BlockSpec((tm, nv * LANES), row),
            pl.BlockSpec((tm, nk * LANES), row),
            pl.BlockSpec((tm, nv * LANES), row),
        ],
        out_shape=[
            jax.ShapeDtypeStruct((t, nq * LANES), BF16),
            jax.ShapeDtypeStruct((t, nk * LANES), BF16),
            jax.ShapeDtypeStruct((t, nv * LANES), BF16),
            jax.ShapeDtypeStruct((t, nk * LANES), F32),
            jax.ShapeDtypeStruct((t, nv * LANES), F32),
        ],
        compiler_params=_params(("parallel",)),
    )(*x_arrays, mods_l, w_exp, bd, gq, gk, rope_c, rope_s)


def _softmax_pv(qh, k_all, v_all):
    scores = [_mm_nt(qh, k) for k in k_all]
    m = scores[0].max(axis=-1, keepdims=True)
    for s in scores[1:]:
        m = jnp.maximum(m, s.max(axis=-1, keepdims=True))
    o = None
    l = None
    for s, v in zip(scores, v_all):
        p = jnp.exp(s - m)
        ls = p.sum(axis=-1, keepdims=True)
        os_ = _mm(p.astype(BF16), v)
        o = os_ if o is None else o + os_
        l = ls if l is None else l + ls
    return o, l


def _gqa_attn_kernel(*refs, has_ctx):
    if has_ctx:
        q_ref, k_ref, v_ref, kc_ref, vc_ref, o_ref = refs
    else:
        q_ref, k_ref, v_ref, o_ref = refs
    tq = q_ref.shape[0]
    for g in range(A_KV):
        ks = slice((g // 2) * LANES, (g // 2 + 1) * LANES)
        heads = [slice((g * A_REP + r) * LANES, (g * A_REP + r + 1) * LANES) for r in range(A_REP)]
        k_all = [k_ref[:, ks]]
        v_all = [v_ref[:, ks]]
        if has_ctx:
            k_all.append(kc_ref[0, :, ks])
            v_all.append(vc_ref[0, :, ks])
        o, l = _softmax_pv(jnp.concatenate([q_ref[:, sl] for sl in heads], axis=0), k_all, v_all)
        o = (o / l).astype(BF16)
        for r, sl in enumerate(heads):
            o_ref[:, sl] = o[r * tq:(r + 1) * tq, :]


def _diff_attn_kernel(*refs, has_ctx, lam_init):
    if has_ctx:
        lv_ref, sg_ref, q_ref, k_ref, v_ref, kc_ref, vc_ref, o_ref = refs
    else:
        lv_ref, sg_ref, q_ref, k_ref, v_ref, o_ref = refs
    lv = lv_ref[...]
    lam = (jnp.exp(jnp.sum(lv[0:1] * lv[1:2], axis=-1, keepdims=True))
           - jnp.exp(jnp.sum(lv[2:3] * lv[3:4], axis=-1, keepdims=True)) + lam_init)
    for h in range(D_HEADS):
        ks = slice(h * LANES, (h + 1) * LANES)
        k_all = [k_ref[:, ks]]
        v_all = [v_ref[:, ks]]
        if has_ctx:
            k_all.append(kc_ref[0, :, ks])
            v_all.append(vc_ref[0, :, ks])
        tq = q_ref.shape[0]
        q2 = jnp.concatenate([q_ref[:, (2 * h + j) * LANES:(2 * h + j + 1) * LANES] for j in range(2)], axis=0)
        o, l = _softmax_pv(q2, k_all, v_all)
        o = o / l
        o = o[:tq] - lam * o[tq:]
        ms = jnp.mean(o * o, axis=-1, keepdims=True)
        o = o * lax.rsqrt(ms + EPS) * sg_ref[...] * (1.0 - lam_init)
        o_ref[:, ks] = o.astype(BF16)


def _attention(kern, extra, q, k, v, ctx_k, ctx_v, *, n_out, t_p, seq_p, batch_s, seq_s):
    nqc = q.shape[1]
    nkc = k.shape[1]
    extra_specs = [pl.BlockSpec(a.shape, lambda *_: (0, 0)) for a in extra]
    n_seq_p = t_p // seq_p
    o_p = pl.pallas_call(
        functools.partial(kern, has_ctx=False),
        grid=(n_seq_p,),
        in_specs=extra_specs + [
            pl.BlockSpec((seq_p, nqc), lambda b: (b, 0)),
            pl.BlockSpec((seq_p, nkc), lambda b: (b, 0)),
            pl.BlockSpec((seq_p, nkc), lambda b: (b, 0)),
        ],
        out_specs=pl.BlockSpec((seq_p, n_out), lambda b: (b, 0)),
        out_shape=jax.ShapeDtypeStruct((t_p, n_out), BF16),
        compiler_params=_params(("parallel",)),
    )(*extra, q, k, v)
    tq = TQ_ATTN
    nqt = seq_s // tq
    assert t_p % seq_s == 0 and t_p % tq == 0
    q_off = t_p // tq
    k_off = t_p // seq_s
    past = ctx_k.shape[1]
    o_s = pl.pallas_call(
        functools.partial(kern, has_ctx=True),
        grid=(batch_s, nqt),
        in_specs=extra_specs + [
            pl.BlockSpec((tq, nqc), lambda b, i: (q_off + b * nqt + i, 0)),
            pl.BlockSpec((seq_s, nkc), lambda b, i: (k_off + b, 0)),
            pl.BlockSpec((seq_s, nkc), lambda b, i: (k_off + b, 0)),
            pl.BlockSpec((1, past, nkc), lambda b, i: (b, 0, 0)),
            pl.BlockSpec((1, past, nkc), lambda b, i: (b, 0, 0)),
        ],
        out_specs=pl.BlockSpec((tq, n_out), lambda b, i: (b * nqt + i, 0)),
        out_shape=jax.ShapeDtypeStruct((batch_s * seq_s, n_out), BF16),
        compiler_params=_params(("parallel", "arbitrary")),
    )(*extra, q, k, v, ctx_k, ctx_v)
    return o_p, o_s


def _proj_ln_kernel(op_ref, os_ref, w_ref, xp_ref, xs_ref, m_ref, g_ref, b_ref, y_ref, *, gate_idx, n_p_tiles):
    o = _select_x(op_ref, os_ref, n_p_tiles)
    y = _mm(o, w_ref[...])
    z = DN_ALPHA * _select_x(xp_ref, xs_ref, n_p_tiles) + m_ref[0, gate_idx:gate_idx + 1, :] * y
    y_ref[...] = _layer_norm(z, g_ref[...], b_ref[...])


def _proj_ln(o_p, o_s, w, x, mods_l, ln_g, ln_b, *, gate_idx, t_p, s_len):
    tm = TM_PROJ
    n_p = t_p // tm
    tpb = s_len // tm
    x_arrays, x_specs, t = _x_operands(x, tm, n_p)
    d = x_arrays[0].shape[1]
    kin = o_p.shape[1]
    row = lambda i: (i, 0)
    fixed = lambda i: (0, 0)
    return pl.pallas_call(
        functools.partial(_proj_ln_kernel, gate_idx=gate_idx, n_p_tiles=n_p),
        grid=(t // tm,),
        in_specs=[
            pl.BlockSpec((tm, kin), lambda i: (jnp.minimum(i, n_p - 1), 0)),
            pl.BlockSpec((tm, kin), lambda i: (jnp.maximum(i - n_p, 0), 0)),
            pl.BlockSpec((kin, d), fixed),
        ] + x_specs + [
            pl.BlockSpec((1, 6, d), lambda i: (_tile_group(i, n_p, tpb), 0, 0)),
            pl.BlockSpec((1, d), fixed),
            pl.BlockSpec((1, d), fixed),
        ],
        out_specs=pl.BlockSpec((tm, d), row),
        out_shape=jax.ShapeDtypeStruct((t, d), F32),
        compiler_params=_params(("parallel",)),
    )(o_p, o_s, w, *x_arrays, mods_l, ln_g.reshape(1, d), ln_b.reshape(1, d))


def _gmlp_kernel(x_ref, m_ref, win_ref, bin_ref, vg_ref, vb_ref, ws_ref, bs_ref, wout_ref,
                 g_ref, b_ref, y_ref, uv_sc):
    x = x_ref[...]
    h = x * (1.0 + m_ref[0, 1:2, :]) + m_ref[0, 0:1, :]
    z = jax.nn.gelu(_mm(h.astype(BF16), win_ref[...]) + bin_ref[...])
    u = z[:, :G_DIM]
    v = _layer_norm(z[:, G_DIM:], vg_ref[...], vb_ref[...])
    tm = x.shape[0]
    for c in range(tm // G_CHUNK):
        rs = slice(c * G_CHUNK, (c + 1) * G_CHUNK)
        for g in range(G_GROUPS):
            cs = slice(g * LANES, (g + 1) * LANES)
            vm = _mm(ws_ref[g], v[rs, cs].astype(BF16)) + bs_ref[g]
            uv_sc[rs, cs] = (u[rs, cs] * vm).astype(BF16)
    y = _mm(uv_sc[...], wout_ref[...])
    zz = DN_ALPHA * x + m_ref[0, 2:3, :] * y
    y_ref[...] = _layer_norm(zz, g_ref[...], b_ref[...])


def _gmlp(x, mods_l, w_in, b_in, vg, vb, w_s, b_s, w_out, ln_g, ln_b, *, t_p, s_len):
    t, d = x.shape
    tm = TM_GMLP
    n_p = t_p // tm
    tpb = s_len // tm
    row = lambda i: (i, 0)
    fixed = lambda i: (0, 0)
    fixed3 = lambda i: (0, 0, 0)
    b_s_b = jnp.broadcast_to(b_s[:, :, None], (G_GROUPS, G_CHUNK, LANES))
    return pl.pallas_call(
        _gmlp_kernel,
        grid=(t // tm,),
        in_specs=[
            pl.BlockSpec((tm, d), row),
            pl.BlockSpec((1, 6, d), lambda i: (_tile_group(i, n_p, tpb), 0, 0)),
            pl.BlockSpec((d, 2 * G_DIM), fixed),
            pl.BlockSpec((1, 2 * G_DIM), fixed),
            pl.BlockSpec((1, G_DIM), fixed),
            pl.BlockSpec((1, G_DIM), fixed),
            pl.BlockSpec((G_GROUPS, G_CHUNK, G_CHUNK), fixed3),
            pl.BlockSpec((G_GROUPS, G_CHUNK, LANES), fixed3),
            pl.BlockSpec((G_DIM, d), fixed),
            pl.BlockSpec((1, d), fixed),
            pl.BlockSpec((1, d), fixed),
        ],
        out_specs=pl.BlockSpec((tm, d), row),
        out_shape=jax.ShapeDtypeStruct((t, d), F32),
        scratch_shapes=[pltpu.VMEM((tm, G_DIM), BF16)],
        compiler_params=_params(("parallel",)),
    )(x, mods_l, w_in.astype(BF16), b_in.reshape(1, -1), vg.reshape(1, -1), vb.reshape(1, -1),
      w_s.astype(BF16), b_s_b, w_out.astype(BF16), ln_g.reshape(1, d), ln_b.reshape(1, d))


def _top16(sc):
    n, L = sc.shape
    pos = lax.broadcasted_iota(jnp.int32, (n, L), 0).astype(F32)
    slot = lax.broadcasted_iota(jnp.int32, (P_TOPK, L), 0)

    def body(k, carry):
        cur, rank, vals = carry
        m = jnp.max(cur, axis=0, keepdims=True)
        idx = jnp.min(jnp.where(cur == m, pos, float(n)), axis=0, keepdims=True)
        hit = pos == idx
        cur = jnp.where(hit, NEG_INF, cur)
        rank = jnp.where(hit, lax.convert_element_type(k, F32), rank)
        vals = jnp.where(slot == k, m, vals)
        return cur, rank, vals

    init = (sc, jnp.full((n, L), 64.0, F32), jnp.zeros((P_TOPK, L), F32))
    _, rank, vals = lax.fori_loop(0, P_TOPK, body, init)
    return vals, rank


_CAND_BLOCKS = ((0, 0, 8), (0, 8, 8), (1, 0, 8), (2, 0, 5), (3, 0, 4), (4, 0, 3), (5, 0, 2), (6, 0, 2), (7, 0, 2))


def _pair_select(v1, v2):
    L = v1.shape[1]
    r8 = lax.broadcasted_iota(jnp.int32, (8, L), 0).astype(F32)
    blocks = []
    codes = []
    for a, b0, nb in _CAND_BLOCKS:
        c = v1[a:a + 1] + v2[b0:b0 + 8]
        if nb < 8:
            c = jnp.where(r8 < float(nb), c, NEG_INF)
        blocks.append(c)
        codes.append(r8 + float(a * P_TOPK + b0))
    blocks.append(v1[8:16] + v2[0:1])
    codes.append((r8 + 8.0) * float(P_TOPK))
    cand0 = jnp.concatenate(blocks, axis=0)
    code = jnp.concatenate(codes, axis=0)

    def body(k, carry):
        cur, sel = carry
        m = jnp.max(cur, axis=0, keepdims=True)
        idx = jnp.min(jnp.where(cur == m, code, 1e9), axis=0, keepdims=True)
        hit = code == idx
        return jnp.where(hit, NEG_INF, cur), jnp.where(hit, 1.0, sel)

    _, sel = lax.fori_loop(0, P_TOPK, body, (cand0, jnp.zeros_like(cand0)))
    top = v1[0:1] + v2[0:1]
    z = jnp.sum(jnp.where(sel > 0.0, jnp.exp(cand0 - top), 0.0), axis=0, keepdims=True)
    cnt_lo = jnp.zeros((8, L), F32)
    starts = (0, 16, 24, 32, 40, 48, 56, 64, 72)
    for a in range(8):
        n_a = jnp.sum(sel[starts[a]:starts[a + 1]], axis=0, keepdims=True)
        cnt_lo = jnp.where(r8 == float(a), n_a, cnt_lo)
    count = jnp.concatenate([cnt_lo, sel[72:80]], axis=0)
    return count, z


def _retrieve_exact(s1, s2):
    v1, r1 = _top16(s1)
    v2, r2 = _top16(s2)
    count, z = _pair_select(v1, v2)
    npos = jnp.zeros_like(r1)
    for a in range(P_TOPK):
        npos = jnp.where(r1 == float(a), count[a:a + 1], npos)
    return npos, jnp.exp(s1 - v1[0:1]) / z, r2, jnp.exp(s2 - v2[0:1])


def _oddeven_merge_sort_pairs(n):
    pairs = []
    p = 1
    while p < n:
        k = p
        while k >= 1:
            for j in range(k % p, n - k, 2 * k):
                for i in range(min(k, n - j - k)):
                    if (i + j) // (2 * p) == (i + j + k) // (2 * p):
                        pairs.append((i + j, i + j + k))
            k //= 2
        p *= 2
    return tuple(pairs)


_SORT16 = _oddeven_merge_sort_pairs(P_TOPK)


def _sort16_desc(v):
    for i, j in _SORT16:
        v[i], v[j] = jnp.maximum(v[i], v[j]), jnp.minimum(v[i], v[j])


def _bitonic_merge16_desc(v):
    dist = P_TOPK // 2
    while dist >= 1:
        for i in range(P_TOPK):
            if not i & dist:
                v[i], v[i + dist] = jnp.maximum(v[i], v[i + dist]), jnp.minimum(v[i], v[i + dist])
        dist //= 2


def _merge_top16(a, b):
    v = [jnp.maximum(a[i], b[P_TOPK - 1 - i]) for i in range(P_TOPK)]
    _bitonic_merge16_desc(v)
    return v


def _sorted_top16(s):
    v = [s[8 * k:8 * k + 8, :] for k in range(N_KEYS // 8)]
    _sort16_desc(v)
    for shift in (4, 2, 1):
        v = [jnp.maximum(v[i], pltpu.roll(v[P_TOPK - 1 - i], shift, 0)) for i in range(P_TOPK)]
        _bitonic_merge16_desc(v)
    return v


_ROW_LEN = tuple(P_TOPK // (a + 1) for a in range(P_TOPK))


def _pair_counts(p1, p2):
    rows = [[p1[a] + p2[b] for b in range(_ROW_LEN[a])] for a in range(P_TOPK)]
    ninf = jnp.full_like(p1[0], NEG_INF)
    g0 = list(rows[0])
    g1 = rows[1] + [rows[a][0] for a in range(P_TOPK - 1, 7, -1)]
    _bitonic_merge16_desc(g1)
    g2 = rows[2] + rows[3] + rows[4] + rows[5] + rows[6]
    _sort16_desc(g2)
    g3 = rows[7] + [ninf] * (P_TOPK - len(rows[7]))
    tau = _merge_top16(_merge_top16(g0, g1), _merge_top16(g2, g3))[P_TOPK - 1]
    top = rows[0][0]
    counts = []
    z = jnp.zeros_like(top)
    total = jnp.zeros_like(top)
    for a in range(P_TOPK):
        n = jnp.zeros_like(top)
        for cand in rows[a]:
            ge = cand >= tau
            n = n + jnp.where(ge, 1.0, 0.0)
            z = z + jnp.where(ge, jnp.exp(cand - top), 0.0)
        counts.append(n)
        total = total + n
    return counts, z, jnp.where(total != float(P_TOPK), 1.0, 0.0)


def _retrieve_tie_free(s1_chunks, s2_chunks):
    n_chunks = len(s1_chunks)
    assert n_chunks <= 8
    sub = lax.broadcasted_iota(jnp.int32, (8, LANES), 0)
    blocks = [slice(8 * k, 8 * k + 8) for k in range(N_KEYS // 8)]
    add = lambda p, q: p + q
    rank_sum_distinct = float(sum(range(P_TOPK)) + P_TOPK * (N_KEYS - P_TOPK))

    tops = []
    rank2 = []
    ties = []
    p1 = p2 = None
    for c in range(n_chunks):
        s1, s2 = s1_chunks[c], s2_chunks[c]
        v1 = _sorted_top16(s1)
        v2 = _sorted_top16(s2)
        tops.append((v1, v2))
        tie = jnp.zeros((8, LANES), F32)
        ranks = []
        for blk in blocks:
            r = jnp.zeros((8, LANES), F32)
            for a in range(P_TOPK):
                r = jnp.where(v2[a] > s2[blk], float(a + 1), r)
            ranks.append(r)
        rank2.append(jnp.concatenate(ranks, axis=0))
        rank_sum = jnp.sum(functools.reduce(add, ranks), axis=0, keepdims=True)
        tie = jnp.where(rank_sum != rank_sum_distinct, 1.0, tie)
        for a in range(P_TOPK - 1):
            tie = jnp.where(v1[a] == v1[a + 1], 1.0, tie)
        n_ge = functools.reduce(add, [jnp.where(s1[blk] >= v1[P_TOPK - 1], 1.0, 0.0) for blk in blocks])
        ties.append(jnp.where(jnp.sum(n_ge, axis=0, keepdims=True) != float(P_TOPK), 1.0, tie))
        if c == 0:
            p1, p2 = list(v1), list(v2)
        else:
            p1 = [jnp.where(sub == c, v, p) for v, p in zip(v1, p1)]
            p2 = [jnp.where(sub == c, v, p) for v, p in zip(v2, p2)]

    counts, z, pair_tie = _pair_counts(p1, p2)
    ties = [jnp.maximum(t, pair_tie[c:c + 1, :]) for c, t in enumerate(ties)]

    tables = []
    for c in range(n_chunks):
        s1, s2 = s1_chunks[c], s2_chunks[c]
        v1, v2 = tops[c]
        cnt = [jnp.broadcast_to(n[c:c + 1, :], (8, LANES)) for n in counts] + [jnp.zeros((8, LANES), F32)]
        npos = []
        for blk in blocks:
            n = cnt[0]
            for a in range(P_TOPK):
                n = jnp.where(v1[a] > s1[blk], cnt[a + 1], n)
            npos.append(n)
        tables.append((jnp.concatenate(npos, axis=0), jnp.exp(s1 - v1[0][0:1]) / z[c:c + 1, :],
                       rank2[c], jnp.exp(s2 - v2[0][0:1])))
    return tables, ties


def _dup_bf16_word(x):
    u = lax.bitcast_convert_type(x.astype(BF16).astype(F32), jnp.uint32)
    return lax.bitcast_convert_type(u | (u >> 16), jnp.int32)


def _peer_topk_kernel(x_ref, m_ref, wh_ref, wl_ref, kh_ref, kl_ref,
                      h2t_ref, ni_ref, ai_ref, rj_ref, bj_ref, qt_sc):
    hd = pl.program_id(1)

    @pl.when(hd == 0)
    def _():
        x = x_ref[...]
        h2 = x * (1.0 + m_ref[0, 4:5, :]) + m_ref[0, 3:4, :]
        h_hi, h_lo = _split(h2.T)
        h2t_ref[...] = h_hi
        qt_sc[...] = _mm3(wh_ref[...], wl_ref[...], h_hi, h_lo)

    qh = qt_sc[pl.ds(pl.multiple_of(hd * (2 * P_HALF), 2 * P_HALF), 2 * P_HALF), :]
    q1_hi, q1_lo = _split(qh[:P_HALF])
    q2_hi, q2_lo = _split(qh[P_HALF:])
    sc1 = _mm3(kh_ref[0], kl_ref[0], q1_hi, q1_lo)
    sc2 = _mm3(kh_ref[1], kl_ref[1], q2_hi, q2_lo)
    n_chunks = sc1.shape[1] // LANES
    chunk = [slice(c * LANES, (c + 1) * LANES) for c in range(n_chunks)]

    def emit(c, tables):
        npos, a_i, r2, b_j = tables
        ni_ref[0, :, chunk[c]] = _dup_bf16_word(npos)
        ai_ref[0, :, chunk[c]] = _dup_bf16_word(0.5 * a_i)
        rj_ref[0, :, chunk[c]] = r2.astype(BF16)
        bj_ref[0, :, chunk[c]] = b_j.astype(BF16)

    tables, ties = _retrieve_tie_free([sc1[:, ls] for ls in chunk], [sc2[:, ls] for ls in chunk])
    for c in range(n_chunks):
        emit(c, tables[c])

    def redo(chunks):
        for c in chunks:
            emit(c, _retrieve_exact(sc1[:, chunk[c]], sc2[:, chunk[c]]))

    for g0 in range(0, n_chunks, REDO_GROUP):
        group = range(g0, min(g0 + REDO_GROUP, n_chunks))
        flag = functools.reduce(jnp.maximum, [ties[c] for c in group])
        pl.when(jnp.max(flag) > 0.0)(functools.partial(redo, group))


def _peer_topk(x, mods_l, wq_t_hi, wq_t_lo, keys_hi, keys_lo, *, t_p, s_len):
    t, d = x.shape
    tm = TM_TOPK
    n_p = t_p // tm
    tpb = s_len // tm
    row_tab = jax.ShapeDtypeStruct((P_HEADS, N_KEYS, t), jnp.int32)
    row_spec = pl.BlockSpec((1, N_KEYS, tm), lambda i, h: (h, 0, i))
    col_tab = jax.ShapeDtypeStruct((P_HEADS, N_KEYS, t), BF16)
    col_spec = row_spec
    return pl.pallas_call(
        _peer_topk_kernel,
        grid=(t // tm, P_HEADS),
        in_specs=[
            pl.BlockSpec((tm, d), lambda i, h: (i, 0)),
            pl.BlockSpec((1, 6, d), lambda i, h: (_tile_group(i, n_p, tpb), 0, 0)),
            pl.BlockSpec((d, d), lambda i, h: (0, 0)),
            pl.BlockSpec((d, d), lambda i, h: (0, 0)),
            pl.BlockSpec((2, N_KEYS, P_HALF), lambda i, h: (h, 0, 0)),
            pl.BlockSpec((2, N_KEYS, P_HALF), lambda i, h: (h, 0, 0)),
        ],
        out_specs=[pl.BlockSpec((d, tm), lambda i, h: (0, i)), row_spec, row_spec, col_spec, col_spec],
        out_shape=[jax.ShapeDtypeStruct((d, t), BF16), row_tab, row_tab, col_tab, col_tab],
        scratch_shapes=[pltpu.VMEM((d, tm), F32)],
        compiler_params=_params(("parallel", "arbitrary")),
    )(x, mods_l, wq_t_hi, wq_t_lo, keys_hi, keys_lo)


def _row_tile(row):
    tile = pltpu.bitcast(jnp.broadcast_to(row, (8, row.shape[1])), BF16)
    return jnp.concatenate([tile] * (N_KEYS // tile.shape[0]), axis=0)


GELU_C1 = math.sqrt(2.0 / math.pi)
GELU_C2 = 0.044715 * GELU_C1
K_CHUNK = 512


def _peer_dense_kernel(h2t0_ref, h2tn_ref, eu0_ref, eun_ref, evt_ref, ni_ref, ai_ref, rj_ref, bj_ref,
                       x_ref, m_ref, g_ref, b_ref, y_ref, acc_sc, act0_sc, act1_sc, wt_sc, *, n_e):
    e = pl.program_id(1)

    @pl.when((pl.program_id(0) == 0) & (e == 0))
    def _():
        act0_sc[...] = _mm(eu0_ref[...], h2t0_ref[...]).astype(BF16)

    @pl.when(e == 0)
    def _():
        acc_sc[...] = jnp.zeros_like(acc_sc)

    te, tm = wt_sc.shape
    zero = jnp.zeros((), BF16)

    def after(words, product):
        bits = lax.bitcast_convert_type(product[0:1, :], jnp.int32)
        return words + lax.shift_right_logical(lax.shift_right_logical(bits, 16), 16)

    def step(act_ref, next_ref):
        part = None
        nxt_prev = None
        for kc in range(te // K_CHUNK):
            rows = slice(kc * K_CHUNK, (kc + 1) * K_CHUNK)
            nxt = _mm(eun_ref[rows, :], h2tn_ref[...])
            next_ref[rows, :] = nxt.astype(BF16)
            for il in range(kc * K_CHUNK // N_KEYS, (kc + 1) * K_CHUNK // N_KEYS):
                rs = slice(il * N_KEYS, (il + 1) * N_KEYS)
                gate = None
                for h in range(P_HEADS):
                    cnt_words = ni_ref[h, il:il + 1, :]
                    if nxt_prev is not None and h == 0 and il * N_KEYS == kc * K_CHUNK:
                        cnt_words = after(cnt_words, nxt_prev)
                    cnt = _row_tile(cnt_words)
                    half_a = _row_tile(ai_ref[h, il:il + 1, :])
                    gh = jnp.where(rj_ref[h] < cnt, bj_ref[h], zero) * half_a
                    gate = gh if gate is None else gate + gh
                a = act_ref[rs, :]
                t = jnp.tanh(a * (GELU_C1 + GELU_C2 * (a * a)))
                wt_sc[rs, :] = gate * (a + a * t)
            p = _mm(evt_ref[:, rows], wt_sc[rows, :])
            part = p if part is None else part + p
            nxt_prev = nxt
        acc_sc[...] += part

    @pl.when(e % 2 == 0)
    def _():
        step(act0_sc, act1_sc)

    @pl.when(e % 2 == 1)
    def _():
        step(act1_sc, act0_sc)

    @pl.when(e == n_e - 1)
    def _():
        f = acc_sc[...].T
        z = DN_ALPHA * x_ref[...] + m_ref[0, 5:6, :] * f
        y_ref[...] = _layer_norm(z, g_ref[...], b_ref[...])


def _peer_dense(h2t, e_u, e_vt, layer, ni, ai, rj, bj, x, mods_l, ln_g, ln_b, *, t_p, s_len):
    t, d = x.shape
    tm = TM_PEER
    te = TE_PEER
    n_p = t_p // tm
    tpb = s_len // tm
    n_exp = e_u.shape[1]
    n_e = n_exp // te
    n_i = te // N_KEYS
    n_t = t // tm
    assert n_e % 2 == 0
    nxt_tok = lambda i, e: jnp.minimum(i + (e + 1) // n_e, n_t - 1)
    return pl.pallas_call(
        functools.partial(_peer_dense_kernel, n_e=n_e),
        grid=(n_t, n_e),
        in_specs=[
            pl.BlockSpec((d, tm), lambda i, e: (0, 0), pipeline_mode=pl.Buffered(1)),
            pl.BlockSpec((d, tm), lambda i, e: (0, nxt_tok(i, e))),
            pl.BlockSpec((None, te, d), lambda i, e: (layer, 0, 0), pipeline_mode=pl.Buffered(1)),
            pl.BlockSpec((None, te, d), lambda i, e: (layer, (e + 1) % n_e, 0)),
            pl.BlockSpec((None, d, te), lambda i, e: (layer, 0, e)),
            pl.BlockSpec((P_HEADS, n_i, tm), lambda i, e: (0, e, i)),
            pl.BlockSpec((P_HEADS, n_i, tm), lambda i, e: (0, e, i)),
            pl.BlockSpec((P_HEADS, N_KEYS, tm), lambda i, e: (0, 0, i)),
            pl.BlockSpec((P_HEADS, N_KEYS, tm), lambda i, e: (0, 0, i)),
            pl.BlockSpec((tm, d), lambda i, e: (i, 0)),
            pl.BlockSpec((1, 6, d), lambda i, e: (_tile_group(i, n_p, tpb), 0, 0)),
            pl.BlockSpec((1, d), lambda i, e: (0, 0)),
            pl.BlockSpec((1, d), lambda i, e: (0, 0)),
        ],
        out_specs=pl.BlockSpec((tm, d), lambda i, e: (i, 0)),
        out_shape=jax.ShapeDtypeStruct((t, d), F32),
        scratch_shapes=[pltpu.VMEM((d, tm), F32), pltpu.VMEM((te, tm), BF16), pltpu.VMEM((te, tm), BF16),
                        pltpu.VMEM((te, tm), BF16)],
        compiler_params=_params(("arbitrary", "arbitrary")),
    )(h2t, h2t, e_u, e_u, e_vt, ni, ai, rj, bj, x, mods_l, ln_g.reshape(1, d), ln_b.reshape(1, d))


def _rope_tables(n_tok):
    rows = n_tok // GRID_W
    row = jnp.repeat(jnp.arange(rows), GRID_W).astype(F32)
    col = jnp.tile(jnp.arange(GRID_W), rows).astype(F32)
    n_freq = A_HD // 4
    inv = ROPE_THETA ** (-jnp.arange(n_freq, dtype=F32) / n_freq)
    ang = jnp.concatenate([row[:, None] * inv, col[:, None] * inv], -1)
    cos = jnp.repeat(jnp.cos(ang), 2, axis=-1)
    sin = jnp.repeat(jnp.sin(ang), 2, axis=-1)
    sign = jnp.tile(jnp.asarray([-1.0, 1.0], F32), A_HD // 2)
    reps = LANES // A_HD
    return jnp.tile(cos, (1, reps)), jnp.tile(sin * sign, (1, reps))


def _gqa_weights(w_qkv, q_norm, k_norm, w_o):
    d = w_qkv.shape[0]
    nq = A_HEADS * A_HD
    half = (np.arange(A_HEADS) // A_REP) % 2
    sel = np.stack([half == 0, half == 1], axis=1).astype(np.float32)
    wq = w_qkv[:, :nq].reshape(d, A_HEADS, 1, A_HD) * sel[None, :, :, None]
    w_exp = jnp.concatenate([wq.reshape(d, A_HEADS * LANES), w_qkv[:, nq:]], axis=1).astype(BF16)
    gq = jnp.tile(q_norm, 2 * A_HEADS).reshape(1, -1)
    gk = jnp.tile(k_norm, A_KV).reshape(1, -1)
    wo = w_o.reshape(A_HEADS, 1, A_HD, d) * sel[:, :, None, None]
    return w_exp, gq, gk, wo.reshape(A_HEADS * LANES, d).astype(BF16)


def _diff_weights(w_qkv):
    d = w_qkv.shape[0]
    eye = np.eye(2, dtype=np.float32)
    wq = w_qkv[:, :D_W].reshape(d, D_HEADS, 1, 2, D_HD) * eye[None, None, :, :, None]
    return jnp.concatenate([wq.reshape(d, D_HEADS * 2 * LANES), w_qkv[:, D_W:]], axis=1).astype(BF16)


def _split_f32(w):
    hi = w.astype(BF16)
    return hi, (w - hi.astype(F32)).astype(BF16)


def kernel(x_prompt, x_sample, cache_a_k, cache_a_v, cache_d_k, cache_d_v, c, c_ctx, w_mod, b_mod, ln_g, ln_b, a_w_qkv, a_q_norm, a_k_norm, a_w_o, d_w_qkv, d_lambda_q1, d_lambda_k1, d_lambda_q2, d_lambda_k2, d_sub_norm, d_w_o, g_w_in, g_b_in, g_v_norm_g, g_v_norm_b, g_w_s, g_b_s, g_w_out, p_w_q, p_sub_keys, p_expert_u, p_expert_v):
    batch, seq, d = x_prompt.shape
    dec_batch, dec_seq, _ = x_sample.shape
    past = cache_a_k.shape[2]
    t_p = batch * seq
    t_s = dec_batch * dec_seq
    dims = dict(t_p=t_p, s_len=dec_seq)

    x = (x_prompt.reshape(t_p, d), x_sample.reshape(t_s, d))
    e_u_all = p_expert_u.astype(BF16)
    e_vt_all = jnp.swapaxes(p_expert_v, 1, 2).astype(BF16)
    n_cond = -(-(1 + dec_batch) // 8) * 8
    cond = jnp.zeros((n_cond, d), F32).at[0].set(c_ctx).at[1:1 + dec_batch].set(c)
    mods = _modulation(cond, w_mod, b_mod).reshape(DEPTH, n_cond, 6, d)
    rope_c, rope_s = _rope_tables(dec_seq)

    ak, av, dk, dv = [], [], [], []
    for i in range(DEPTH):
        kind, j = i % N_MIXERS, i // N_MIXERS
        mods_l = mods[i]
        if kind == 0:
            w_exp, gq, gk, wo = _gqa_weights(a_w_qkv[j], a_q_norm[j], a_k_norm[j], a_w_o[j])
            q, k_att, v_att, k_f, v_f = _qkv_project(
                x, mods_l, w_exp, gq, gk, rope_c, rope_s, nq=A_HEADS, nk=A_KV * A_HD // LANES,
                nv=A_KV * A_HD // LANES, use_norm=True, q_scale=A_HD ** -0.5, **dims)
            ctx_k = cache_a_k[:, j].reshape(dec_batch, past, A_KV * A_HD).astype(BF16)
            ctx_v = cache_a_v[:, j].reshape(dec_batch, past, A_KV * A_HD).astype(BF16)
            o = _attention(_gqa_attn_kernel, (), q, k_att, v_att, ctx_k, ctx_v, n_out=A_HEADS * LANES,
                           t_p=t_p, seq_p=seq, batch_s=dec_batch, seq_s=dec_seq)
            ak.append(k_f[:t_p].reshape(batch, seq, A_KV, A_HD))
            av.append(v_f[:t_p].reshape(batch, seq, A_KV, A_HD))
        elif kind == 1:
            lam_init = 0.8 - 0.6 * math.exp(-0.3 * i)
            w_exp = _diff_weights(d_w_qkv[j])
            q, k_att, v_att, k_f, v_f = _qkv_project(
                x, mods_l, w_exp, jnp.ones((1, 2 * D_W), F32), jnp.ones((1, D_W), F32),
                rope_c, rope_s, nq=2 * D_HEADS, nk=D_HEADS, nv=D_HEADS,
                use_norm=False, q_scale=D_HD ** -0.5, **dims)
            ctx_k = cache_d_k[:, j].reshape(dec_batch, past, D_W).astype(BF16)
            ctx_v = cache_d_v[:, j].reshape(dec_batch, past, D_HEADS * D_VD).astype(BF16)
            lvec = jnp.stack([d_lambda_q1[j], d_lambda_k1[j], d_lambda_q2[j], d_lambda_k2[j]])
            kern = functools.partial(_diff_attn_kernel, lam_init=lam_init)
            o = _attention(kern, (lvec, d_sub_norm[j].reshape(1, D_VD)), q, k_att, v_att, ctx_k, ctx_v,
                           n_out=D_HEADS * D_VD, t_p=t_p, seq_p=seq, batch_s=dec_batch, seq_s=dec_seq)
            wo = d_w_o[j].astype(BF16)
            dk.append(k_f[:t_p].reshape(batch, seq, D_HEADS, 2, D_HD))
            dv.append(v_f[:t_p].reshape(batch, seq, D_HEADS, D_VD))
        if kind == 2:
            if isinstance(x, tuple):
                x = jnp.concatenate(x, axis=0)
            x = _gmlp(x, mods_l, g_w_in[j], g_b_in[j], g_v_norm_g[j], g_v_norm_b[j], g_w_s[j], g_b_s[j],
                      g_w_out[j], ln_g[i, 0], ln_b[i, 0], **dims)
        else:
            x = _proj_ln(*o, wo, x, mods_l, ln_g[i, 0], ln_b[i, 0], gate_idx=2, **dims)

        wq_hi, wq_lo = _split_f32(p_w_q[i].T)
        keys = p_sub_keys[i].reshape(P_HEADS * 2, N_KEYS, P_HALF)
        k_hi, k_lo = _split_f32(keys)
        h2t, ni, ai, rj, bj = _peer_topk(x, mods_l, wq_hi, wq_lo, k_hi, k_lo, **dims)
        x = _peer_dense(h2t, e_u_all, e_vt_all, i, ni, ai, rj, bj, x, mods_l, ln_g[i, 1], ln_b[i, 1], **dims)

    y_prompt = x[:t_p].reshape(batch, seq, d)
    y_sample = x[t_p:].reshape(dec_batch, dec_seq, d)
    return (y_prompt, y_sample, jnp.stack(ak, axis=1), jnp.stack(av, axis=1),
            jnp.stack(dk, axis=1), jnp.stack(dv, axis=1))
```
